```python
import math
import jax
import jax.numpy as jnp
from jax import lax
import numpy as np


D_MODEL = 4096
BATCH = 1
SEQ = 16384
DEPTH = 4

GRID_W = 64
CTX_LEN = 256
CHUNK = 128
ROWS_PER_CHUNK = CHUNK // GRID_W
EPS = 1e-6

ADA_RANK = 256
N_MOD = 6

N_BRANCH = 4
BRANCH_W = 1024
GATE_RANK = 512

SSD_HEADS = 16
SSD_HEAD_DIM = 64
SSD_INNER = SSD_HEADS * SSD_HEAD_DIM
SSD_GROUPS = 4
SSD_HPG = SSD_HEADS // SSD_GROUPS
SSD_STATE = 64
SSD_XBC = SSD_INNER + 2 * SSD_GROUPS * SSD_STATE
CONV_K = 5

FOURIER_GROUPS = 4
FOURIER_W = BRANCH_W

MLSTM_HEADS = 8
MLSTM_QK = 64
MLSTM_V = 128

SGU_GROUPS = 4
SGU_W = BRANCH_W

N_EXPERTS = 16
N_EXPERT_GROUPS = 4
EXPERTS_PER_GROUP = N_EXPERTS // N_EXPERT_GROUPS
TOP_K = 2
D_EXPERT = 640

IN_SPLITS = (SSD_INNER, SSD_XBC, 2 * SSD_HEADS,
             FOURIER_W,
             MLSTM_HEADS * MLSTM_QK, MLSTM_HEADS * MLSTM_QK,
             MLSTM_HEADS * MLSTM_V, MLSTM_HEADS * MLSTM_V,
             4 * MLSTM_HEADS,
             SGU_W, SGU_W,
             GATE_RANK)
D_IN = sum(IN_SPLITS)
IN_OFFSETS = [int(v) for v in np.cumsum(IN_SPLITS)[:-1]]

F32 = jnp.float32

kernel_name = 'hybrid_ssd_fnet_mlstm_gmlp_moe_dit'


def rmsnorm(x, w):
    xf = x.astype(F32)
    y = xf * lax.rsqrt(jnp.mean(xf * xf, axis=-1, keepdims=True) + EPS)
    return (y * w.astype(F32)).astype(x.dtype)


def modulate(xn, shift, scale):
    return xn * (1 + scale[:, None, :]) + shift[:, None, :]


def ada_mod(cvec, down, up, bias):
    m = (jax.nn.silu(cvec) @ down) @ up + bias
    return m.reshape(cvec.shape[0], N_MOD, D_MODEL)


def flip(a):
    return jnp.flip(a, axis=1)


def tri_mask():
    return jnp.tril(jnp.ones((CHUNK, CHUNK), dtype=bool))


def dwconv(x, w, b):
    pad = CONV_K // 2
    y = lax.conv_general_dilated(x, w[:, None, :].astype(x.dtype), window_strides=(1,),
                                 padding=[(pad, pad)], dimension_numbers=('NWC', 'WIO', 'NWC'),
                                 feature_group_count=x.shape[-1])
    return y + b.astype(x.dtype)


def ssd_scan(xs, dt, a_log, bm, cm, h0, with_output):
    b, L, H, P = xs.shape
    G, N = bm.shape[2], bm.shape[3]
    J = H // G
    nc = L // CHUNK
    x_ = xs.astype(F32).reshape(b, nc, CHUNK, G, J, P)
    dt_ = dt.astype(F32).reshape(b, nc, CHUNK, G, J)
    B_ = bm.astype(F32).reshape(b, nc, CHUNK, G, N)
    C_ = cm.astype(F32).reshape(b, nc, CHUNK, G, N)
    A = -jnp.exp(a_log.astype(F32)).reshape(G, J)
    acum = jnp.cumsum(dt_ * A, axis=2)
    a_last = acum[:, :, -1]
    xdt = x_ * dt_[..., None]
    states = jnp.einsum('bcsgn,bcsgj,bcsgjp->bcgjpn', B_, jnp.exp(a_last[:, :, None] - acum), xdt)

    def step(h, inp):
        dec, st = inp
        return dec[..., None, None] * h + st, h

    h_fin, h_in = lax.scan(step, h0, (jnp.moveaxis(jnp.exp(a_last), 1, 0), jnp.moveaxis(states, 1, 0)))
    if not with_output:
        return None, h_fin
    h_in = jnp.moveaxis(h_in, 0, 1)
    seg = acum[:, :, :, None] - acum[:, :, None, :]
    lmat = jnp.exp(jnp.where(tri_mask()[:, :, None, None], seg, -jnp.inf))
    cb = jnp.einsum('bctgn,bcsgn->bctsg', C_, B_)
    y = jnp.einsum('bctsg,bctsgj,bcsgjp->bctgjp', cb, lmat, xdt)
    y = y + jnp.einsum('bctgn,bctgj,bcgjpn->bctgjp', C_, jnp.exp(acum), h_in)
    return y.reshape(b, L, H, P).astype(xs.dtype), h_fin


def mlstm_scan(q, k, v, ig, lf, state0, with_output):
    b, L, H, dk = q.shape
    dv = v.shape[-1]
    nc = L // CHUNK
    q_ = q.astype(F32).reshape(b, nc, CHUNK, H, dk)
    k_ = k.astype(F32).reshape(b, nc, CHUNK, H, dk) * (dk ** -0.5)
    v_ = v.astype(F32).reshape(b, nc, CHUNK, H, dv)
    ig_ = ig.astype(F32).reshape(b, nc, CHUNK, H)
    bcum = jnp.cumsum(lf.astype(F32).reshape(b, nc, CHUNK, H), axis=2)
    b_last = bcum[:, :, -1]
    w_end = b_last[:, :, None] - bcum + ig_
    m_loc = jnp.max(w_end, axis=2)
    e_end = jnp.exp(w_end - m_loc[:, :, None])
    c_loc = jnp.einsum('bcsh,bcshk,bcshv->bchkv', e_end, k_, v_)
    n_loc = jnp.einsum('bcsh,bcshk->bchk', e_end, k_)

    def step(carry, inp):
        C, n, m = carry
        bl, ml, cl, nl = inp
        m_new = jnp.maximum(bl + m, ml)
        a = jnp.exp(bl + m - m_new)
        g = jnp.exp(ml - m_new)
        new = (a[..., None, None] * C + g[..., None, None] * cl, a[..., None] * n + g[..., None] * nl, m_new)
        return new, (C, n, m)

    final, (c_in, n_in, m_in) = lax.scan(
        step, state0, (jnp.moveaxis(b_last, 1, 0), jnp.moveaxis(m_loc, 1, 0),
                       jnp.moveaxis(c_loc, 1, 0), jnp.moveaxis(n_loc, 1, 0)))
    if not with_output:
        return None, final
    c_in = jnp.moveaxis(c_in, 0, 1)
    n_in = jnp.moveaxis(n_in, 0, 1)
    m_in = jnp.moveaxis(m_in, 0, 1)
    dmat = bcum[:, :, :, None, :] - bcum[:, :, None, :, :] + ig_[:, :, None, :, :]
    dmat = jnp.where(tri_mask()[:, :, None], dmat, -jnp.inf)
    w_inter = bcum + m_in[:, :, None]
    m_t = jnp.maximum(w_inter, jnp.max(dmat, axis=3))
    s = jnp.einsum('bcthk,bcshk->bctsh', q_, k_) * jnp.exp(dmat - m_t[:, :, :, None, :])
    e_inter = jnp.exp(w_inter - m_t)
    num = jnp.einsum('bctsh,bcshv->bcthv', s, v_) + e_inter[..., None] * jnp.einsum('bcthk,bchkv->bcthv', q_, c_in)
    den = jnp.sum(s, axis=3) + e_inter * jnp.einsum('bcthk,bchk->bcth', q_, n_in)
    h = num / jnp.maximum(jnp.abs(den), jnp.exp(-m_t))[..., None]
    return h.reshape(b, L, H, dv).astype(q.dtype), final


def token_mixers(h, lp, init, n_chunks, with_output):
    b, L, _ = h.shape
    proj = h @ lp['w_in']
    z, xbc, dt_raw, fx, mq, mk, mv, mo, mg, su, sv, gd = jnp.split(proj, IN_OFFSETS, axis=-1)
    ssd_f0, ssd_b0, ml_f0, ml_b0 = init

    xbc = jax.nn.silu(dwconv(xbc, lp['ssd_conv_w'], lp['ssd_conv_b']))
    xs, bm, cm = jnp.split(xbc, [SSD_INNER, SSD_INNER + SSD_GROUPS * SSD_STATE], axis=-1)
    xs = xs.reshape(b, L, SSD_HEADS, SSD_HEAD_DIM)
    bm = bm.reshape(b, L, SSD_GROUPS, SSD_STATE)
    cm = cm.reshape(b, L, SSD_GROUPS, SSD_STATE)
    dt = jax.nn.softplus(dt_raw.astype(F32).reshape(b, L, 2, SSD_HEADS) + lp['ssd_dt_bias'].astype(F32))
    ys_f, ssd_f = ssd_scan(xs, dt[:, :, 0], lp['ssd_a_log'][0], bm, cm, ssd_f0, with_output)
    ys_b, ssd_b = ssd_scan(flip(xs), flip(dt[:, :, 1]), lp['ssd_a_log'][1], flip(bm), flip(cm), ssd_b0, with_output)

    qk = jax.nn.silu(dwconv(jnp.concatenate([mq, mk], axis=-1), lp['mlstm_conv_w'], lp['mlstm_conv_b']))
    q, k = jnp.split(qk, 2, axis=-1)
    q = q.reshape(b, L, MLSTM_HEADS, MLSTM_QK)
    k = k.reshape(b, L, MLSTM_HEADS, MLSTM_QK)
    v = mv.reshape(b, L, MLSTM_HEADS, MLSTM_V)
    gates = mg.astype(F32).reshape(b, L, 4, MLSTM_HEADS) + lp['mlstm_gate_b'].astype(F32)
    hm_f, ml_f = mlstm_scan(q, k, v, gates[:, :, 0], jax.nn.log_sigmoid(gates[:, :, 1]), ml_f0, with_output)
    hm_b, ml_b = mlstm_scan(flip(q), flip(k), flip(v), flip(gates[:, :, 2]),
                            flip(jax.nn.log_sigmoid(gates[:, :, 3])), ml_b0, with_output)
    states = (ssd_f, ssd_b, ml_f, ml_b)
    if not with_output:
        return None, states

    y = ys_f + flip(ys_b) + lp['ssd_d'][:, None].astype(xs.dtype) * xs
    y_ssd = rmsnorm(y.reshape(b, L, SSD_INNER) * jax.nn.silu(z), lp['ssd_norm_w'])

    xf = fx.astype(F32).reshape(b, L, FOURIER_GROUPS, FOURIER_W // FOURIER_GROUPS)
    y_four = jnp.real(jnp.fft.fftn(xf, axes=(1, 3), norm='ortho')).reshape(b, L, FOURIER_W).astype(h.dtype)

    hm = rmsnorm(hm_f + flip(hm_b), lp['mlstm_norm_w'].reshape(MLSTM_HEADS, MLSTM_V))
    y_ml = jax.nn.sigmoid(mo) * hm.reshape(b, L, MLSTM_HEADS * MLSTM_V)

    su = jax.nn.gelu(su)
    svn = rmsnorm(jax.nn.gelu(sv), lp['sgu_norm_w'])
    svg = svn.reshape(b, n_chunks, CHUNK, SGU_GROUPS, SGU_W // SGU_GROUPS)
    mix = jnp.einsum('gts,bcsgd->bctgd', lp['sgu_w'], svg) + lp['sgu_b'].T[:, :, None]
    y_sgu = su * mix.reshape(b, L, SGU_W)

    branches = (y_ssd, y_four, y_ml, y_sgu)
    merged = sum(jax.nn.sigmoid(gd @ lp['w_gate_up'][n]) * (branches[n] @ lp['w_branch'][n])
                 for n in range(N_BRANCH))
    return merged @ lp['w_out'], states


def moe(h, w_router, router_bias, w_exp_in, w_exp_out):
    b, L, D = h.shape
    t = h.reshape(b * L, D)
    aff = jax.nn.sigmoid((t @ w_router).astype(F32))
    sel = aff + router_bias.astype(F32)
    gscore = jnp.sum(lax.top_k(sel.reshape(-1, N_EXPERT_GROUPS, EXPERTS_PER_GROUP), TOP_K)[0], axis=-1)
    gmask = jax.nn.one_hot(jnp.argmax(gscore, axis=-1), N_EXPERT_GROUPS, dtype=jnp.bool_)
    sel = jnp.where(jnp.repeat(gmask, EXPERTS_PER_GROUP, axis=-1), sel, -jnp.inf)
    _, e_idx = lax.top_k(sel, TOP_K)
    w = jnp.take_along_axis(aff, e_idx, axis=-1)
    w = w / jnp.sum(w, axis=-1, keepdims=True)
    dense_w = jnp.sum(jax.nn.one_hot(e_idx, N_EXPERTS, dtype=F32) * w[..., None], axis=1).astype(t.dtype)
    out = jnp.zeros_like(t)
    for e in range(N_EXPERTS):
        g, u = jnp.split(t @ w_exp_in[e], 2, axis=-1)
        out = out + dense_w[:, e:e + 1] * ((jax.nn.silu(g) * u) @ w_exp_out[e])
    return out.reshape(b, L, D)


def zero_states(b):
    ssd0 = jnp.zeros((b, SSD_GROUPS, SSD_HPG, SSD_HEAD_DIM, SSD_STATE), F32)
    ml0 = (jnp.zeros((b, MLSTM_HEADS, MLSTM_QK, MLSTM_V), F32),
           jnp.zeros((b, MLSTM_HEADS, MLSTM_QK), F32),
           jnp.zeros((b, MLSTM_HEADS), F32))
    return (ssd0, ssd0, ml0, ml0)


def setup_inputs(seed: int = 0) -> dict:
    key = jax.random.key(seed)
    ks = jax.random.split(key, 40)

    def nrm(k, shape, scale):
        return jax.random.normal(k, shape, F32) * scale

    def gain(k, shape):
        return 1.0 + 0.02 * jax.random.normal(k, shape, F32)

    dt0 = jnp.exp(jax.random.uniform(ks[12], (DEPTH, 2, SSD_HEADS), F32, math.log(1e-3), math.log(1e-1)))
    gate_base = jnp.array([0.0, 3.0, 0.0, 3.0], F32)[None, :, None]
    return {
        'x': nrm(ks[0], (BATCH, SEQ, D_MODEL), 1.0),
        'c': nrm(ks[1], (BATCH, D_MODEL), 1.0),
        'ctx': nrm(ks[2], (BATCH, CTX_LEN, D_MODEL), 1.0),
        'c_ctx': nrm(ks[3], (D_MODEL,), 1.0),
        'ada_down': nrm(ks[4], (DEPTH, D_MODEL, ADA_RANK), D_MODEL ** -0.5),
        'ada_up': nrm(ks[5], (DEPTH, ADA_RANK, N_MOD * D_MODEL), 0.3 * ADA_RANK ** -0.5),
        'ada_b': nrm(ks[6], (DEPTH, N_MOD * D_MODEL), 0.02),
        'norm1_w': gain(ks[7], (DEPTH, D_MODEL)),
        'norm2_w': gain(ks[8], (DEPTH, D_MODEL)),
        'w_in': nrm(ks[9], (DEPTH, D_MODEL, D_IN), D_MODEL ** -0.5),
        'ssd_conv_w': nrm(ks[10], (DEPTH, CONV_K, SSD_XBC), CONV_K ** -0.5),
        'ssd_conv_b': nrm(ks[11], (DEPTH, SSD_XBC), 0.02),
        'ssd_dt_bias': dt0 + jnp.log(-jnp.expm1(-dt0)),
        'ssd_a_log': jnp.log(jax.random.uniform(ks[13], (DEPTH, 2, SSD_HEADS), F32, 1.0, 16.0)),
        'ssd_d': gain(ks[14], (DEPTH, SSD_HEADS)),
        'ssd_norm_w': gain(ks[15], (DEPTH, SSD_INNER)),
        'mlstm_conv_w': nrm(ks[16], (DEPTH, CONV_K, 2 * MLSTM_HEADS * MLSTM_QK), CONV_K ** -0.5),
        'mlstm_conv_b': nrm(ks[17], (DEPTH, 2 * MLSTM_HEADS * MLSTM_QK), 0.02),
        'mlstm_gate_b': gate_base + nrm(ks[18], (DEPTH, 4, MLSTM_HEADS), 0.1),
        'mlstm_norm_w': gain(ks[19], (DEPTH, MLSTM_HEADS * MLSTM_V)),
        'sgu_norm_w': gain(ks[20], (DEPTH, SGU_W)),
        'sgu_w': nrm(ks[21], (DEPTH, SGU_GROUPS, CHUNK, CHUNK), CHUNK ** -0.5),
        'sgu_b': gain(ks[22], (DEPTH, SGU_GROUPS, CHUNK)),
        'w_gate_up': nrm(ks[23], (DEPTH, N_BRANCH, GATE_RANK, D_MODEL), GATE_RANK ** -0.5),
        'w_branch': nrm(ks[24], (DEPTH, N_BRANCH, BRANCH_W, D_MODEL), BRANCH_W ** -0.5),
        'w_out': nrm(ks[25], (DEPTH, D_MODEL, D_MODEL), D_MODEL ** -0.5),
        'w_router': nrm(ks[26], (D_MODEL, N_EXPERTS), D_MODEL ** -0.5),
        'router_bias': nrm(ks[27], (N_EXPERTS,), 0.01),
        'w_exp_in': nrm(ks[28], (DEPTH, N_EXPERTS, D_MODEL, 2 * D_EXPERT), D_MODEL ** -0.5),
        'w_exp_out': nrm(ks[29], (DEPTH, N_EXPERTS, D_EXPERT, D_MODEL), D_EXPERT ** -0.5),
        'final_norm_w': gain(ks[30], (D_MODEL,)),
    }


def reference(x, c, ctx, c_ctx, ada_down, ada_up, ada_b, norm1_w, norm2_w, w_in,
              ssd_conv_w, ssd_conv_b, ssd_dt_bias, ssd_a_log, ssd_d, ssd_norm_w,
              mlstm_conv_w, mlstm_conv_b, mlstm_gate_b, mlstm_norm_w,
              sgu_norm_w, sgu_w, sgu_b, w_gate_up, w_branch, w_out,
              w_router, router_bias, w_exp_in, w_exp_out, final_norm_w):
    b, seq, _ = x.shape
    rows = seq // GRID_W
    lat_chunks = rows // ROWS_PER_CHUNK
    ctx_chunks = ctx.shape[1] // CHUNK
    xc = ctx
    for l in range(DEPTH):
        last = l == DEPTH - 1
        lp = {'w_in': w_in[l], 'ssd_conv_w': ssd_conv_w[l], 'ssd_conv_b': ssd_conv_b[l],
              'ssd_dt_bias': ssd_dt_bias[l], 'ssd_a_log': ssd_a_log[l], 'ssd_d': ssd_d[l],
              'ssd_norm_w': ssd_norm_w[l], 'mlstm_conv_w': mlstm_conv_w[l],
              'mlstm_conv_b': mlstm_conv_b[l], 'mlstm_gate_b': mlstm_gate_b[l],
              'mlstm_norm_w': mlstm_norm_w[l], 'sgu_norm_w': sgu_norm_w[l], 'sgu_w': sgu_w[l],
              'sgu_b': sgu_b[l], 'w_gate_up': w_gate_up[l], 'w_branch': w_branch[l], 'w_out': w_out[l]}
        mod = ada_mod(c, ada_down[l], ada_up[l], ada_b[l])
        modc = ada_mod(c_ctx[None], ada_down[l], ada_up[l], ada_b[l])

        hc = modulate(rmsnorm(xc, norm1_w[l]), modc[:, 0], modc[:, 1])
        yc, ctx_states = token_mixers(hc, lp, zero_states(b), ctx_chunks, not last)
        hl = modulate(rmsnorm(x, norm1_w[l]), mod[:, 0], mod[:, 1])
        yl, _ = token_mixers(hl, lp, ctx_states, lat_chunks, True)
        x = x + mod[:, 2][:, None] * yl

        x = x + mod[:, 5][:, None] * moe(modulate(rmsnorm(x, norm2_w[l]), mod[:, 3], mod[:, 4]),
                                         w_router, router_bias, w_exp_in[l], w_exp_out[l])
        if not last:
            xc = xc + modc[:, 2][:, None] * yc
            xc = xc + modc[:, 5][:, None] * moe(modulate(rmsnorm(xc, norm2_w[l]), modc[:, 3], modc[:, 4]),
                                                w_router, router_bias, w_exp_in[l], w_exp_out[l])
    return rmsnorm(x, final_norm_w)
```

```python
import functools
import math

import numpy as np
import jax
import jax.numpy as jnp
from jax import lax
from jax.experimental import pallas as pl
from jax.experimental.pallas import tpu as pltpu

F32 = jnp.float32
BF16 = jnp.bfloat16
HIGHEST = lax.Precision.HIGHEST

D_MODEL = 4096
CHUNK = 128
EPS = 1e-6
ADA_RANK = 256
N_MOD = 6
MOD_ROWS = 8
N_BRANCH = 4
BRANCH_W = 1024
GATE_RANK = 512
SSD_HEADS = 16
SSD_HEAD_DIM = 64
SSD_INNER = SSD_HEADS * SSD_HEAD_DIM
SSD_GROUPS = 4
SSD_HPG = SSD_HEADS // SSD_GROUPS
SSD_STATE = 64
SSD_XBC = SSD_INNER + 2 * SSD_GROUPS * SSD_STATE
CONV_K = 5
CONV_PAD = CONV_K // 2
FOURIER_GROUPS = 4
FOURIER_GW = BRANCH_W // FOURIER_GROUPS
MLSTM_HEADS = 8
MLSTM_QK = 64
MLSTM_V = 128
MLSTM_QKW = 2 * MLSTM_HEADS * MLSTM_QK
SGU_GROUPS = 4
SGU_GW = BRANCH_W // SGU_GROUPS
N_EXPERTS = 16
N_EXPERT_GROUPS = 4
EXPERTS_PER_GROUP = N_EXPERTS // N_EXPERT_GROUPS
TOP_K = 2
D_EXPERT = 640

IN_SPLITS = (SSD_INNER, SSD_XBC, 2 * SSD_HEADS, BRANCH_W, MLSTM_HEADS * MLSTM_QK, MLSTM_HEADS * MLSTM_QK,
             MLSTM_HEADS * MLSTM_V, MLSTM_HEADS * MLSTM_V, 4 * MLSTM_HEADS, BRANCH_W, BRANCH_W, GATE_RANK)
_OFF = [0] + [int(v) for v in np.cumsum(IN_SPLITS)]
(O_Z, O_XBC, O_DT, O_FX, O_MQ, O_MK, O_MV, O_MO, O_MG, O_SU, O_SV, O_GD) = _OFF[:-1]

P_XBC = 0
P_GD = SSD_XBC
P_Z = P_GD + GATE_RANK
P_FX = P_Z + BRANCH_W
P_QK = P_FX + BRANCH_W
P_MV = P_QK + MLSTM_QKW
P_MO = P_MV + BRANCH_W
P_SU = P_MO + BRANCH_W
P_SV = P_SU + BRANCH_W
P_TOTAL = P_SV + BRANCH_W
GATE_LANES = 128
G_DT = 0
G_MG = 2 * SSD_HEADS

HALO = 16
VMEM_LIMIT = 56 * 1024 * 1024

ROW_TILE = 256
MM_TM = 1280
MM_TN = 512
EXPERT_TM = 256


def _cparams(sem):
    return pltpu.CompilerParams(dimension_semantics=sem, vmem_limit_bytes=VMEM_LIMIT)


def _pick_tile(n, pref):
    t = min(pref, n)
    while n % t:
        t -= ROW_TILE if t > ROW_TILE else 8
    return t


def _sigmoid(x):
    return 1.0 / (1.0 + jnp.exp(-x))


def _silu(x):
    return x * _sigmoid(x)


def _softplus(x):
    return jnp.maximum(x, 0.0) + jnp.log(1.0 + jnp.exp(-jnp.abs(x)))


def _log_sigmoid(x):
    return -_softplus(-x)


def _gelu_tanh(x):
    return 0.5 * x * (1.0 + jnp.tanh(math.sqrt(2.0 / math.pi) * (x + 0.044715 * (x * x * x))))


def _dot(a, b):
    return jnp.dot(a, b, preferred_element_type=F32)


def _dot_nt(a, b):
    return lax.dot_general(a, b, (((1,), (1,)), ((), ())), preferred_element_type=F32)


def _dot_hi(a, b):
    return jnp.dot(a, b, preferred_element_type=F32, precision=HIGHEST)


def _row_is_ctx(i, tm, n_ctx):
    row = i * tm + lax.broadcasted_iota(jnp.int32, (tm, 1), 0)
    return row < n_ctx


def _mod_row(mods_ref, is_ctx, idx):
    return jnp.where(is_ctx, mods_ref[0, idx:idx + 1, :], mods_ref[1, idx:idx + 1, :])


def _ada_kernel(c_ref, down_ref, up_ref, b_ref, o_ref):
    low = _dot_hi(_silu(c_ref[...]), down_ref[...])
    o_ref[...] = _dot_hi(low, up_ref[...]) + b_ref[...]


def ada_mods(cvec, ada_down, ada_up, ada_b):
    depth = ada_down.shape[0]
    n_out = ada_up.shape[2]
    tn = 2048
    return pl.pallas_call(
        _ada_kernel,
        out_shape=jax.ShapeDtypeStruct((depth, 8, n_out), F32),
        grid=(depth, n_out // tn),
        in_specs=[pl.BlockSpec((8, D_MODEL), lambda l, j: (0, 0)),
                  pl.BlockSpec((None, D_MODEL, ADA_RANK), lambda l, j: (l, 0, 0)),
                  pl.BlockSpec((None, ADA_RANK, tn), lambda l, j: (l, 0, j)),
                  pl.BlockSpec((None, 1, tn), lambda l, j: (l, 0, j))],
        out_specs=pl.BlockSpec((None, 8, tn), lambda l, j: (l, 0, j)),
        compiler_params=_cparams(("arbitrary", "arbitrary")),
        name="ada_mods",
    )(cvec, ada_down, ada_up, ada_b.reshape(depth, 1, n_out))


def _norm_kernel(x_ref, w_ref, mods_ref, o_ref, *, tm, n_ctx, shift_idx, scale_idx, row_off):
    x = x_ref[...]
    y = x * lax.rsqrt(jnp.mean(x * x, axis=-1, keepdims=True) + EPS) * w_ref[...]
    if shift_idx is not None:
        is_ctx = _row_is_ctx(pl.program_id(0) + row_off, tm, n_ctx)
        y = y * (1.0 + _mod_row(mods_ref, is_ctx, scale_idx)) + _mod_row(mods_ref, is_ctx, shift_idx)
    o_ref[...] = y.astype(o_ref.dtype)


def norm_mod(x, w, mods, n_ctx, shift_idx, scale_idx, out_dtype, row_start=0):
    t = x.shape[0] - row_start
    tm = ROW_TILE
    row_off = row_start // tm
    return pl.pallas_call(
        functools.partial(_norm_kernel, tm=tm, n_ctx=n_ctx, shift_idx=shift_idx, scale_idx=scale_idx,
                          row_off=row_off),
        out_shape=jax.ShapeDtypeStruct((t, D_MODEL), out_dtype),
        grid=(t // tm,),
        in_specs=[pl.BlockSpec((tm, D_MODEL), lambda i: (i + row_off, 0)),
                  pl.BlockSpec((1, D_MODEL), lambda i: (0, 0)),
                  pl.BlockSpec((2, MOD_ROWS, D_MODEL), lambda i: (0, 0, 0))],
        out_specs=pl.BlockSpec((tm, D_MODEL), lambda i: (i, 0)),
        compiler_params=_cparams(("arbitrary",)),
        name="norm_mod",
    )(x, w.reshape(1, D_MODEL), mods)


def _mm_kernel(a_ref, b_ref, o_ref):
    o_ref[...] = _dot(a_ref[...], b_ref[...]).astype(o_ref.dtype)


def matmul(a, b, out_dtype, tn):
    m, k = a.shape
    n = b.shape[1]
    tm = _pick_tile(m, MM_TM)
    return pl.pallas_call(
        _mm_kernel,
        out_shape=jax.ShapeDtypeStruct((m, n), out_dtype),
        grid=(m // tm, n // tn),
        in_specs=[pl.BlockSpec((tm, k), lambda i, j: (i, 0)),
                  pl.BlockSpec((k, tn), lambda i, j: (0, j))],
        out_specs=pl.BlockSpec((tm, tn), lambda i, j: (i, j)),
        compiler_params=_cparams(("arbitrary", "arbitrary")),
        name="matmul",
    )(a, b)


def _conv_silu(cur_ref, prev_ref, next_ref, cw_ref, cb_ref, ext_ref, first, last):
    prev = jnp.where(first, 0.0, prev_ref[...].astype(F32))
    nxt = jnp.where(last, 0.0, next_ref[...].astype(F32))
    ext_ref[0:HALO, :] = prev
    ext_ref[HALO:HALO + CHUNK, :] = cur_ref[...].astype(F32)
    ext_ref[HALO + CHUNK:2 * HALO + CHUNK, :] = nxt
    acc = cb_ref[...] + cw_ref[0:1, :] * ext_ref[pl.ds(HALO - CONV_PAD, CHUNK), :]
    for k in range(1, CONV_K):
        acc = acc + cw_ref[k:k + 1, :] * ext_ref[pl.ds(HALO - CONV_PAD + k, CHUNK), :]
    return _silu(acc)


def _tri_masks():
    r = lax.broadcasted_iota(jnp.int32, (CHUNK, CHUNK), 0)
    c = lax.broadcasted_iota(jnp.int32, (CHUNK, CHUNK), 1)
    return c <= r, c >= r


def _cumsums(col, row, reverse):
    lower, upper = _tri_masks()
    lo = lower.astype(F32)
    up = upper.astype(F32)
    if reverse:
        return _dot_hi(up, col), _dot_hi(row, lo)
    return _dot_hi(lo, col), _dot_hi(row, up)


def _chunk_order(n_ctx_chunks, n_chunks, reverse):
    if not reverse:
        return jnp.arange(n_chunks, dtype=jnp.int32)
    return jnp.concatenate([jnp.arange(n_ctx_chunks - 1, -1, -1, dtype=jnp.int32),
                            jnp.arange(n_chunks - 1, n_ctx_chunks - 1, -1, dtype=jnp.int32)])


def _halo_specs(width, col_block, n_rows):
    per = CHUNK // HALO
    last_halo = n_rows // HALO - 1
    cur = pl.BlockSpec((CHUNK, width), lambda c, o: (o[c], col_block))
    prev = pl.BlockSpec((HALO, width), lambda c, o: (jnp.maximum(o[c] * per - 1, 0), col_block))
    nxt = pl.BlockSpec((HALO, width), lambda c, o: (jnp.minimum((o[c] + 1) * per, last_halo), col_block))
    return [cur, prev, nxt]


def _ssd_kernel(order_ref, xbc_ref, xprev_ref, xnext_ref, gates_ref, cw_ref, cb_ref, gbias_row_ref,
                alog_row_ref, alog_col_ref, gbias_col_ref, expand_ref, *rest,
                reverse, final, n_ctx_chunks, n_chunks):
    if final:
        yb_ref, z_ref, dskip_ref, normw_ref, o_ref, ext_ref, h_ref = rest
    else:
        o_ref, ext_ref, h_ref = rest
    c = pl.program_id(0)
    ci = order_ref[c]
    first = jnp.logical_or(ci == 0, ci == n_ctx_chunks)
    last = jnp.logical_or(ci == n_ctx_chunks - 1, ci == n_chunks - 1)

    @pl.when(c == 0)
    def _():
        h_ref[...] = jnp.zeros_like(h_ref)

    xbc = _conv_silu(xbc_ref, xprev_ref, xnext_ref, cw_ref, cb_ref, ext_ref, first, last)
    xs = xbc[:, :SSD_INNER]
    bm = xbc[:, SSD_INNER:SSD_INNER + SSD_GROUPS * SSD_STATE]
    cm = xbc[:, SSD_INNER + SSD_GROUPS * SSD_STATE:]

    lane0 = SSD_HEADS if reverse else 0
    g = gates_ref[...]
    dt_col = _softplus(g + gbias_row_ref[...])
    dt_row = _softplus(g.T + gbias_col_ref[...])
    acum_col, acum_row = _cumsums(dt_col * -jnp.exp(alog_row_ref[...]),
                                  dt_row * -jnp.exp(alog_col_ref[...]), reverse)
    expand = expand_ref[...]
    dt_e = _dot_hi(dt_col, expand)
    acum_e = _dot_hi(acum_col, expand)
    alast_e = acum_e[0:1, :] if reverse else acum_e[CHUNK - 1:CHUNK, :]
    xdt = xs * dt_e
    xdt_b = xdt.astype(BF16)
    xw_b = (xdt * jnp.exp(alast_e - acum_e)).astype(BF16)
    dec_e = jnp.exp(alast_e)
    lower, upper = _tri_masks()
    mask = upper if reverse else lower
    cm_b = cm.astype(BF16)
    bm_b = bm.astype(BF16)
    bm_t = bm.T.astype(BF16)
    lane = lax.broadcasted_iota(jnp.int32, (CHUNK, 2 * SSD_HEAD_DIM), 1)
    gw = SSD_HPG * SSD_HEAD_DIM
    ys, inters = [], []
    for grp in range(SSD_GROUPS):
        c_g = cm_b[:, grp * SSD_STATE:(grp + 1) * SSD_STATE]
        b_g = bm_b[:, grp * SSD_STATE:(grp + 1) * SSD_STATE]
        cb = _dot_nt(c_g, b_g)
        h_t = h_ref[grp]
        inters.append(_dot(c_g, h_t.astype(BF16)))
        for pair in range(SSD_HPG // 2):
            ps = []
            for k in range(2):
                f = lane0 + grp * SSD_HPG + pair * 2 + k
                seg = acum_col[:, f:f + 1] - acum_row[f:f + 1, :]
                ps.append((cb * jnp.exp(jnp.where(mask, seg, -jnp.inf))).astype(BF16))
            col0 = (grp * (SSD_HPG // 2) + pair) * 2 * SSD_HEAD_DIM
            xp = xdt_b[:, col0:col0 + 2 * SSD_HEAD_DIM]
            rhs = jnp.concatenate([jnp.where(lane < SSD_HEAD_DIM, xp, jnp.zeros_like(xp)),
                                   jnp.where(lane >= SSD_HEAD_DIM, xp, jnp.zeros_like(xp))], axis=0)
            ys.append(_dot(jnp.concatenate(ps, axis=1), rhs))
        new = _dot(bm_t[grp * SSD_STATE:(grp + 1) * SSD_STATE, :], xw_b[:, grp * gw:(grp + 1) * gw])
        h_ref[grp] = dec_e[:, grp * gw:(grp + 1) * gw] * h_t + new
    y = jnp.concatenate(ys, axis=1) + jnp.exp(acum_e) * jnp.concatenate(inters, axis=1)
    if final:
        y = y + yb_ref[...].astype(F32) + dskip_ref[...] * xs
        z = z_ref[...].astype(F32)
        y = y * _silu(z)
        y = y * lax.rsqrt(jnp.mean(y * y, axis=-1, keepdims=True) + EPS) * normw_ref[...]
    o_ref[...] = y.astype(o_ref.dtype)


def ssd_pass(proj, gates, lw, n_ctx, reverse, y_other=None):
    t = proj.shape[0]
    n_chunks = t // CHUNK
    n_ctx_chunks = n_ctx // CHUNK
    final = y_other is not None
    order = _chunk_order(n_ctx_chunks, n_chunks, reverse)
    const = lambda c, o: (0, 0)
    chunk = lambda c, o: (o[c], 0)
    in_specs = _halo_specs(SSD_XBC, P_XBC // SSD_XBC, t) + [
        pl.BlockSpec((CHUNK, GATE_LANES), chunk),
        pl.BlockSpec((8, SSD_XBC), const),
        pl.BlockSpec((1, SSD_XBC), const),
        pl.BlockSpec((1, GATE_LANES), const),
        pl.BlockSpec((1, GATE_LANES), const),
        pl.BlockSpec((GATE_LANES, 1), const),
        pl.BlockSpec((GATE_LANES, 1), const),
        pl.BlockSpec((GATE_LANES, SSD_INNER), const)]
    args = [proj, proj, proj, gates, lw['ssd_conv_w'], lw['ssd_conv_b'], lw['gbias_row'], lw['alog_row'],
            lw['alog_col'], lw['gbias_col'], lw['expand_b' if reverse else 'expand_f']]
    if final:
        in_specs += [pl.BlockSpec((CHUNK, SSD_INNER), chunk),
                     pl.BlockSpec((CHUNK, SSD_INNER), lambda c, o: (o[c], P_Z // SSD_INNER)),
                     pl.BlockSpec((1, SSD_INNER), const),
                     pl.BlockSpec((1, SSD_INNER), const)]
        args += [y_other, proj, lw['ssd_d_e'], lw['ssd_norm_w']]
    return pl.pallas_call(
        functools.partial(_ssd_kernel, reverse=reverse, final=final, n_ctx_chunks=n_ctx_chunks,
                          n_chunks=n_chunks),
        out_shape=jax.ShapeDtypeStruct((t, SSD_INNER), BF16),
        grid_spec=pltpu.PrefetchScalarGridSpec(
            num_scalar_prefetch=1, grid=(n_chunks,), in_specs=in_specs,
            out_specs=pl.BlockSpec((CHUNK, SSD_INNER), chunk),
            scratch_shapes=[pltpu.VMEM((CHUNK + 2 * HALO, SSD_XBC), F32),
                            pltpu.VMEM((SSD_GROUPS, SSD_STATE, SSD_HPG * SSD_HEAD_DIM), F32)]),
        compiler_params=_cparams(("arbitrary",)),
        name="ssd_bwd" if reverse else "ssd_fwd",
    )(order, *args)


def _mlstm_kernel(order_ref, qk_ref, qprev_ref, qnext_ref, v_ref, gates_ref, cw_ref, cb_ref, gbias_row_ref,
                  gbias_col_ref, *rest, reverse, final, n_ctx_chunks, n_chunks):
    if final:
        hb_ref, mo_ref, normw_ref, o_ref, ext_ref, cn_ref, m_ref = rest
    else:
        o_ref, ext_ref, cn_ref, m_ref = rest
    c = pl.program_id(0)
    ci = order_ref[c]
    first = jnp.logical_or(ci == 0, ci == n_ctx_chunks)
    last = jnp.logical_or(ci == n_ctx_chunks - 1, ci == n_chunks - 1)

    @pl.when(c == 0)
    def _():
        cn_ref[...] = jnp.zeros_like(cn_ref)
        m_ref[...] = jnp.zeros_like(m_ref)

    qk = _conv_silu(qk_ref, qprev_ref, qnext_ref, cw_ref, cb_ref, ext_ref, first, last)
    hq = MLSTM_HEADS * MLSTM_QK
    q_b = qk[:, :hq].astype(BF16)
    k = qk[:, hq:] * (MLSTM_QK ** -0.5)
    k_b = k.astype(BF16)
    k_t = k.T
    v_b = v_ref[...]

    icol = G_MG + (2 * MLSTM_HEADS if reverse else 0)
    fcol = icol + MLSTM_HEADS
    g_col = gates_ref[...] + gbias_row_ref[...]
    g_row = gates_ref[...].T + gbias_col_ref[...]
    bcum_col, bcum_row = _cumsums(_log_sigmoid(g_col), _log_sigmoid(g_row), reverse)
    lower, upper = _tri_masks()
    mask = upper if reverse else lower
    ones_col = (lax.broadcasted_iota(jnp.int32, (CHUNK, MLSTM_V), 1) == 0).astype(BF16)

    for h in range(MLSTM_HEADS):
        bc = bcum_col[:, fcol + h:fcol + h + 1]
        br = bcum_row[fcol + h:fcol + h + 1, :]
        ig = g_row[icol + h:icol + h + 1, :]
        dm = jnp.where(mask, bc - br + ig, -jnp.inf)
        m_in = m_ref[h:h + 1, 0:1]
        w_inter = bc + m_in
        m_t = jnp.maximum(w_inter, jnp.max(dm, axis=1, keepdims=True))
        q_h = q_b[:, h * MLSTM_QK:(h + 1) * MLSTM_QK]
        s = _dot_nt(q_h, k_b[:, h * MLSTM_QK:(h + 1) * MLSTM_QK]) * jnp.exp(dm - m_t)
        v_ext = jnp.concatenate([v_b[:, h * MLSTM_V:(h + 1) * MLSTM_V], ones_col], axis=1)
        cn = cn_ref[h]
        tot = _dot(s.astype(BF16), v_ext) + jnp.exp(w_inter - m_t) * _dot(q_h, cn.astype(BF16))
        num = tot[:, :MLSTM_V]
        den = tot[:, MLSTM_V:MLSTM_V + 1]
        hh = num / jnp.maximum(jnp.abs(den), jnp.exp(-m_t))

        b_last = br[:, 0:1] if reverse else br[:, CHUNK - 1:CHUNK]
        w_end = b_last - br + ig
        m_loc = jnp.max(w_end, axis=1, keepdims=True)
        kw_t = (k_t[h * MLSTM_QK:(h + 1) * MLSTM_QK, :] * jnp.exp(w_end - m_loc)).astype(BF16)
        m_new = jnp.maximum(b_last + m_in, m_loc)
        cn_ref[h] = jnp.exp(b_last + m_in - m_new) * cn + jnp.exp(m_loc - m_new) * _dot(kw_t, v_ext)
        m_ref[h:h + 1, :] = jnp.broadcast_to(m_new, (1, GATE_LANES))

        sl = slice(h * MLSTM_V, (h + 1) * MLSTM_V)
        if final:
            hh = hh + hb_ref[:, sl].astype(F32)
            hh = hh * lax.rsqrt(jnp.mean(hh * hh, axis=-1, keepdims=True) + EPS) * normw_ref[:, sl]
            hh = _sigmoid(mo_ref[:, sl].astype(F32)) * hh
        o_ref[:, sl] = hh.astype(o_ref.dtype)


def mlstm_pass(proj, gates, lw, n_ctx, reverse, h_other=None):
    t = proj.shape[0]
    n_chunks = t // CHUNK
    n_ctx_chunks = n_ctx // CHUNK
    final = h_other is not None
    order = _chunk_order(n_ctx_chunks, n_chunks, reverse)
    const = lambda c, o: (0, 0)
    chunk = lambda c, o: (o[c], 0)
    in_specs = _halo_specs(MLSTM_QKW, P_QK // MLSTM_QKW, t) + [
        pl.BlockSpec((CHUNK, BRANCH_W), lambda c, o: (o[c], P_MV // BRANCH_W)),
        pl.BlockSpec((CHUNK, GATE_LANES), chunk),
        pl.BlockSpec((8, MLSTM_QKW), const),
        pl.BlockSpec((1, MLSTM_QKW), const),
        pl.BlockSpec((1, GATE_LANES), const),
        pl.BlockSpec((GATE_LANES, 1), const)]
    args = [proj, proj, proj, proj, gates, lw['mlstm_conv_w'], lw['mlstm_conv_b'], lw['gbias_row'],
            lw['gbias_col']]
    if final:
        in_specs += [pl.BlockSpec((CHUNK, BRANCH_W), chunk),
                     pl.BlockSpec((CHUNK, BRANCH_W), lambda c, o: (o[c], P_MO // BRANCH_W)),
                     pl.BlockSpec((1, BRANCH_W), const)]
        args += [h_other, proj, lw['mlstm_norm_w']]
    return pl.pallas_call(
        functools.partial(_mlstm_kernel, reverse=reverse, final=final, n_ctx_chunks=n_ctx_chunks,
                          n_chunks=n_chunks),
        out_shape=jax.ShapeDtypeStruct((t, BRANCH_W), BF16),
        grid_spec=pltpu.PrefetchScalarGridSpec(
            num_scalar_prefetch=1, grid=(n_chunks,), in_specs=in_specs,
            out_specs=pl.BlockSpec((CHUNK, BRANCH_W), chunk),
            scratch_shapes=[pltpu.VMEM((CHUNK + 2 * HALO, MLSTM_QKW), F32),
                            pltpu.VMEM((MLSTM_HEADS, MLSTM_QK, 2 * MLSTM_V), F32),
                            pltpu.VMEM((MLSTM_HEADS, GATE_LANES), F32)]),
        compiler_params=_cparams(("arbitrary",)),
        name="mlstm_bwd" if reverse else "mlstm_fwd",
    )(order, *args)


def _fft_split(length):
    l1 = 1 << (int(math.log2(length)) // 2)
    return l1, length // l1


def _fft1_kernel(x_ref, cs_ref, f1_ref, twc_ref, tws_ref, ure_ref, uim_ref, *, l1):
    x = x_ref[...]
    cs = cs_ref[...]
    a, b = [], []
    for grp in range(FOURIER_GROUPS):
        ab = _dot(x[:, grp * FOURIER_GW:(grp + 1) * FOURIER_GW], cs)
        a.append(ab[:, :FOURIER_GW])
        b.append(ab[:, FOURIER_GW:])
    ab = jnp.concatenate(a + b, axis=1).astype(BF16)
    prod = _dot(f1_ref[...], ab)
    c_a = prod[:l1, :BRANCH_W]
    c_b = prod[:l1, BRANCH_W:]
    s_a = prod[l1:, :BRANCH_W]
    s_b = prod[l1:, BRANCH_W:]
    u_re = c_a - s_b
    u_im = -(c_b + s_a)
    twc = twc_ref[...]
    tws = tws_ref[...]
    ure_ref[...] = (u_re * twc + u_im * tws).astype(ure_ref.dtype)
    uim_ref[...] = (u_im * twc - u_re * tws).astype(uim_ref.dtype)


def _fft2_kernel(ure_ref, uim_ref, f2_ref, o_ref):
    u = jnp.concatenate([ure_ref[...], uim_ref[...]], axis=0)
    o_ref[...] = _dot(f2_ref[...], u).astype(o_ref.dtype)


def _fft_tables(length):
    l1, l2 = _fft_split(length)
    scale = 1.0 / math.sqrt(length * FOURIER_GW)
    kc = np.arange(FOURIER_GW)
    ang_c = 2.0 * np.pi * np.outer(kc, kc) / FOURIER_GW
    cs = np.concatenate([np.cos(ang_c), np.sin(ang_c)], axis=1) * scale
    k1 = np.arange(l1)
    ang1 = 2.0 * np.pi * np.outer(k1, k1) / l1
    f1 = np.concatenate([np.cos(ang1), np.sin(ang1)], axis=0)
    n2 = np.arange(l2)
    ang_t = 2.0 * np.pi * np.outer(n2, k1) / length
    ang2 = 2.0 * np.pi * np.outer(n2, n2) / l2
    f2 = np.concatenate([np.cos(ang2), np.sin(ang2)], axis=1)
    return (jnp.asarray(cs, BF16), jnp.asarray(f1, BF16), jnp.asarray(np.cos(ang_t)[:, :, None], F32),
            jnp.asarray(np.sin(ang_t)[:, :, None], F32), jnp.asarray(f2, BF16))


def fourier_mix(fx):
    length = fx.shape[0]
    l1, l2 = _fft_split(length)
    cs, f1, twc, tws, f2 = _fft_tables(length)
    u_shape = jax.ShapeDtypeStruct((l1, l2 * BRANCH_W), BF16)
    ure, uim = pl.pallas_call(
        functools.partial(_fft1_kernel, l1=l1),
        out_shape=(u_shape, u_shape),
        grid=(l2,),
        in_specs=[pl.BlockSpec((l1, BRANCH_W), lambda j: (0, j)),
                  pl.BlockSpec((FOURIER_GW, 2 * FOURIER_GW), lambda j: (0, 0)),
                  pl.BlockSpec((2 * l1, l1), lambda j: (0, 0)),
                  pl.BlockSpec((None, l1, 1), lambda j: (j, 0, 0)),
                  pl.BlockSpec((None, l1, 1), lambda j: (j, 0, 0))],
        out_specs=(pl.BlockSpec((l1, BRANCH_W), lambda j: (0, j)),
                   pl.BlockSpec((l1, BRANCH_W), lambda j: (0, j))),
        compiler_params=_cparams(("arbitrary",)),
        name="fft_stage1",
    )(fx.reshape(l1, l2 * BRANCH_W), cs, f1, twc, tws)
    out = pl.pallas_call(
        _fft2_kernel,
        out_shape=jax.ShapeDtypeStruct((l2, l1 * BRANCH_W), BF16),
        grid=(l1,),
        in_specs=[pl.BlockSpec((None, l2, BRANCH_W), lambda i: (i, 0, 0)),
                  pl.BlockSpec((None, l2, BRANCH_W), lambda i: (i, 0, 0)),
                  pl.BlockSpec((l2, 2 * l2), lambda i: (0, 0))],
        out_specs=pl.BlockSpec((l2, BRANCH_W), lambda i: (0, i)),
        compiler_params=_cparams(("arbitrary",)),
        name="fft_stage2",
    )(ure.reshape(l1, l2, BRANCH_W), uim.reshape(l1, l2, BRANCH_W), f2)
    return out.reshape(length, BRANCH_W)


def _sgu_kernel(su_ref, sv_ref, normw_ref, w_ref, b_ref, o_ref):
    v = _gelu_tanh(sv_ref[...].astype(F32))
    v = (v * lax.rsqrt(jnp.mean(v * v, axis=-1, keepdims=True) + EPS) * normw_ref[...]).astype(BF16)
    mix = []
    for grp in range(SGU_GROUPS):
        mix.append(_dot(w_ref[grp], v[:, grp * SGU_GW:(grp + 1) * SGU_GW]) + b_ref[:, grp:grp + 1])
    o_ref[...] = (_gelu_tanh(su_ref[...].astype(F32)) * jnp.concatenate(mix, axis=1)).astype(o_ref.dtype)


def sgu(proj, lw):
    t = proj.shape[0]
    return pl.pallas_call(
        _sgu_kernel,
        out_shape=jax.ShapeDtypeStruct((t, BRANCH_W), BF16),
        grid=(t // CHUNK,),
        in_specs=[pl.BlockSpec((CHUNK, BRANCH_W), lambda c: (c, P_SU // BRANCH_W)),
                  pl.BlockSpec((CHUNK, BRANCH_W), lambda c: (c, P_SV // BRANCH_W)),
                  pl.BlockSpec((1, BRANCH_W), lambda c: (0, 0)),
                  pl.BlockSpec((SGU_GROUPS, CHUNK, CHUNK), lambda c: (0, 0, 0)),
                  pl.BlockSpec((CHUNK, GATE_LANES), lambda c: (0, 0))],
        out_specs=pl.BlockSpec((CHUNK, BRANCH_W), lambda c: (c, 0)),
        compiler_params=_cparams(("arbitrary",)),
        name="sgu",
    )(proj, proj, lw['sgu_norm_w'], lw['sgu_w'], lw['sgu_b_t'])


def _merge_kernel(gd_ref, b0_ref, b1_ref, b2_ref, b3_ref, wg_ref, wb_ref, o_ref):
    gd = gd_ref[...]
    acc = None
    for n, b_ref in enumerate((b0_ref, b1_ref, b2_ref, b3_ref)):
        term = _sigmoid(_dot(gd, wg_ref[n])) * _dot(b_ref[...], wb_ref[n])
        acc = term if acc is None else acc + term
    o_ref[...] = acc.astype(o_ref.dtype)


def merge(proj, branches, lw):
    t = proj.shape[0]
    tm = _pick_tile(t, MM_TM)
    tn = MM_TN
    row = lambda i, j: (i, 0)
    return pl.pallas_call(
        _merge_kernel,
        out_shape=jax.ShapeDtypeStruct((t, D_MODEL), BF16),
        grid=(t // tm, D_MODEL // tn),
        in_specs=[pl.BlockSpec((tm, GATE_RANK), lambda i, j: (i, P_GD // GATE_RANK))] +
                 [pl.BlockSpec((tm, BRANCH_W), row)] * N_BRANCH +
                 [pl.BlockSpec((N_BRANCH, GATE_RANK, tn), lambda i, j: (0, 0, j)),
                  pl.BlockSpec((N_BRANCH, BRANCH_W, tn), lambda i, j: (0, 0, j))],
        out_specs=pl.BlockSpec((tm, tn), lambda i, j: (i, j)),
        compiler_params=_cparams(("arbitrary", "arbitrary")),
        name="merge",
    )(proj, *branches, lw['w_gate_up'], lw['w_branch'])


def _outproj_kernel(a_ref, w_ref, x_ref, mods_ref, o_ref, *, tm, n_ctx, gate_idx):
    is_ctx = _row_is_ctx(pl.program_id(0), tm, n_ctx)
    o_ref[...] = x_ref[...] + _mod_row(mods_ref, is_ctx, gate_idx) * _dot(a_ref[...], w_ref[...])


def outproj_residual(a, w, x, mods, n_ctx, gate_idx):
    t = a.shape[0]
    tm = _pick_tile(t, MM_TM)
    tn = MM_TN
    return pl.pallas_call(
        functools.partial(_outproj_kernel, tm=tm, n_ctx=n_ctx, gate_idx=gate_idx),
        out_shape=jax.ShapeDtypeStruct((t, D_MODEL), F32),
        grid=(t // tm, D_MODEL // tn),
        in_specs=[pl.BlockSpec((tm, D_MODEL), lambda i, j: (i, 0)),
                  pl.BlockSpec((D_MODEL, tn), lambda i, j: (0, j)),
                  pl.BlockSpec((tm, tn), lambda i, j: (i, j)),
                  pl.BlockSpec((2, MOD_ROWS, tn), lambda i, j: (0, 0, j))],
        out_specs=pl.BlockSpec((tm, tn), lambda i, j: (i, j)),
        compiler_params=_cparams(("arbitrary", "arbitrary")),
        name="outproj_residual",
    )(a, w, x, mods)


def _router_kernel(x_ref, w_ref, mods_ref, wr_ref, rb_ref, h_ref, idx_ref, wts_ref, *, tm, n_ctx):
    x = x_ref[...]
    is_ctx = _row_is_ctx(pl.program_id(0), tm, n_ctx)
    y = x * lax.rsqrt(jnp.mean(x * x, axis=-1, keepdims=True) + EPS) * w_ref[...]
    h = y * (1.0 + _mod_row(mods_ref, is_ctx, 4)) + _mod_row(mods_ref, is_ctx, 3)
    h_ref[...] = h
    logits = lax.dot_general(wr_ref[...], h, (((1,), (1,)), ((), ())), preferred_element_type=F32,
                             precision=HIGHEST)
    aff = _sigmoid(logits)
    sel = aff + rb_ref[...]
    rows = [sel[e:e + 1, :] for e in range(N_EXPERTS)]
    affs = [aff[e:e + 1, :] for e in range(N_EXPERTS)]
    gscore = []
    for grp in range(N_EXPERT_GROUPS):
        r = rows[grp * EXPERTS_PER_GROUP:(grp + 1) * EXPERTS_PER_GROUP]
        best = None
        for a in range(EXPERTS_PER_GROUP):
            for b in range(a + 1, EXPERTS_PER_GROUP):
                pair = r[a] + r[b]
                best = pair if best is None else jnp.maximum(best, pair)
        gscore.append(best)
    g_best = gscore[0]
    g_idx = jnp.zeros_like(g_best, dtype=jnp.int32)
    for grp in range(1, N_EXPERT_GROUPS):
        better = gscore[grp] > g_best
        g_best = jnp.where(better, gscore[grp], g_best)
        g_idx = jnp.where(better, grp, g_idx)
    masked = [jnp.where(g_idx == e // EXPERTS_PER_GROUP, rows[e], -jnp.inf) for e in range(N_EXPERTS)]

    def argbest(exclude):
        best_v = jnp.full_like(g_best, -jnp.inf)
        best_i = jnp.full_like(g_idx, -1)
        best_a = jnp.zeros_like(g_best)
        for e in range(N_EXPERTS):
            ok = masked[e] > best_v
            if exclude is not None:
                ok = jnp.logical_and(ok, exclude != e)
            best_v = jnp.where(ok, masked[e], best_v)
            best_i = jnp.where(ok, e, best_i)
            best_a = jnp.where(ok, affs[e], best_a)
        return best_i, best_a

    i1, a1 = argbest(None)
    i2, a2 = argbest(i1)
    tot = a1 + a2
    idx_ref[...] = jnp.concatenate([i1, i2], axis=0)
    wts_ref[...] = jnp.concatenate([a1 / tot, a2 / tot], axis=0)


def norm_router(x, norm_w, mods, w_router_t, router_bias, n_ctx):
    t = x.shape[0]
    tm = ROW_TILE
    return pl.pallas_call(
        functools.partial(_router_kernel, tm=tm, n_ctx=n_ctx),
        out_shape=(jax.ShapeDtypeStruct((t, D_MODEL), F32),
                   jax.ShapeDtypeStruct((TOP_K, t), jnp.int32),
                   jax.ShapeDtypeStruct((TOP_K, t), F32)),
        grid=(t // tm,),
        in_specs=[pl.BlockSpec((tm, D_MODEL), lambda i: (i, 0)),
                  pl.BlockSpec((1, D_MODEL), lambda i: (0, 0)),
                  pl.BlockSpec((2, MOD_ROWS, D_MODEL), lambda i: (0, 0, 0)),
                  pl.BlockSpec((N_EXPERTS, D_MODEL), lambda i: (0, 0)),
                  pl.BlockSpec((N_EXPERTS, 1), lambda i: (0, 0))],
        out_specs=(pl.BlockSpec((tm, D_MODEL), lambda i: (i, 0)),
                   pl.BlockSpec((TOP_K, tm), lambda i: (0, i)),
                   pl.BlockSpec((TOP_K, tm), lambda i: (0, i))),
        compiler_params=_cparams(("arbitrary",)),
        name="norm_router",
    )(x, norm_w.reshape(1, D_MODEL), mods, w_router_t, router_bias.reshape(N_EXPERTS, 1))


def _row_copy(src_hbm, dst_ref, sem, src_row, dst_row):
    return pltpu.make_async_copy(src_hbm.at[pl.ds(src_row, 1), :], dst_ref.at[pl.ds(dst_row, 1), :], sem)


def _experts_kernel(texp_ref, nused_ref, tok_ref, h_hbm, wrow_ref, win_ref, wout_ref, o_ref, xbuf, sem, *, tm):
    i = pl.program_id(0)

    @pl.when(i < nused_ref[0])
    def _():
        def start(r, carry):
            _row_copy(h_hbm, xbuf, sem, tok_ref[i * tm + r], r).start()
            return carry

        def wait(r, carry):
            _row_copy(h_hbm, xbuf, sem, 0, r).wait()
            return carry

        lax.fori_loop(0, tm, start, 0)
        lax.fori_loop(0, tm, wait, 0)
        gu = _dot(xbuf[...].astype(BF16), win_ref[...])
        act = (_silu(gu[:, :D_EXPERT]) * gu[:, D_EXPERT:]).astype(BF16)
        o_ref[...] = wrow_ref[...] * _dot(act, wout_ref[...])

    @pl.when(i >= nused_ref[0])
    def _():
        o_ref[...] = jnp.zeros_like(o_ref)


def experts(h, tile_expert, n_used, tok_sorted, w_sorted, w_exp_in, w_exp_out):
    tm = EXPERT_TM
    n_tiles = tile_expert.shape[0]
    return pl.pallas_call(
        functools.partial(_experts_kernel, tm=tm),
        out_shape=jax.ShapeDtypeStruct((n_tiles * tm, D_MODEL), F32),
        grid_spec=pltpu.PrefetchScalarGridSpec(
            num_scalar_prefetch=3, grid=(n_tiles,),
            in_specs=[pl.BlockSpec(memory_space=pl.ANY),
                      pl.BlockSpec((tm, 1), lambda i, te, nu, tok: (i, 0)),
                      pl.BlockSpec((None, D_MODEL, 2 * D_EXPERT), lambda i, te, nu, tok: (te[i], 0, 0)),
                      pl.BlockSpec((None, D_EXPERT, D_MODEL), lambda i, te, nu, tok: (te[i], 0, 0))],
            out_specs=pl.BlockSpec((tm, D_MODEL), lambda i, te, nu, tok: (i, 0)),
            scratch_shapes=[pltpu.VMEM((tm, D_MODEL), F32), pltpu.SemaphoreType.DMA]),
        compiler_params=_cparams(("arbitrary",)),
        name="experts",
    )(tile_expert, n_used, tok_sorted, h, w_sorted, w_exp_in, w_exp_out)


def _combine_kernel(pos_ref, x_ref, mods_ref, y_hbm, o_ref, ybuf, sem, *, tm, n_ctx, n_tok):
    i = pl.program_id(0)

    def start(r, carry):
        for k in range(TOP_K):
            _row_copy(y_hbm, ybuf.at[k], sem, pos_ref[k * n_tok + i * tm + r], r).start()
        return carry

    def wait(r, carry):
        for k in range(TOP_K):
            _row_copy(y_hbm, ybuf.at[k], sem, 0, r).wait()
        return carry

    lax.fori_loop(0, tm, start, 0)
    lax.fori_loop(0, tm, wait, 0)
    is_ctx = _row_is_ctx(i, tm, n_ctx)
    o_ref[...] = x_ref[...] + _mod_row(mods_ref, is_ctx, 5) * (ybuf[0] + ybuf[1])


def combine(x, y_sorted, pos, mods, n_ctx):
    t = x.shape[0]
    tm = CHUNK
    return pl.pallas_call(
        functools.partial(_combine_kernel, tm=tm, n_ctx=n_ctx, n_tok=t),
        out_shape=jax.ShapeDtypeStruct((t, D_MODEL), F32),
        grid_spec=pltpu.PrefetchScalarGridSpec(
            num_scalar_prefetch=1, grid=(t // tm,),
            in_specs=[pl.BlockSpec((tm, D_MODEL), lambda i, p: (i, 0)),
                      pl.BlockSpec((2, MOD_ROWS, D_MODEL), lambda i, p: (0, 0, 0)),
                      pl.BlockSpec(memory_space=pl.ANY)],
            out_specs=pl.BlockSpec((tm, D_MODEL), lambda i, p: (i, 0)),
            scratch_shapes=[pltpu.VMEM((TOP_K, tm, D_MODEL), F32), pltpu.SemaphoreType.DMA]),
        compiler_params=_cparams(("arbitrary",)),
        name="combine",
    )(pos, x, mods, y_sorted)


def _routing_metadata(idx, wts, n_tiles):
    tm = EXPERT_TM
    t = idx.shape[1]
    e_flat = idx.reshape(-1)
    onehot = (e_flat[:, None] == jnp.arange(N_EXPERTS, dtype=jnp.int32)[None, :]).astype(jnp.int32)
    rank = jnp.take_along_axis(jnp.cumsum(onehot, axis=0) - onehot, e_flat[:, None], axis=1)[:, 0]
    counts = jnp.sum(onehot, axis=0)
    padded = ((counts + tm - 1) // tm) * tm
    ends = jnp.cumsum(padded)
    starts = ends - padded
    pos = (starts[e_flat] + rank).astype(jnp.int32)
    tok = jnp.arange(TOP_K * t, dtype=jnp.int32) % t
    tok_sorted = jnp.zeros((n_tiles * tm,), jnp.int32).at[pos].set(tok)
    w_sorted = jnp.zeros((n_tiles * tm,), F32).at[pos].set(wts.reshape(-1))
    n_used = (ends[-1] // tm).astype(jnp.int32)
    tile_start = jnp.arange(n_tiles, dtype=jnp.int32) * tm
    tile_expert = jnp.minimum(jnp.searchsorted(ends, tile_start, side='right'), N_EXPERTS - 1).astype(jnp.int32)
    last_used = tile_expert[jnp.maximum(n_used - 1, 0)]
    tile_expert = jnp.where(jnp.arange(n_tiles) < n_used, tile_expert, last_used)
    return tile_expert, n_used.reshape(1), tok_sorted, w_sorted.reshape(-1, 1), pos


def _prep_layer_params(l, p):
    w_in = p['w_in'][l]
    cols = lambda o, n: w_in[:, o:o + n]
    w_main = jnp.concatenate([cols(O_XBC, SSD_XBC), cols(O_GD, GATE_RANK), cols(O_Z, SSD_INNER),
                              cols(O_FX, BRANCH_W), cols(O_MQ, MLSTM_QKW), cols(O_MV, BRANCH_W),
                              cols(O_MO, BRANCH_W), cols(O_SU, BRANCH_W), cols(O_SV, BRANCH_W)],
                             axis=1).astype(BF16)
    pad = GATE_LANES - 2 * SSD_HEADS - 4 * MLSTM_HEADS
    w_gates = jnp.concatenate([cols(O_DT, 2 * SSD_HEADS), cols(O_MG, 4 * MLSTM_HEADS),
                               jnp.zeros((D_MODEL, pad), F32)], axis=1).astype(BF16)
    gbias = jnp.concatenate([p['ssd_dt_bias'][l].reshape(-1), p['mlstm_gate_b'][l].reshape(-1),
                             jnp.zeros((pad,), F32)])
    alog = jnp.concatenate([p['ssd_a_log'][l].reshape(-1), jnp.zeros((GATE_LANES - 2 * SSD_HEADS,), F32)])
    feat = np.arange(GATE_LANES)[:, None]
    head = np.arange(SSD_INNER)[None, :] // SSD_HEAD_DIM
    pad_k = lambda w: jnp.concatenate([w, jnp.zeros((8 - CONV_K, w.shape[1]), F32)], axis=0)
    return {
        'w_main': w_main, 'w_gates': w_gates,
        'gbias_row': gbias.reshape(1, -1), 'gbias_col': gbias.reshape(-1, 1),
        'alog_row': alog.reshape(1, -1), 'alog_col': alog.reshape(-1, 1),
        'expand_f': jnp.asarray(feat == head, F32), 'expand_b': jnp.asarray(feat == head + SSD_HEADS, F32),
        'ssd_conv_w': pad_k(p['ssd_conv_w'][l]), 'ssd_conv_b': p['ssd_conv_b'][l].reshape(1, -1),
        'ssd_d_e': jnp.repeat(p['ssd_d'][l], SSD_HEAD_DIM).reshape(1, -1),
        'ssd_norm_w': p['ssd_norm_w'][l].reshape(1, -1),
        'mlstm_conv_w': pad_k(p['mlstm_conv_w'][l]), 'mlstm_conv_b': p['mlstm_conv_b'][l].reshape(1, -1),
        'mlstm_norm_w': p['mlstm_norm_w'][l].reshape(1, -1),
        'sgu_norm_w': p['sgu_norm_w'][l].reshape(1, -1),
        'sgu_w': p['sgu_w'][l].astype(BF16),
        'sgu_b_t': jnp.concatenate([p['sgu_b'][l].T, jnp.zeros((CHUNK, GATE_LANES - SGU_GROUPS), F32)], axis=1),
        'w_gate_up': p['w_gate_up'][l].astype(BF16), 'w_branch': p['w_branch'][l].astype(BF16),
        'w_out': p['w_out'][l].astype(BF16),
        'w_exp_in': p['w_exp_in'][l].astype(BF16), 'w_exp_out': p['w_exp_out'][l].astype(BF16),
    }


def kernel(x, c, ctx, c_ctx, ada_down, ada_up, ada_b, norm1_w, norm2_w, w_in, ssd_conv_w, ssd_conv_b,
           ssd_dt_bias, ssd_a_log, ssd_d, ssd_norm_w, mlstm_conv_w, mlstm_conv_b, mlstm_gate_b, mlstm_norm_w,
           sgu_norm_w, sgu_w, sgu_b, w_gate_up, w_branch, w_out, w_router, router_bias, w_exp_in, w_exp_out,
           final_norm_w):
    assert x.shape[0] == 1 and ctx.shape[0] == 1, "single-sequence kernel"
    p = dict(w_in=w_in, ssd_conv_w=ssd_conv_w, ssd_conv_b=ssd_conv_b, ssd_dt_bias=ssd_dt_bias,
             ssd_a_log=ssd_a_log, ssd_d=ssd_d, ssd_norm_w=ssd_norm_w, mlstm_conv_w=mlstm_conv_w,
             mlstm_conv_b=mlstm_conv_b, mlstm_gate_b=mlstm_gate_b, mlstm_norm_w=mlstm_norm_w,
             sgu_norm_w=sgu_norm_w, sgu_w=sgu_w, sgu_b=sgu_b, w_gate_up=w_gate_up, w_branch=w_branch,
             w_out=w_out, w_exp_in=w_exp_in, w_exp_out=w_exp_out)
    depth = w_in.shape[0]
    seq = x.shape[1]
    n_ctx = ctx.shape[1]
    t = n_ctx + seq
    assert n_ctx % ROW_TILE == 0 and seq % ROW_TILE == 0
    n_tiles = -(-(TOP_K * t) // EXPERT_TM) + N_EXPERTS

    xs = jnp.concatenate([ctx[0], x[0]], axis=0)
    cvec = jnp.concatenate([c_ctx[None], c, jnp.zeros((6, D_MODEL), F32)], axis=0)
    mods_all = ada_mods(cvec, ada_down, ada_up, ada_b)
    mods_all = mods_all[:, :2].reshape(depth, 2, N_MOD, D_MODEL)
    mods_all = jnp.concatenate([mods_all, jnp.zeros((depth, 2, MOD_ROWS - N_MOD, D_MODEL), F32)], axis=2)
    w_router_t = w_router.T

    for l in range(depth):
        lw = _prep_layer_params(l, p)
        mods = mods_all[l]
        h = norm_mod(xs, norm1_w[l], mods, n_ctx, 0, 1, BF16)
        proj = matmul(h, lw['w_main'], BF16, MM_TN)
        gates = matmul(h, lw['w_gates'], F32, GATE_LANES)

        ys_b = ssd_pass(proj, gates, lw, n_ctx, True)
        y_ssd = ssd_pass(proj, gates, lw, n_ctx, False, ys_b)
        hm_b = mlstm_pass(proj, gates, lw, n_ctx, True)
        y_ml = mlstm_pass(proj, gates, lw, n_ctx, False, hm_b)
        fx = proj[:, P_FX:P_FX + BRANCH_W]
        y_four = jnp.concatenate([fourier_mix(fx[:n_ctx]), fourier_mix(fx[n_ctx:])], axis=0)
        y_sgu = sgu(proj, lw)

        merged = merge(proj, (y_ssd, y_four, y_ml, y_sgu), lw)
        xs = outproj_residual(merged, lw['w_out'], xs, mods, n_ctx, 2)

        h2, idx, wts = norm_router(xs, norm2_w[l], mods, w_router_t, router_bias, n_ctx)
        tile_expert, n_used, tok_sorted, w_sorted, pos = _routing_metadata(idx, wts, n_tiles)
        y_sorted = experts(h2, tile_expert, n_used, tok_sorted, w_sorted, lw['w_exp_in'], lw['w_exp_out'])
        xs = combine(xs, y_sorted, pos, mods, n_ctx)

    out = norm_mod(xs, final_norm_w, mods_all[0], n_ctx, None, None, F32, row_start=n_ctx)
    return out[None]
```

```python
import functools
import math

import numpy as np
import jax
import jax.numpy as jnp
from jax import lax
from jax.experimental import pallas as pl
from jax.experimental.pallas import tpu as pltpu

F32 = jnp.float32
BF16 = jnp.bfloat16
HIGHEST = lax.Precision.HIGHEST

D_MODEL = 4096
CHUNK = 128
EPS = 1e-6
ADA_RANK = 256
N_MOD = 6
MOD_ROWS = 8
N_BRANCH = 4
BRANCH_W = 1024
GATE_RANK = 512
SSD_HEADS = 16
SSD_HEAD_DIM = 64
SSD_INNER = SSD_HEADS * SSD_HEAD_DIM
SSD_GROUPS = 4
SSD_HPG = SSD_HEADS // SSD_GROUPS
SSD_STATE = 64
SSD_XBC = SSD_INNER + 2 * SSD_GROUPS * SSD_STATE
CONV_K = 5
CONV_PAD = CONV_K // 2
FOURIER_GROUPS = 4
FOURIER_GW = BRANCH_W // FOURIER_GROUPS
MLSTM_HEADS = 8
MLSTM_QK = 64
MLSTM_V = 128
MLSTM_QKW = 2 * MLSTM_HEADS * MLSTM_QK
SGU_GROUPS = 4
SGU_GW = BRANCH_W // SGU_GROUPS
N_EXPERTS = 16
N_EXPERT_GROUPS = 4
EXPERTS_PER_GROUP = N_EXPERTS // N_EXPERT_GROUPS
TOP_K = 2
D_EXPERT = 640

IN_SPLITS = (SSD_INNER, SSD_XBC, 2 * SSD_HEADS, BRANCH_W, MLSTM_HEADS * MLSTM_QK, MLSTM_HEADS * MLSTM_QK,
             MLSTM_HEADS * MLSTM_V, MLSTM_HEADS * MLSTM_V, 4 * MLSTM_HEADS, BRANCH_W, BRANCH_W, GATE_RANK)
_OFF = [0] + [int(v) for v in np.cumsum(IN_SPLITS)]
(O_Z, O_XBC, O_DT, O_FX, O_MQ, O_MK, O_MV, O_MO, O_MG, O_SU, O_SV, O_GD) = _OFF[:-1]

P_XBC = 0
P_GD = SSD_XBC
P_Z = P_GD + GATE_RANK
P_FX = P_Z + BRANCH_W
P_QK = P_FX + BRANCH_W
P_MV = P_QK + MLSTM_QKW
P_MO = P_MV + BRANCH_W
P_SU = P_MO + BRANCH_W
P_SV = P_SU + BRANCH_W
P_TOTAL = P_SV + BRANCH_W
GATE_LANES = 128
G_DT = 0
G_MG = 2 * SSD_HEADS

HALO = 16
VMEM_LIMIT = 56 * 1024 * 1024

ROW_TILE = 256
MM_TM = 1280
MM_TN = 512
EXPERT_TM = 512


def _cparams(sem):
    return pltpu.CompilerParams(dimension_semantics=sem, vmem_limit_bytes=VMEM_LIMIT)


def _pick_tile(n, pref):
    t = min(pref, n)
    while n % t:
        t -= ROW_TILE if t > ROW_TILE else 8
    return t


def _sigmoid(x):
    return 1.0 / (1.0 + jnp.exp(-x))


def _silu(x):
    return x * _sigmoid(x)


def _softplus(x):
    return jnp.maximum(x, 0.0) + jnp.log(1.0 + jnp.exp(-jnp.abs(x)))


def _log_sigmoid(x):
    return -_softplus(-x)


def _gelu_tanh(x):
    return 0.5 * x * (1.0 + jnp.tanh(math.sqrt(2.0 / math.pi) * (x + 0.044715 * (x * x * x))))


def _dot(a, b):
    return jnp.dot(a, b, preferred_element_type=F32)


def _dot_nt(a, b):
    return lax.dot_general(a, b, (((1,), (1,)), ((), ())), preferred_element_type=F32)


def _dot_hi(a, b):
    return jnp.dot(a, b, preferred_element_type=F32, precision=HIGHEST)


def _split_bf16(x, parts=3):
    out, rest = [], x
    for _ in range(parts):
        piece = rest.astype(BF16)
        out.append(piece)
        rest = rest - piece.astype(F32)
    return out


def _dot_f32_by_exact(a, b_exact):
    terms = [_dot(piece, b_exact) for piece in _split_bf16(a)]
    return terms[0] + terms[1] + terms[2]


def _dot_exact_by_f32(a_exact, b):
    terms = [_dot(a_exact, piece) for piece in _split_bf16(b)]
    return terms[0] + terms[1] + terms[2]


HI_MASK = 0xFFFF0000


def _pack_halves(v):
    bits = lax.bitcast_convert_type(v.astype(BF16).astype(F32), jnp.uint32)
    n = v.shape[1] // 2
    return (bits[:, :n] >> 16) | (bits[:, n:] & jnp.uint32(HI_MASK))


def _unpack_lo(p):
    return lax.bitcast_convert_type(p << 16, F32)


def _unpack_hi(p):
    return lax.bitcast_convert_type(p & jnp.uint32(HI_MASK), F32)


def _row_is_ctx(i, tm, n_ctx):
    row = i * tm + lax.broadcasted_iota(jnp.int32, (tm, 1), 0)
    return row < n_ctx


def _mod_row(mods_ref, is_ctx, idx):
    return jnp.where(is_ctx, mods_ref[0, idx:idx + 1, :], mods_ref[1, idx:idx + 1, :])


def _ada_kernel(c_ref, down_ref, up_ref, b_ref, o_ref):
    low = _dot_hi(_silu(c_ref[...]), down_ref[...])
    o_ref[...] = _dot_hi(low, up_ref[...]) + b_ref[...]


def ada_mods(cvec, ada_down, ada_up, ada_b):
    depth = ada_down.shape[0]
    n_out = ada_up.shape[2]
    tn = 2048
    return pl.pallas_call(
        _ada_kernel,
        out_shape=jax.ShapeDtypeStruct((depth, 8, n_out), F32),
        grid=(depth, n_out // tn),
        in_specs=[pl.BlockSpec((8, D_MODEL), lambda l, j: (0, 0)),
                  pl.BlockSpec((None, D_MODEL, ADA_RANK), lambda l, j: (l, 0, 0)),
                  pl.BlockSpec((None, ADA_RANK, tn), lambda l, j: (l, 0, j)),
                  pl.BlockSpec((None, 1, tn), lambda l, j: (l, 0, j))],
        out_specs=pl.BlockSpec((None, 8, tn), lambda l, j: (l, 0, j)),
        compiler_params=_cparams(("arbitrary", "arbitrary")),
        name="ada_mods",
    )(cvec, ada_down, ada_up, ada_b.reshape(depth, 1, n_out))


def _norm_kernel(x_ref, w_ref, mods_ref, o_ref, *, tm, n_ctx, shift_idx, scale_idx, row_off):
    x = x_ref[...]
    y = x * lax.rsqrt(jnp.mean(x * x, axis=-1, keepdims=True) + EPS) * w_ref[...]
    if shift_idx is not None:
        is_ctx = _row_is_ctx(pl.program_id(0) + row_off, tm, n_ctx)
        y = y * (1.0 + _mod_row(mods_ref, is_ctx, scale_idx)) + _mod_row(mods_ref, is_ctx, shift_idx)
    o_ref[...] = y.astype(o_ref.dtype)


def norm_mod(x, w, mods, n_ctx, shift_idx, scale_idx, out_dtype, row_start=0):
    t = x.shape[0] - row_start
    tm = ROW_TILE
    row_off = row_start // tm
    return pl.pallas_call(
        functools.partial(_norm_kernel, tm=tm, n_ctx=n_ctx, shift_idx=shift_idx, scale_idx=scale_idx,
                          row_off=row_off),
        out_shape=jax.ShapeDtypeStruct((t, D_MODEL), out_dtype),
        grid=(t // tm,),
        in_specs=[pl.BlockSpec((tm, D_MODEL), lambda i: (i + row_off, 0)),
                  pl.BlockSpec((1, D_MODEL), lambda i: (0, 0)),
                  pl.BlockSpec((2, MOD_ROWS, D_MODEL), lambda i: (0, 0, 0))],
        out_specs=pl.BlockSpec((tm, D_MODEL), lambda i: (i, 0)),
        compiler_params=_cparams(("arbitrary",)),
        name="norm_mod",
    )(x, w.reshape(1, D_MODEL), mods)


def _mm_kernel(a_ref, b_ref, o_ref):
    o_ref[...] = _dot(a_ref[...], b_ref[...]).astype(o_ref.dtype)


def matmul(a, b, out_dtype, tn):
    m, k = a.shape
    n = b.shape[1]
    tm = _pick_tile(m, MM_TM)
    return pl.pallas_call(
        _mm_kernel,
        out_shape=jax.ShapeDtypeStruct((m, n), out_dtype),
        grid=(m // tm, n // tn),
        in_specs=[pl.BlockSpec((tm, k), lambda i, j: (i, 0)),
                  pl.BlockSpec((k, tn), lambda i, j: (0, j))],
        out_specs=pl.BlockSpec((tm, tn), lambda i, j: (i, j)),
        compiler_params=_cparams(("arbitrary", "arbitrary")),
        name="matmul",
    )(a, b)


def _conv_silu(cur_ref, prev_ref, next_ref, cw_ref, cb_ref, ext_ref, first, last):
    prev = jnp.where(first, 0.0, prev_ref[...].astype(F32))
    nxt = jnp.where(last, 0.0, next_ref[...].astype(F32))
    ext_ref[0:HALO, :] = prev
    ext_ref[HALO:HALO + CHUNK, :] = cur_ref[...].astype(F32)
    ext_ref[HALO + CHUNK:2 * HALO + CHUNK, :] = nxt
    acc = cb_ref[...] + cw_ref[0:1, :] * ext_ref[pl.ds(HALO - CONV_PAD, CHUNK), :]
    for k in range(1, CONV_K):
        acc = acc + cw_ref[k:k + 1, :] * ext_ref[pl.ds(HALO - CONV_PAD + k, CHUNK), :]
    return _silu(acc)


def _tri_masks():
    r = lax.broadcasted_iota(jnp.int32, (CHUNK, CHUNK), 0)
    c = lax.broadcasted_iota(jnp.int32, (CHUNK, CHUNK), 1)
    return c <= r, c >= r


def _cumsums(col, row, reverse):
    lower, upper = _tri_masks()
    lo = jnp.where(lower, 1.0, 0.0).astype(BF16)
    up = jnp.where(upper, 1.0, 0.0).astype(BF16)
    if reverse:
        return _dot_exact_by_f32(up, col), _dot_f32_by_exact(row, lo)
    return _dot_exact_by_f32(lo, col), _dot_f32_by_exact(row, up)


def _chunk_order(n_ctx_chunks, n_chunks, reverse):
    if not reverse:
        return jnp.arange(n_chunks, dtype=jnp.int32)
    return jnp.concatenate([jnp.arange(n_ctx_chunks - 1, -1, -1, dtype=jnp.int32),
                            jnp.arange(n_chunks - 1, n_ctx_chunks - 1, -1, dtype=jnp.int32)])


def _halo_specs(width, col_block, n_rows):
    per = CHUNK // HALO
    last_halo = n_rows // HALO - 1
    cur = pl.BlockSpec((CHUNK, width), lambda c, o: (o[c], col_block))
    prev = pl.BlockSpec((HALO, width), lambda c, o: (jnp.maximum(o[c] * per - 1, 0), col_block))
    nxt = pl.BlockSpec((HALO, width), lambda c, o: (jnp.minimum((o[c] + 1) * per, last_halo), col_block))
    return [cur, prev, nxt]


def _ssd_kernel(order_ref, *refs, reverse, final, n_ctx_chunks, n_chunks):
    if final:
        (xc_ref, gates_ref, gbias_row_ref, alog_row_ref, alog_col_ref, gbias_col_ref, expand_ref,
         yb_ref, z_ref, dskip_ref, normw_ref, o_ref, h_ref) = refs
    else:
        (xbc_ref, xprev_ref, xnext_ref, cw_ref, cb_ref, gates_ref, gbias_row_ref, alog_row_ref, alog_col_ref,
         gbias_col_ref, expand_ref, o_ref, xc_ref, ext_ref, h_ref) = refs
    c = pl.program_id(0)
    ci = order_ref[c]

    @pl.when(c == 0)
    def _():
        h_ref[...] = jnp.zeros_like(h_ref)

    if final:
        xbc = xc_ref[...].astype(F32)
    else:
        first = jnp.logical_or(ci == 0, ci == n_ctx_chunks)
        last = jnp.logical_or(ci == n_ctx_chunks - 1, ci == n_chunks - 1)
        xbc = _conv_silu(xbc_ref, xprev_ref, xnext_ref, cw_ref, cb_ref, ext_ref, first, last)
        xc_ref[...] = xbc.astype(xc_ref.dtype)
    xs = xbc[:, :SSD_INNER]
    bm = xbc[:, SSD_INNER:SSD_INNER + SSD_GROUPS * SSD_STATE]
    cm = xbc[:, SSD_INNER + SSD_GROUPS * SSD_STATE:]

    lane0 = SSD_HEADS if reverse else 0
    g = gates_ref[...]
    dt_col = _softplus(g + gbias_row_ref[...])
    dt_row = _softplus(g.T + gbias_col_ref[...])
    acum_col, acum_row = _cumsums(dt_col * -jnp.exp(alog_row_ref[...]),
                                  dt_row * -jnp.exp(alog_col_ref[...]), reverse)
    both_e = _dot_f32_by_exact(jnp.concatenate([dt_col, acum_col], axis=0), expand_ref[...])
    dt_e = both_e[:CHUNK]
    acum_e = both_e[CHUNK:]
    alast_e = acum_e[0:1, :] if reverse else acum_e[CHUNK - 1:CHUNK, :]
    xdt = xs * dt_e
    xdt_b = xdt.astype(BF16)
    xw_b = (xdt * jnp.exp(alast_e - acum_e)).astype(BF16)
    dec_e = jnp.exp(alast_e)
    lower, upper = _tri_masks()
    mask = upper if reverse else lower
    cm_b = cm.astype(BF16)
    bm_b = bm.astype(BF16)
    bm_t = bm.T.astype(BF16)
    lane = lax.broadcasted_iota(jnp.int32, (CHUNK, 2 * SSD_HEAD_DIM), 1)
    gw = SSD_HPG * SSD_HEAD_DIM
    ys, inters = [], []
    for grp in range(SSD_GROUPS):
        c_g = cm_b[:, grp * SSD_STATE:(grp + 1) * SSD_STATE]
        b_g = bm_b[:, grp * SSD_STATE:(grp + 1) * SSD_STATE]
        cb = _dot_nt(c_g, b_g)
        h_t = h_ref[grp]
        inters.append(_dot(c_g, h_t.astype(BF16)))
        for pair in range(SSD_HPG // 2):
            ps = []
            for k in range(2):
                f = lane0 + grp * SSD_HPG + pair * 2 + k
                seg = acum_col[:, f:f + 1] - acum_row[f:f + 1, :]
                ps.append((cb * jnp.exp(jnp.where(mask, seg, -jnp.inf))).astype(BF16))
            col0 = (grp * (SSD_HPG // 2) + pair) * 2 * SSD_HEAD_DIM
            xp = xdt_b[:, col0:col0 + 2 * SSD_HEAD_DIM]
            rhs = jnp.concatenate([jnp.where(lane < SSD_HEAD_DIM, xp, jnp.zeros_like(xp)),
                                   jnp.where(lane >= SSD_HEAD_DIM, xp, jnp.zeros_like(xp))], axis=0)
            ys.append(_dot(jnp.concatenate(ps, axis=1), rhs))
        new = _dot(bm_t[grp * SSD_STATE:(grp + 1) * SSD_STATE, :], xw_b[:, grp * gw:(grp + 1) * gw])
        h_ref[grp] = dec_e[:, grp * gw:(grp + 1) * gw] * h_t + new
    y = jnp.concatenate(ys, axis=1) + jnp.exp(acum_e) * jnp.concatenate(inters, axis=1)
    if final:
        y = y + yb_ref[...].astype(F32) + dskip_ref[...] * xs
        z = z_ref[...].astype(F32)
        y = y * _silu(z)
        y = y * lax.rsqrt(jnp.mean(y * y, axis=-1, keepdims=True) + EPS) * normw_ref[...]
    o_ref[...] = y.astype(o_ref.dtype)


def ssd_pass(proj, gates, lw, n_ctx, xbc_conv=None, y_other=None):
    t = proj.shape[0]
    n_chunks = t // CHUNK
    n_ctx_chunks = n_ctx // CHUNK
    final = xbc_conv is not None
    reverse = not final
    order = _chunk_order(n_ctx_chunks, n_chunks, reverse)
    const = lambda c, o: (0, 0)
    chunk = lambda c, o: (o[c], 0)
    gate_specs = [pl.BlockSpec((CHUNK, GATE_LANES), chunk),
                  pl.BlockSpec((1, GATE_LANES), const),
                  pl.BlockSpec((1, GATE_LANES), const),
                  pl.BlockSpec((GATE_LANES, 1), const),
                  pl.BlockSpec((GATE_LANES, 1), const),
                  pl.BlockSpec((GATE_LANES, SSD_INNER), const)]
    gate_args = [gates, lw['gbias_row'], lw['alog_row'], lw['alog_col'], lw['gbias_col'],
                 lw['expand_b' if reverse else 'expand_f']]
    y_shape = jax.ShapeDtypeStruct((t, SSD_INNER), BF16)
    y_spec = pl.BlockSpec((CHUNK, SSD_INNER), chunk)
    state = pltpu.VMEM((SSD_GROUPS, SSD_STATE, SSD_HPG * SSD_HEAD_DIM), F32)
    if final:
        in_specs = [pl.BlockSpec((CHUNK, SSD_XBC), chunk)] + gate_specs + [
            pl.BlockSpec((CHUNK, SSD_INNER), chunk),
            pl.BlockSpec((CHUNK, SSD_INNER), lambda c, o: (o[c], P_Z // SSD_INNER)),
            pl.BlockSpec((1, SSD_INNER), const),
            pl.BlockSpec((1, SSD_INNER), const)]
        args = [xbc_conv] + gate_args + [y_other, proj, lw['ssd_d_e'], lw['ssd_norm_w']]
        out_shape, out_specs, scratch = y_shape, y_spec, [state]
    else:
        in_specs = _halo_specs(SSD_XBC, P_XBC // SSD_XBC, t) + [
            pl.BlockSpec((8, SSD_XBC), const), pl.BlockSpec((1, SSD_XBC), const)] + gate_specs
        args = [proj, proj, proj, lw['ssd_conv_w'], lw['ssd_conv_b']] + gate_args
        out_shape = (y_shape, jax.ShapeDtypeStruct((t, SSD_XBC), BF16))
        out_specs = (y_spec, pl.BlockSpec((CHUNK, SSD_XBC), chunk))
        scratch = [pltpu.VMEM((CHUNK + 2 * HALO, SSD_XBC), F32), state]
    return pl.pallas_call(
        functools.partial(_ssd_kernel, reverse=reverse, final=final, n_ctx_chunks=n_ctx_chunks,
                          n_chunks=n_chunks),
        out_shape=out_shape,
        grid_spec=pltpu.PrefetchScalarGridSpec(
            num_scalar_prefetch=1, grid=(n_chunks,), in_specs=in_specs, out_specs=out_specs,
            scratch_shapes=scratch),
        compiler_params=_cparams(("arbitrary",)),
        name="ssd_bwd" if reverse else "ssd_fwd",
    )(order, *args)


def _mlstm_kernel(order_ref, *refs, reverse, final, n_ctx_chunks, n_chunks):
    if final:
        (qc_ref, v_ref, gates_ref, gbias_row_ref, gbias_col_ref, hb_ref, mo_ref, normw_ref,
         o_ref, cn_ref, m_ref) = refs
    else:
        (qk_ref, qprev_ref, qnext_ref, cw_ref, cb_ref, v_ref, gates_ref, gbias_row_ref, gbias_col_ref,
         o_ref, qc_ref, ext_ref, cn_ref, m_ref) = refs
    c = pl.program_id(0)
    ci = order_ref[c]

    @pl.when(c == 0)
    def _():
        cn_ref[...] = jnp.zeros_like(cn_ref)
        m_ref[...] = jnp.zeros_like(m_ref)

    if final:
        qk = qc_ref[...].astype(F32)
    else:
        first = jnp.logical_or(ci == 0, ci == n_ctx_chunks)
        last = jnp.logical_or(ci == n_ctx_chunks - 1, ci == n_chunks - 1)
        qk = _conv_silu(qk_ref, qprev_ref, qnext_ref, cw_ref, cb_ref, ext_ref, first, last)
        qc_ref[...] = qk.astype(qc_ref.dtype)
    hq = MLSTM_HEADS * MLSTM_QK
    q_b = qk[:, :hq].astype(BF16)
    k = qk[:, hq:] * (MLSTM_QK ** -0.5)
    k_b = k.astype(BF16)
    k_t = k.T
    v_b = v_ref[...]

    icol = G_MG + (2 * MLSTM_HEADS if reverse else 0)
    fcol = icol + MLSTM_HEADS
    g_col = gates_ref[...] + gbias_row_ref[...]
    g_row = gates_ref[...].T + gbias_col_ref[...]
    bcum_col, bcum_row = _cumsums(_log_sigmoid(g_col), _log_sigmoid(g_row), reverse)
    lower, upper = _tri_masks()
    mask = upper if reverse else lower
    ones_col = (lax.broadcasted_iota(jnp.int32, (CHUNK, MLSTM_V), 1) == 0).astype(BF16)

    for h in range(MLSTM_HEADS):
        bc = bcum_col[:, fcol + h:fcol + h + 1]
        br = bcum_row[fcol + h:fcol + h + 1, :]
        ig = g_row[icol + h:icol + h + 1, :]
        dm = jnp.where(mask, bc - br + ig, -jnp.inf)
        m_in = m_ref[h:h + 1, 0:1]
        w_inter = bc + m_in
        m_t = jnp.maximum(w_inter, jnp.max(dm, axis=1, keepdims=True))
        q_h = q_b[:, h * MLSTM_QK:(h + 1) * MLSTM_QK]
        s = _dot_nt(q_h, k_b[:, h * MLSTM_QK:(h + 1) * MLSTM_QK]) * jnp.exp(dm - m_t)
        v_ext = jnp.concatenate([v_b[:, h * MLSTM_V:(h + 1) * MLSTM_V], ones_col], axis=1)
        cn = cn_ref[h]
        tot = _dot(s.astype(BF16), v_ext) + jnp.exp(w_inter - m_t) * _dot(q_h, cn.astype(BF16))
        num = tot[:, :MLSTM_V]
        den = tot[:, MLSTM_V:MLSTM_V + 1]
        hh = num / jnp.maximum(jnp.abs(den), jnp.exp(-m_t))

        b_last = br[:, 0:1] if reverse else br[:, CHUNK - 1:CHUNK]
        w_end = b_last - br + ig
        m_loc = jnp.max(w_end, axis=1, keepdims=True)
        kw_t = (k_t[h * MLSTM_QK:(h + 1) * MLSTM_QK, :] * jnp.exp(w_end - m_loc)).astype(BF16)
        m_new = jnp.maximum(b_last + m_in, m_loc)
        cn_ref[h] = jnp.exp(b_last + m_in - m_new) * cn + jnp.exp(m_loc - m_new) * _dot(kw_t, v_ext)
        m_ref[h:h + 1, :] = jnp.broadcast_to(m_new, (1, GATE_LANES))

        sl = slice(h * MLSTM_V, (h + 1) * MLSTM_V)
        if final:
            hh = hh + hb_ref[:, sl].astype(F32)
            hh = hh * lax.rsqrt(jnp.mean(hh * hh, axis=-1, keepdims=True) + EPS) * normw_ref[:, sl]
            hh = _sigmoid(mo_ref[:, sl].astype(F32)) * hh
        o_ref[:, sl] = hh.astype(o_ref.dtype)


def mlstm_pass(proj, gates, lw, n_ctx, qk_conv=None, h_other=None):
    t = proj.shape[0]
    n_chunks = t // CHUNK
    n_ctx_chunks = n_ctx // CHUNK
    final = qk_conv is not None
    reverse = not final
    order = _chunk_order(n_ctx_chunks, n_chunks, reverse)
    const = lambda c, o: (0, 0)
    chunk = lambda c, o: (o[c], 0)
    common_specs = [pl.BlockSpec((CHUNK, BRANCH_W), lambda c, o: (o[c], P_MV // BRANCH_W)),
                    pl.BlockSpec((CHUNK, GATE_LANES), chunk),
                    pl.BlockSpec((1, GATE_LANES), const),
                    pl.BlockSpec((GATE_LANES, 1), const)]
    common_args = [proj, gates, lw['gbias_row'], lw['gbias_col']]
    h_shape = jax.ShapeDtypeStruct((t, BRANCH_W), BF16)
    h_spec = pl.BlockSpec((CHUNK, BRANCH_W), chunk)
    state = [pltpu.VMEM((MLSTM_HEADS, MLSTM_QK, 2 * MLSTM_V), F32), pltpu.VMEM((MLSTM_HEADS, GATE_LANES), F32)]
    if final:
        in_specs = [pl.BlockSpec((CHUNK, MLSTM_QKW), chunk)] + common_specs + [
            pl.BlockSpec((CHUNK, BRANCH_W), chunk),
            pl.BlockSpec((CHUNK, BRANCH_W), lambda c, o: (o[c], P_MO // BRANCH_W)),
            pl.BlockSpec((1, BRANCH_W), const)]
        args = [qk_conv] + common_args + [h_other, proj, lw['mlstm_norm_w']]
        out_shape, out_specs, scratch = h_shape, h_spec, state
    else:
        in_specs = _halo_specs(MLSTM_QKW, P_QK // MLSTM_QKW, t) + [
            pl.BlockSpec((8, MLSTM_QKW), const), pl.BlockSpec((1, MLSTM_QKW), const)] + common_specs
        args = [proj, proj, proj, lw['mlstm_conv_w'], lw['mlstm_conv_b']] + common_args
        out_shape = (h_shape, jax.ShapeDtypeStruct((t, MLSTM_QKW), BF16))
        out_specs = (h_spec, pl.BlockSpec((CHUNK, MLSTM_QKW), chunk))
        scratch = [pltpu.VMEM((CHUNK + 2 * HALO, MLSTM_QKW), F32)] + state
    return pl.pallas_call(
        functools.partial(_mlstm_kernel, reverse=reverse, final=final, n_ctx_chunks=n_ctx_chunks,
                          n_chunks=n_chunks),
        out_shape=out_shape,
        grid_spec=pltpu.PrefetchScalarGridSpec(
            num_scalar_prefetch=1, grid=(n_chunks,), in_specs=in_specs, out_specs=out_specs,
            scratch_shapes=scratch),
        compiler_params=_cparams(("arbitrary",)),
        name="mlstm_bwd" if reverse else "mlstm_fwd",
    )(order, *args)


def _fft_split(length):
    l1 = 1 << (int(math.log2(length)) // 2)
    return l1, length // l1


def _fft1_kernel(x_ref, cs_ref, f1_ref, twc_ref, tws_ref, ure_ref, uim_ref, *, l1):
    x = x_ref[...]
    cs = cs_ref[...]
    a, b = [], []
    for grp in range(FOURIER_GROUPS):
        ab = _dot(x[:, grp * FOURIER_GW:(grp + 1) * FOURIER_GW], cs)
        a.append(ab[:, :FOURIER_GW])
        b.append(ab[:, FOURIER_GW:])
    ab = jnp.concatenate(a + b, axis=1).astype(BF16)
    prod = _dot(f1_ref[...], ab)
    c_a = prod[:l1, :BRANCH_W]
    c_b = prod[:l1, BRANCH_W:]
    s_a = prod[l1:, :BRANCH_W]
    s_b = prod[l1:, BRANCH_W:]
    u_re = c_a - s_b
    u_im = -(c_b + s_a)
    twc = twc_ref[...]
    tws = tws_ref[...]
    ure_ref[...] = (u_re * twc + u_im * tws).astype(ure_ref.dtype)
    uim_ref[...] = (u_im * twc - u_re * tws).astype(uim_ref.dtype)


def _fft2_kernel(ure_ref, uim_ref, f2_ref, o_ref):
    u = jnp.concatenate([ure_ref[...], uim_ref[...]], axis=0)
    o_ref[...] = _dot(f2_ref[...], u).astype(o_ref.dtype)


def _fft_tables(length):
    l1, l2 = _fft_split(length)
    scale = 1.0 / math.sqrt(length * FOURIER_GW)
    kc = np.arange(FOURIER_GW)
    ang_c = 2.0 * np.pi * np.outer(kc, kc) / FOURIER_GW
    cs = np.concatenate([np.cos(ang_c), np.sin(ang_c)], axis=1) * scale
    k1 = np.arange(l1)
    ang1 = 2.0 * np.pi * np.outer(k1, k1) / l1
    f1 = np.concatenate([np.cos(ang1), np.sin(ang1)], axis=0)
    n2 = np.arange(l2)
    ang_t = 2.0 * np.pi * np.outer(n2, k1) / length
    ang2 = 2.0 * np.pi * np.outer(n2, n2) / l2
    f2 = np.concatenate([np.cos(ang2), np.sin(ang2)], axis=1)
    return (jnp.asarray(cs, BF16), jnp.asarray(f1, BF16), jnp.asarray(np.cos(ang_t)[:, :, None], F32),
            jnp.asarray(np.sin(ang_t)[:, :, None], F32), jnp.asarray(f2, BF16))


def fourier_mix(fx):
    length = fx.shape[0]
    l1, l2 = _fft_split(length)
    cs, f1, twc, tws, f2 = _fft_tables(length)
    u_shape = jax.ShapeDtypeStruct((l1, l2 * BRANCH_W), BF16)
    ure, uim = pl.pallas_call(
        functools.partial(_fft1_kernel, l1=l1),
        out_shape=(u_shape, u_shape),
        grid=(l2,),
        in_specs=[pl.BlockSpec((l1, BRANCH_W), lambda j: (0, j)),
                  pl.BlockSpec((FOURIER_GW, 2 * FOURIER_GW), lambda j: (0, 0)),
                  pl.BlockSpec((2 * l1, l1), lambda j: (0, 0)),
                  pl.BlockSpec((None, l1, 1), lambda j: (j, 0, 0)),
                  pl.BlockSpec((None, l1, 1), lambda j: (j, 0, 0))],
        out_specs=(pl.BlockSpec((l1, BRANCH_W), lambda j: (0, j)),
                   pl.BlockSpec((l1, BRANCH_W), lambda j: (0, j))),
        compiler_params=_cparams(("arbitrary",)),
        name="fft_stage1",
    )(fx.reshape(l1, l2 * BRANCH_W), cs, f1, twc, tws)
    out = pl.pallas_call(
        _fft2_kernel,
        out_shape=jax.ShapeDtypeStruct((l2, l1 * BRANCH_W), BF16),
        grid=(l1,),
        in_specs=[pl.BlockSpec((None, l2, BRANCH_W), lambda i: (i, 0, 0)),
                  pl.BlockSpec((None, l2, BRANCH_W), lambda i: (i, 0, 0)),
                  pl.BlockSpec((l2, 2 * l2), lambda i: (0, 0))],
        out_specs=pl.BlockSpec((l2, BRANCH_W), lambda i: (0, i)),
        compiler_params=_cparams(("arbitrary",)),
        name="fft_stage2",
    )(ure.reshape(l1, l2, BRANCH_W), uim.reshape(l1, l2, BRANCH_W), f2)
    return out.reshape(length, BRANCH_W)


def _sgu_kernel(su_ref, sv_ref, normw_ref, w_ref, b_ref, o_ref):
    v = _gelu_tanh(sv_ref[...].astype(F32))
    v = (v * lax.rsqrt(jnp.mean(v * v, axis=-1, keepdims=True) + EPS) * normw_ref[...]).astype(BF16)
    mix = []
    for grp in range(SGU_GROUPS):
        mix.append(_dot(w_ref[grp], v[:, grp * SGU_GW:(grp + 1) * SGU_GW]) + b_ref[:, grp:grp + 1])
    o_ref[...] = (_gelu_tanh(su_ref[...].astype(F32)) * jnp.concatenate(mix, axis=1)).astype(o_ref.dtype)


def sgu(proj, lw):
    t = proj.shape[0]
    return pl.pallas_call(
        _sgu_kernel,
        out_shape=jax.ShapeDtypeStruct((t, BRANCH_W), BF16),
        grid=(t // CHUNK,),
        in_specs=[pl.BlockSpec((CHUNK, BRANCH_W), lambda c: (c, P_SU // BRANCH_W)),
                  pl.BlockSpec((CHUNK, BRANCH_W), lambda c: (c, P_SV // BRANCH_W)),
                  pl.BlockSpec((1, BRANCH_W), lambda c: (0, 0)),
                  pl.BlockSpec((SGU_GROUPS, CHUNK, CHUNK), lambda c: (0, 0, 0)),
                  pl.BlockSpec((CHUNK, GATE_LANES), lambda c: (0, 0))],
        out_specs=pl.BlockSpec((CHUNK, BRANCH_W), lambda c: (c, 0)),
        compiler_params=_cparams(("arbitrary",)),
        name="sgu",
    )(proj, proj, lw['sgu_norm_w'], lw['sgu_w'], lw['sgu_b_t'])


def _merge_kernel(gd_ref, b0_ref, b1_ref, b2_ref, b3_ref, wg_ref, wb_ref, o_ref):
    gd = gd_ref[...]
    acc = None
    for n, b_ref in enumerate((b0_ref, b1_ref, b2_ref, b3_ref)):
        term = _sigmoid(_dot(gd, wg_ref[n])) * _dot(b_ref[...], wb_ref[n])
        acc = term if acc is None else acc + term
    o_ref[...] = acc.astype(o_ref.dtype)


def merge(proj, branches, lw):
    t = proj.shape[0]
    tm = _pick_tile(t, MM_TM)
    tn = MM_TN
    row = lambda i, j: (i, 0)
    return pl.pallas_call(
        _merge_kernel,
        out_shape=jax.ShapeDtypeStruct((t, D_MODEL), BF16),
        grid=(t // tm, D_MODEL // tn),
        in_specs=[pl.BlockSpec((tm, GATE_RANK), lambda i, j: (i, P_GD // GATE_RANK))] +
                 [pl.BlockSpec((tm, BRANCH_W), row)] * N_BRANCH +
                 [pl.BlockSpec((N_BRANCH, GATE_RANK, tn), lambda i, j: (0, 0, j)),
                  pl.BlockSpec((N_BRANCH, BRANCH_W, tn), lambda i, j: (0, 0, j))],
        out_specs=pl.BlockSpec((tm, tn), lambda i, j: (i, j)),
        compiler_params=_cparams(("arbitrary", "arbitrary")),
        name="merge",
    )(proj, *branches, lw['w_gate_up'], lw['w_branch'])


def _outproj_kernel(a_ref, w_ref, x_ref, mods_ref, o_ref, *, tm, n_ctx, gate_idx):
    is_ctx = _row_is_ctx(pl.program_id(0), tm, n_ctx)
    o_ref[...] = x_ref[...] + _mod_row(mods_ref, is_ctx, gate_idx) * _dot(a_ref[...], w_ref[...])


def outproj_residual(a, w, x, mods, n_ctx, gate_idx):
    t = a.shape[0]
    tm = _pick_tile(t, MM_TM)
    tn = MM_TN
    return pl.pallas_call(
        functools.partial(_outproj_kernel, tm=tm, n_ctx=n_ctx, gate_idx=gate_idx),
        out_shape=jax.ShapeDtypeStruct((t, D_MODEL), F32),
        grid=(t // tm, D_MODEL // tn),
        in_specs=[pl.BlockSpec((tm, D_MODEL), lambda i, j: (i, 0)),
                  pl.BlockSpec((D_MODEL, tn), lambda i, j: (0, j)),
                  pl.BlockSpec((tm, tn), lambda i, j: (i, j)),
                  pl.BlockSpec((2, MOD_ROWS, tn), lambda i, j: (0, 0, j))],
        out_specs=pl.BlockSpec((tm, tn), lambda i, j: (i, j)),
        compiler_params=_cparams(("arbitrary", "arbitrary")),
        name="outproj_residual",
    )(a, w, x, mods)


def _router_kernel(x_ref, w_ref, mods_ref, wr_ref, rb_ref, h_ref, idx_ref, wts_ref, rank_ref, cnt_ref, base_ref,
                   *, tm, n_ctx):
    @pl.when(pl.program_id(0) == 0)
    def _():
        base_ref[...] = jnp.zeros_like(base_ref)

    x = x_ref[...]
    is_ctx = _row_is_ctx(pl.program_id(0), tm, n_ctx)
    y = x * lax.rsqrt(jnp.mean(x * x, axis=-1, keepdims=True) + EPS) * w_ref[...]
    h = y * (1.0 + _mod_row(mods_ref, is_ctx, 4)) + _mod_row(mods_ref, is_ctx, 3)
    h_ref[...] = _pack_halves(h)
    logits = lax.dot_general(wr_ref[...], h, (((1,), (1,)), ((), ())), preferred_element_type=F32,
                             precision=HIGHEST)
    aff = _sigmoid(logits)
    sel = aff + rb_ref[...]
    rows = [sel[e:e + 1, :] for e in range(N_EXPERTS)]
    affs = [aff[e:e + 1, :] for e in range(N_EXPERTS)]
    gscore = []
    for grp in range(N_EXPERT_GROUPS):
        r = rows[grp * EXPERTS_PER_GROUP:(grp + 1) * EXPERTS_PER_GROUP]
        best = None
        for a in range(EXPERTS_PER_GROUP):
            for b in range(a + 1, EXPERTS_PER_GROUP):
                pair = r[a] + r[b]
                best = pair if best is None else jnp.maximum(best, pair)
        gscore.append(best)
    g_best = gscore[0]
    g_idx = jnp.zeros_like(g_best, dtype=jnp.int32)
    for grp in range(1, N_EXPERT_GROUPS):
        better = gscore[grp] > g_best
        g_best = jnp.where(better, gscore[grp], g_best)
        g_idx = jnp.where(better, grp, g_idx)
    masked = [jnp.where(g_idx == e // EXPERTS_PER_GROUP, rows[e], -jnp.inf) for e in range(N_EXPERTS)]

    def argbest(exclude):
        best_v = jnp.full_like(g_best, -jnp.inf)
        best_i = jnp.full_like(g_idx, -1)
        best_a = jnp.zeros_like(g_best)
        for e in range(N_EXPERTS):
            ok = masked[e] > best_v
            if exclude is not None:
                ok = jnp.logical_and(ok, exclude != e)
            best_v = jnp.where(ok, masked[e], best_v)
            best_i = jnp.where(ok, e, best_i)
            best_a = jnp.where(ok, affs[e], best_a)
        return best_i, best_a

    i1, a1 = argbest(None)
    i2, a2 = argbest(i1)
    tot = a1 + a2
    idx_ref[...] = jnp.concatenate([i1, i2], axis=0)
    wts_ref[...] = jnp.concatenate([a1 / tot, a2 / tot], axis=0)

    e_iota = lax.broadcasted_iota(jnp.int32, (N_EXPERTS, tm), 0)
    hit1 = e_iota == i1
    hit2 = e_iota == i2
    onehot = jnp.where(jnp.logical_or(hit1, hit2), 1.0, 0.0)
    r = lax.broadcasted_iota(jnp.int32, (tm, tm), 0)
    cidx = lax.broadcasted_iota(jnp.int32, (tm, tm), 1)
    before = jnp.where(r < cidx, 1.0, 0.0).astype(BF16)
    prior = base_ref[:, 0:1] + _dot(onehot.astype(BF16), before)
    rank1 = jnp.sum(jnp.where(hit1, prior, 0.0), axis=0, keepdims=True)
    rank2 = jnp.sum(jnp.where(hit2, prior, 0.0), axis=0, keepdims=True)
    rank_ref[...] = jnp.concatenate([rank1, rank2], axis=0).astype(jnp.int32)
    total = base_ref[...] + jnp.sum(onehot, axis=1, keepdims=True)
    base_ref[...] = total
    cnt_ref[...] = total


def norm_router(x, norm_w, mods, w_router_t, router_bias, n_ctx):
    t = x.shape[0]
    tm = ROW_TILE
    sel = lambda i: (0, i)
    return pl.pallas_call(
        functools.partial(_router_kernel, tm=tm, n_ctx=n_ctx),
        out_shape=(jax.ShapeDtypeStruct((t, D_MODEL // 2), jnp.uint32),
                   jax.ShapeDtypeStruct((TOP_K, t), jnp.int32),
                   jax.ShapeDtypeStruct((TOP_K, t), F32),
                   jax.ShapeDtypeStruct((TOP_K, t), jnp.int32),
                   jax.ShapeDtypeStruct((N_EXPERTS, GATE_LANES), F32)),
        grid=(t // tm,),
        in_specs=[pl.BlockSpec((tm, D_MODEL), lambda i: (i, 0)),
                  pl.BlockSpec((1, D_MODEL), lambda i: (0, 0)),
                  pl.BlockSpec((2, MOD_ROWS, D_MODEL), lambda i: (0, 0, 0)),
                  pl.BlockSpec((N_EXPERTS, D_MODEL), lambda i: (0, 0)),
                  pl.BlockSpec((N_EXPERTS, 1), lambda i: (0, 0))],
        out_specs=(pl.BlockSpec((tm, D_MODEL // 2), lambda i: (i, 0)),
                   pl.BlockSpec((TOP_K, tm), sel), pl.BlockSpec((TOP_K, tm), sel), pl.BlockSpec((TOP_K, tm), sel),
                   pl.BlockSpec((N_EXPERTS, GATE_LANES), lambda i: (0, 0))),
        scratch_shapes=[pltpu.VMEM((N_EXPERTS, GATE_LANES), F32)],
        compiler_params=_cparams(("arbitrary",)),
        name="norm_router",
    )(x, norm_w.reshape(1, D_MODEL), mods, w_router_t, router_bias.reshape(N_EXPERTS, 1))


def _routing_metadata(idx, rank, counts, n_tiles):
    tm = EXPERT_TM
    counts = counts[:, 0].astype(jnp.int32)
    padded = ((counts + tm - 1) // tm) * tm
    ends = jnp.cumsum(padded)
    starts = ends - padded
    experts_iota = jnp.arange(N_EXPERTS, dtype=jnp.int32)
    start_of = jnp.sum(jnp.where(idx[:, :, None] == experts_iota, starts, 0), axis=-1)
    pos = (start_of + rank).reshape(-1).astype(jnp.int32)
    n_used = (ends[-1] // tm).astype(jnp.int32)
    tile_start = jnp.arange(n_tiles, dtype=jnp.int32) * tm
    tile_expert = jnp.sum((tile_start[:, None] >= ends[None, :]).astype(jnp.int32), axis=1)
    last_used = jnp.sum((jnp.maximum(n_used - 1, 0) * tm >= ends).astype(jnp.int32))
    tile_expert = jnp.minimum(jnp.where(tile_start < n_used * tm, tile_expert, last_used), N_EXPERTS - 1)
    return pos, tile_expert.astype(jnp.int32), n_used.reshape(1), (starts + counts).astype(jnp.int32), \
        (padded - counts).astype(jnp.int32)


ZERO_ROWS = 64


def _dispatch_kernel(pos_ref, padstart_ref, npad_ref, nused_ref, h_hbm, o_hbm, zero_ref, sems,
                     *, tm, n_tok, n_rows):
    i = pl.program_id(0)
    n = pl.num_programs(0)
    slot = lax.rem(i, 2)
    used_rows = nused_ref[0] * EXPERT_TM
    n_trail = (n_rows - used_rows) // ZERO_ROWS

    def row_copy(src_row, dst_row, sem):
        return pltpu.make_async_copy(h_hbm.at[pl.ds(src_row, 1), :], o_hbm.at[pl.ds(dst_row, 1), :], sem)

    def pad_copy(dst_row):
        return pltpu.make_async_copy(zero_ref.at[pl.ds(0, 1), :], o_hbm.at[pl.ds(dst_row, 1), :], sems.at[2])

    def trail_copy(j):
        row = pl.multiple_of(used_rows + j * ZERO_ROWS, ZERO_ROWS)
        return pltpu.make_async_copy(zero_ref, o_hbm.at[pl.ds(row, ZERO_ROWS), :], sems.at[2])

    def pad_loop(fn, trail_fn):
        for e in range(N_EXPERTS):
            def body(r, carry, e=e):
                fn(padstart_ref[e] + r)
                return carry
            lax.fori_loop(0, npad_ref[e], body, 0)

        def trail_body(j, carry):
            trail_fn(j)
            return carry
        lax.fori_loop(0, n_trail, trail_body, 0)

    @pl.when(i == 0)
    def _():
        zero_ref[...] = jnp.zeros_like(zero_ref)
        pad_loop(lambda row: pad_copy(row).start(), lambda j: trail_copy(j).start())

    def issue(r, carry):
        tok = i * tm + r
        for k in range(TOP_K):
            row_copy(tok, pos_ref[k * n_tok + tok], sems.at[slot]).start()
        return carry

    def wait_tile(s):
        def body(r, carry):
            for k in range(TOP_K):
                row_copy(0, 0, sems.at[s]).wait()
            return carry
        lax.fori_loop(0, tm, body, 0)

    lax.fori_loop(0, tm, issue, 0)

    @pl.when(i > 0)
    def _():
        wait_tile(1 - slot)

    @pl.when(i == n - 1)
    def _():
        wait_tile(slot)
        pad_loop(lambda row: pad_copy(0).wait(), lambda j: trail_copy(0).wait())


def dispatch(h_packed, pos, pad_start, n_pad, n_used, n_rows):
    t = h_packed.shape[0]
    tm = ROW_TILE
    return pl.pallas_call(
        functools.partial(_dispatch_kernel, tm=tm, n_tok=t, n_rows=n_rows),
        out_shape=jax.ShapeDtypeStruct((n_rows, D_MODEL // 2), jnp.uint32),
        grid_spec=pltpu.PrefetchScalarGridSpec(
            num_scalar_prefetch=4, grid=(t // tm,),
            in_specs=[pl.BlockSpec(memory_space=pl.ANY)],
            out_specs=pl.BlockSpec(memory_space=pl.ANY),
            scratch_shapes=[pltpu.VMEM((ZERO_ROWS, D_MODEL // 2), jnp.uint32), pltpu.SemaphoreType.DMA((3,))]),
        compiler_params=_cparams(("arbitrary",)),
        name="dispatch",
    )(pos, pad_start, n_pad, n_used, h_packed)


def _experts_kernel(texp_ref, nused_ref, x_ref, win_ref, wout_ref, o_ref):
    i = pl.program_id(0)
    half = D_MODEL // 2

    @pl.when(i < nused_ref[0])
    def _():
        xp = x_ref[...]
        gu = (_dot(_unpack_lo(xp).astype(BF16), win_ref[:half, :]) +
              _dot(_unpack_hi(xp).astype(BF16), win_ref[half:, :]))
        act = (_silu(gu[:, :D_EXPERT]) * gu[:, D_EXPERT:]).astype(BF16)
        o_ref[...] = _pack_halves(_dot(act, wout_ref[...]))

    @pl.when(i >= nused_ref[0])
    def _():
        o_ref[...] = jnp.zeros_like(o_ref)


def experts(x_sorted, tile_expert, n_used, w_exp_in, w_exp_out):
    tm = EXPERT_TM
    n_tiles = tile_expert.shape[0]
    half = D_MODEL // 2
    return pl.pallas_call(
        _experts_kernel,
        out_shape=jax.ShapeDtypeStruct((n_tiles * tm, half), jnp.uint32),
        grid_spec=pltpu.PrefetchScalarGridSpec(
            num_scalar_prefetch=2, grid=(n_tiles,),
            in_specs=[pl.BlockSpec((tm, half), lambda i, te, nu: (jnp.minimum(i, nu[0] - 1), 0)),
                      pl.BlockSpec((None, D_MODEL, 2 * D_EXPERT), lambda i, te, nu: (te[i], 0, 0)),
                      pl.BlockSpec((None, D_EXPERT, D_MODEL), lambda i, te, nu: (te[i], 0, 0))],
            out_specs=pl.BlockSpec((tm, half), lambda i, te, nu: (i, 0))),
        compiler_params=_cparams(("arbitrary",)),
        name="experts",
    )(tile_expert, n_used, x_sorted, w_exp_in, w_exp_out)


def _combine_kernel(pos_ref, x_ref, wts_ref, mods_ref, nw_ref, nmods_ref, y_hbm, *rest, tm, n_ctx, n_tok, last):
    if last:
        f_ref, ybuf, sems = rest
    else:
        xo_ref, h_ref, ybuf, sems = rest
    i = pl.program_id(0)
    n = pl.num_programs(0)
    slot = lax.rem(i, 2)
    half = D_MODEL // 2

    def row_copy(src_row, s, k, r):
        return pltpu.make_async_copy(y_hbm.at[pl.ds(src_row, 1), :], ybuf.at[s, k, pl.ds(r, 1), :], sems.at[s])

    def issue(tile, s):
        def body(r, carry):
            for k in range(TOP_K):
                row_copy(pos_ref[k * n_tok + tile * tm + r], s, k, r).start()
            return carry
        lax.fori_loop(0, tm, body, 0)

    @pl.when(i == 0)
    def _():
        issue(0, 0)

    @pl.when(i + 1 < n)
    def _():
        issue(i + 1, 1 - slot)

    def wait_body(r, carry):
        for k in range(TOP_K):
            row_copy(0, slot, k, r).wait()
        return carry
    lax.fori_loop(0, tm, wait_body, 0)

    w = wts_ref[...]
    y0 = ybuf[slot, 0]
    y1 = ybuf[slot, 1]
    is_ctx = _row_is_ctx(i, tm, n_ctx)
    gate = _mod_row(mods_ref, is_ctx, 5)
    x = x_ref[...]
    halves = []
    for sl, unpack in ((slice(0, half), _unpack_lo), (slice(half, D_MODEL), _unpack_hi)):
        moe = w[:, 0:1] * unpack(y0) + w[:, 1:2] * unpack(y1)
        halves.append(x[:, sl] + gate[:, sl] * moe)
    ssq = sum(jnp.sum(v * v, axis=-1, keepdims=True) for v in halves)
    inv = lax.rsqrt(ssq / D_MODEL + EPS)
    for v, sl in zip(halves, (slice(0, half), slice(half, D_MODEL))):
        normed = v * inv * nw_ref[:, sl]
        if last:
            f_ref[:, sl] = normed
        else:
            xo_ref[:, sl] = v
            scale = _mod_row(nmods_ref, is_ctx, 1)[:, sl]
            shift = _mod_row(nmods_ref, is_ctx, 0)[:, sl]
            h_ref[:, sl] = (normed * (1.0 + scale) + shift).astype(h_ref.dtype)


def combine(x, y_sorted, pos, wts_t, mods, next_norm_w, next_mods, n_ctx, last):
    t = x.shape[0]
    tm = CHUNK
    full = lambda i, p: (i, 0)
    if last:
        ctx_tiles = n_ctx // tm
        out_shape = jax.ShapeDtypeStruct((t - n_ctx, D_MODEL), F32)
        out_specs = pl.BlockSpec((tm, D_MODEL), lambda i, p: (jnp.maximum(i - ctx_tiles, 0), 0))
    else:
        out_shape = (jax.ShapeDtypeStruct((t, D_MODEL), F32), jax.ShapeDtypeStruct((t, D_MODEL), BF16))
        out_specs = (pl.BlockSpec((tm, D_MODEL), full), pl.BlockSpec((tm, D_MODEL), full))
    return pl.pallas_call(
        functools.partial(_combine_kernel, tm=tm, n_ctx=n_ctx, n_tok=t, last=last),
        out_shape=out_shape,
        grid_spec=pltpu.PrefetchScalarGridSpec(
            num_scalar_prefetch=1, grid=(t // tm,),
            in_specs=[pl.BlockSpec((tm, D_MODEL), full),
                      pl.BlockSpec((tm, TOP_K), full),
                      pl.BlockSpec((2, MOD_ROWS, D_MODEL), lambda i, p: (0, 0, 0)),
                      pl.BlockSpec((1, D_MODEL), lambda i, p: (0, 0)),
                      pl.BlockSpec((2, MOD_ROWS, D_MODEL), lambda i, p: (0, 0, 0)),
                      pl.BlockSpec(memory_space=pl.ANY)],
            out_specs=out_specs,
            scratch_shapes=[pltpu.VMEM((2, TOP_K, tm, D_MODEL // 2), jnp.uint32),
                            pltpu.SemaphoreType.DMA((2,))]),
        compiler_params=_cparams(("arbitrary",)),
        name="combine_last" if last else "combine",
    )(pos, x, wts_t, mods, next_norm_w.reshape(1, D_MODEL), next_mods, y_sorted)


_MAIN_PIECES = ((O_XBC, SSD_XBC), (O_GD, GATE_RANK), (O_Z, SSD_INNER), (O_FX, BRANCH_W), (O_MQ, MLSTM_QKW),
                (O_MV, BRANCH_W), (O_MO, BRANCH_W), (O_SU, BRANCH_W), (O_SV, BRANCH_W))


def _relayout_w_in_kernel(w_ref, main_ref, gates_ref):
    col = 0
    for off, width in _MAIN_PIECES:
        main_ref[:, col:col + width] = w_ref[:, off:off + width].astype(BF16)
        col += width
    gates_ref[...] = jnp.zeros_like(gates_ref)
    gates_ref[:, G_DT:G_DT + 2 * SSD_HEADS] = w_ref[:, O_DT:O_DT + 2 * SSD_HEADS].astype(BF16)
    gates_ref[:, G_MG:G_MG + 4 * MLSTM_HEADS] = w_ref[:, O_MG:O_MG + 4 * MLSTM_HEADS].astype(BF16)


def relayout_w_in(w_in):
    depth, d, d_in = w_in.shape
    tk = ROW_TILE
    return pl.pallas_call(
        _relayout_w_in_kernel,
        out_shape=(jax.ShapeDtypeStruct((depth, d, P_TOTAL), BF16),
                   jax.ShapeDtypeStruct((depth, d, GATE_LANES), BF16)),
        grid=(depth, d // tk),
        in_specs=[pl.BlockSpec((None, tk, d_in), lambda l, i: (l, i, 0))],
        out_specs=(pl.BlockSpec((None, tk, P_TOTAL), lambda l, i: (l, i, 0)),
                   pl.BlockSpec((None, tk, GATE_LANES), lambda l, i: (l, i, 0))),
        compiler_params=_cparams(("arbitrary", "arbitrary")),
        name="relayout_w_in",
    )(w_in)


def _prep_layer_params(l, p):
    pad = GATE_LANES - 2 * SSD_HEADS - 4 * MLSTM_HEADS
    gbias = jnp.concatenate([p['ssd_dt_bias'][l].reshape(-1), p['mlstm_gate_b'][l].reshape(-1),
                             jnp.zeros((pad,), F32)])
    alog = jnp.concatenate([p['ssd_a_log'][l].reshape(-1), jnp.zeros((GATE_LANES - 2 * SSD_HEADS,), F32)])
    feat = np.arange(GATE_LANES)[:, None]
    head = np.arange(SSD_INNER)[None, :] // SSD_HEAD_DIM
    pad_k = lambda w: jnp.concatenate([w, jnp.zeros((8 - CONV_K, w.shape[1]), F32)], axis=0)
    return {
        'w_main': p['w_main'][l], 'w_gates': p['w_gates'][l],
        'gbias_row': gbias.reshape(1, -1), 'gbias_col': gbias.reshape(-1, 1),
        'alog_row': alog.reshape(1, -1), 'alog_col': alog.reshape(-1, 1),
        'expand_f': jnp.asarray(feat == head, BF16), 'expand_b': jnp.asarray(feat == head + SSD_HEADS, BF16),
        'ssd_conv_w': pad_k(p['ssd_conv_w'][l]), 'ssd_conv_b': p['ssd_conv_b'][l].reshape(1, -1),
        'ssd_d_e': jnp.repeat(p['ssd_d'][l], SSD_HEAD_DIM).reshape(1, -1),
        'ssd_norm_w': p['ssd_norm_w'][l].reshape(1, -1),
        'mlstm_conv_w': pad_k(p['mlstm_conv_w'][l]), 'mlstm_conv_b': p['mlstm_conv_b'][l].reshape(1, -1),
        'mlstm_norm_w': p['mlstm_norm_w'][l].reshape(1, -1),
        'sgu_norm_w': p['sgu_norm_w'][l].reshape(1, -1),
        'sgu_w': p['sgu_w'][l].astype(BF16),
        'sgu_b_t': jnp.concatenate([p['sgu_b'][l].T, jnp.zeros((CHUNK, GATE_LANES - SGU_GROUPS), F32)], axis=1),
        'w_gate_up': p['w_gate_up'][l], 'w_branch': p['w_branch'][l], 'w_out': p['w_out'][l],
        'w_exp_in': p['w_exp_in'][l], 'w_exp_out': p['w_exp_out'][l],
    }


def kernel(x, c, ctx, c_ctx, ada_down, ada_up, ada_b, norm1_w, norm2_w, w_in, ssd_conv_w, ssd_conv_b,
           ssd_dt_bias, ssd_a_log, ssd_d, ssd_norm_w, mlstm_conv_w, mlstm_conv_b, mlstm_gate_b, mlstm_norm_w,
           sgu_norm_w, sgu_w, sgu_b, w_gate_up, w_branch, w_out, w_router, router_bias, w_exp_in, w_exp_out,
           final_norm_w):
    assert x.shape[0] == 1 and ctx.shape[0] == 1, "single-sequence kernel"
    w_main, w_gates = relayout_w_in(w_in)
    p = dict(w_main=w_main, w_gates=w_gates, ssd_conv_w=ssd_conv_w, ssd_conv_b=ssd_conv_b,
             ssd_dt_bias=ssd_dt_bias, ssd_a_log=ssd_a_log, ssd_d=ssd_d, ssd_norm_w=ssd_norm_w,
             mlstm_conv_w=mlstm_conv_w, mlstm_conv_b=mlstm_conv_b, mlstm_gate_b=mlstm_gate_b,
             mlstm_norm_w=mlstm_norm_w, sgu_norm_w=sgu_norm_w, sgu_w=sgu_w, sgu_b=sgu_b,
             w_gate_up=w_gate_up.astype(BF16), w_branch=w_branch.astype(BF16), w_out=w_out.astype(BF16),
             w_exp_in=w_exp_in.astype(BF16), w_exp_out=w_exp_out.astype(BF16))
    depth = w_in.shape[0]
    seq = x.shape[1]
    n_ctx = ctx.shape[1]
    t = n_ctx + seq
    assert n_ctx % ROW_TILE == 0 and seq % ROW_TILE == 0
    n_tiles = -(-(TOP_K * t) // EXPERT_TM) + N_EXPERTS

    xs = jnp.concatenate([ctx[0], x[0]], axis=0)
    cvec = jnp.concatenate([c_ctx[None], c, jnp.zeros((6, D_MODEL), F32)], axis=0)
    mods_all = ada_mods(cvec, ada_down, ada_up, ada_b)
    mods_all = mods_all[:, :2].reshape(depth, 2, N_MOD, D_MODEL)
    mods_all = jnp.concatenate([mods_all, jnp.zeros((depth, 2, MOD_ROWS - N_MOD, D_MODEL), F32)], axis=2)
    w_router_t = w_router.T

    h = norm_mod(xs, norm1_w[0], mods_all[0], n_ctx, 0, 1, BF16)
    for l in range(depth):
        lw = _prep_layer_params(l, p)
        mods = mods_all[l]
        last = l == depth - 1
        proj = matmul(h, lw['w_main'], BF16, MM_TN)
        gates = matmul(h, lw['w_gates'], F32, GATE_LANES)

        ys_b, xbc_conv = ssd_pass(proj, gates, lw, n_ctx)
        y_ssd = ssd_pass(proj, gates, lw, n_ctx, xbc_conv, ys_b)
        hm_b, qk_conv = mlstm_pass(proj, gates, lw, n_ctx)
        y_ml = mlstm_pass(proj, gates, lw, n_ctx, qk_conv, hm_b)
        fx = proj[:, P_FX:P_FX + BRANCH_W]
        y_four = jnp.concatenate([fourier_mix(fx[:n_ctx]), fourier_mix(fx[n_ctx:])], axis=0)
        y_sgu = sgu(proj, lw)

        merged = merge(proj, (y_ssd, y_four, y_ml, y_sgu), lw)
        xs = outproj_residual(merged, lw['w_out'], xs, mods, n_ctx, 2)

        h_packed, idx, wts, rank, counts = norm_router(xs, norm2_w[l], mods, w_router_t, router_bias, n_ctx)
        pos, tile_expert, n_used, pad_start, n_pad = _routing_metadata(idx, rank, counts, n_tiles)
        x_sorted = dispatch(h_packed, pos, pad_start, n_pad, n_used, n_tiles * EXPERT_TM)
        y_sorted = experts(x_sorted, tile_expert, n_used, lw['w_exp_in'], lw['w_exp_out'])
        if last:
            out = combine(xs, y_sorted, pos, wts.T, mods, final_norm_w, mods, n_ctx, True)
        else:
            xs, h = combine(xs, y_sorted, pos, wts.T, mods, norm1_w[l + 1], mods_all[l + 1], n_ctx, False)
    return out[None]
```

```python
import functools
import math

import numpy as np
import jax
import jax.numpy as jnp
from jax import lax
from jax.experimental import pallas as pl
from jax.experimental.pallas import tpu as pltpu

F32 = jnp.float32
BF16 = jnp.bfloat16
HIGHEST = lax.Precision.HIGHEST

D_MODEL = 4096
CHUNK = 128
EPS = 1e-6
ADA_RANK = 256
N_MOD = 6
MOD_ROWS = 8
N_BRANCH = 4
BRANCH_W = 1024
GATE_RANK = 512
SSD_HEADS = 16
SSD_HEAD_DIM = 64
SSD_INNER = SSD_HEADS * SSD_HEAD_DIM
SSD_GROUPS = 4
SSD_HPG = SSD_HEADS // SSD_GROUPS
SSD_STATE = 64
SSD_XBC = SSD_INNER + 2 * SSD_GROUPS * SSD_STATE
CONV_K = 5
CONV_PAD = CONV_K // 2
FOURIER_GROUPS = 4
FOURIER_GW = BRANCH_W // FOURIER_GROUPS
MLSTM_HEADS = 8
MLSTM_QK = 64
MLSTM_V = 128
MLSTM_QKW = 2 * MLSTM_HEADS * MLSTM_QK
SGU_GROUPS = 4
SGU_GW = BRANCH_W // SGU_GROUPS
N_EXPERTS = 16
N_EXPERT_GROUPS = 4
EXPERTS_PER_GROUP = N_EXPERTS // N_EXPERT_GROUPS
TOP_K = 2
D_EXPERT = 640

IN_SPLITS = (SSD_INNER, SSD_XBC, 2 * SSD_HEADS, BRANCH_W, MLSTM_HEADS * MLSTM_QK, MLSTM_HEADS * MLSTM_QK,
             MLSTM_HEADS * MLSTM_V, MLSTM_HEADS * MLSTM_V, 4 * MLSTM_HEADS, BRANCH_W, BRANCH_W, GATE_RANK)
_OFF = [0] + [int(v) for v in np.cumsum(IN_SPLITS)]
(O_Z, O_XBC, O_DT, O_FX, O_MQ, O_MK, O_MV, O_MO, O_MG, O_SU, O_SV, O_GD) = _OFF[:-1]

P_XBC = 0
P_GD = SSD_XBC
P_Z = P_GD + GATE_RANK
P_FX = P_Z + BRANCH_W
P_QK = P_FX + BRANCH_W
P_MV = P_QK + MLSTM_QKW
P_MO = P_MV + BRANCH_W
P_SU = P_MO + BRANCH_W
P_SV = P_SU + BRANCH_W
P_TOTAL = P_SV + BRANCH_W
GATE_LANES = 128
G_DT = 0
G_MG = 2 * SSD_HEADS

HALO = 16
VMEM_LIMIT = 56 * 1024 * 1024

ROW_TILE = 256
MM_TM = 1280
MM_TN = 512
EXPERT_TM = 512


def _cparams(sem):
    return pltpu.CompilerParams(dimension_semantics=sem, vmem_limit_bytes=VMEM_LIMIT)


def _pick_tile(n, pref):
    t = min(pref, n)
    while n % t:
        t -= ROW_TILE if t > ROW_TILE else 8
    return t


def _sigmoid(x):
    return 1.0 / (1.0 + jnp.exp(-x))


def _silu(x):
    return x * _sigmoid(x)


def _softplus(x):
    return jnp.maximum(x, 0.0) + jnp.log(1.0 + jnp.exp(-jnp.abs(x)))


def _log_sigmoid(x):
    return -_softplus(-x)


def _gelu_tanh(x):
    return 0.5 * x * (1.0 + jnp.tanh(math.sqrt(2.0 / math.pi) * (x + 0.044715 * (x * x * x))))


def _dot(a, b):
    return jnp.dot(a, b, preferred_element_type=F32)


def _dot_nt(a, b):
    return lax.dot_general(a, b, (((1,), (1,)), ((), ())), preferred_element_type=F32)


def _dot_hi(a, b):
    return jnp.dot(a, b, preferred_element_type=F32, precision=HIGHEST)


def _split_bf16(x, parts=3):
    out, rest = [], x
    for _ in range(parts):
        piece = rest.astype(BF16)
        out.append(piece)
        rest = rest - piece.astype(F32)
    return out


def _dot_f32_by_exact(a, b_exact):
    terms = [_dot(piece, b_exact) for piece in _split_bf16(a)]
    return terms[0] + terms[1] + terms[2]


def _dot_exact_by_f32(a_exact, b):
    terms = [_dot(a_exact, piece) for piece in _split_bf16(b)]
    return terms[0] + terms[1] + terms[2]


HI_MASK = 0xFFFF0000


def _pack_halves(v):
    bits = lax.bitcast_convert_type(v.astype(BF16).astype(F32), jnp.uint32)
    n = v.shape[1] // 2
    return (bits[:, :n] >> 16) | (bits[:, n:] & jnp.uint32(HI_MASK))


def _unpack_lo(p):
    return lax.bitcast_convert_type(p << 16, F32)


def _unpack_hi(p):
    return lax.bitcast_convert_type(p & jnp.uint32(HI_MASK), F32)


def _row_is_ctx(i, tm, n_ctx):
    row = i * tm + lax.broadcasted_iota(jnp.int32, (tm, 1), 0)
    return row < n_ctx


def _mod_row(mods_ref, is_ctx, idx):
    return jnp.where(is_ctx, mods_ref[0, idx:idx + 1, :], mods_ref[1, idx:idx + 1, :])


def _ada_kernel(c_ref, down_ref, up_ref, b_ref, o_ref, low_ref):
    @pl.when(pl.program_id(1) == 0)
    def _():
        low_ref[...] = _dot_hi(_silu(c_ref[...]), down_ref[...])

    o_ref[...] = _dot_hi(low_ref[...], up_ref[...]) + b_ref[...]


def ada_mods(cvec, ada_down, ada_up, ada_b):
    depth = ada_down.shape[0]
    n_out = ada_up.shape[2]
    tn = 2048
    return pl.pallas_call(
        _ada_kernel,
        out_shape=jax.ShapeDtypeStruct((depth, 8, n_out), F32),
        grid=(depth, n_out // tn),
        in_specs=[pl.BlockSpec((8, D_MODEL), lambda l, j: (0, 0)),
                  pl.BlockSpec((None, D_MODEL, ADA_RANK), lambda l, j: (l, 0, 0)),
                  pl.BlockSpec((None, ADA_RANK, tn), lambda l, j: (l, 0, j)),
                  pl.BlockSpec((None, 1, tn), lambda l, j: (l, 0, j))],
        out_specs=pl.BlockSpec((None, 8, tn), lambda l, j: (l, 0, j)),
        scratch_shapes=[pltpu.VMEM((8, ADA_RANK), F32)],
        compiler_params=_cparams(("arbitrary", "arbitrary")),
        name="ada_mods",
    )(cvec, ada_down, ada_up, ada_b.reshape(depth, 1, n_out))


def _norm_kernel(x_ref, w_ref, mods_ref, o_ref, *, tm, n_ctx, shift_idx, scale_idx, row_off):
    x = x_ref[...]
    y = x * lax.rsqrt(jnp.mean(x * x, axis=-1, keepdims=True) + EPS) * w_ref[...]
    if shift_idx is not None:
        is_ctx = _row_is_ctx(pl.program_id(0) + row_off, tm, n_ctx)
        y = y * (1.0 + _mod_row(mods_ref, is_ctx, scale_idx)) + _mod_row(mods_ref, is_ctx, shift_idx)
    o_ref[...] = y.astype(o_ref.dtype)


def norm_mod(x, w, mods, n_ctx, shift_idx, scale_idx, out_dtype, row_start=0):
    t = x.shape[0] - row_start
    tm = ROW_TILE
    row_off = row_start // tm
    return pl.pallas_call(
        functools.partial(_norm_kernel, tm=tm, n_ctx=n_ctx, shift_idx=shift_idx, scale_idx=scale_idx,
                          row_off=row_off),
        out_shape=jax.ShapeDtypeStruct((t, D_MODEL), out_dtype),
        grid=(t // tm,),
        in_specs=[pl.BlockSpec((tm, D_MODEL), lambda i: (i + row_off, 0)),
                  pl.BlockSpec((1, D_MODEL), lambda i: (0, 0)),
                  pl.BlockSpec((2, MOD_ROWS, D_MODEL), lambda i: (0, 0, 0))],
        out_specs=pl.BlockSpec((tm, D_MODEL), lambda i: (i, 0)),
        compiler_params=_cparams(("arbitrary",)),
        name="norm_mod",
    )(x, w.reshape(1, D_MODEL), mods)


def _mm_kernel(a_ref, b_ref, o_ref):
    o_ref[...] = _dot(a_ref[...], b_ref[...]).astype(o_ref.dtype)


def matmul(a, b_stack, layer, out_dtype, tn):
    m, k = a.shape
    n = b_stack.shape[2]
    tm = _pick_tile(m, MM_TM)
    return pl.pallas_call(
        _mm_kernel,
        out_shape=jax.ShapeDtypeStruct((m, n), out_dtype),
        grid=(m // tm, n // tn),
        in_specs=[pl.BlockSpec((tm, k), lambda i, j: (i, 0)),
                  pl.BlockSpec((None, k, tn), lambda i, j: (layer, 0, j))],
        out_specs=pl.BlockSpec((tm, tn), lambda i, j: (i, j)),
        compiler_params=_cparams(("arbitrary", "arbitrary")),
        name="matmul",
    )(a, b_stack)


def _conv_silu(cur_ref, prev_ref, next_ref, cw_ref, cb_ref, ext_ref, first, last):
    prev = jnp.where(first, 0.0, prev_ref[...].astype(F32))
    nxt = jnp.where(last, 0.0, next_ref[...].astype(F32))
    ext_ref[0:HALO, :] = prev
    ext_ref[HALO:HALO + CHUNK, :] = cur_ref[...].astype(F32)
    ext_ref[HALO + CHUNK:2 * HALO + CHUNK, :] = nxt
    acc = cb_ref[...] + cw_ref[0:1, :] * ext_ref[pl.ds(HALO - CONV_PAD, CHUNK), :]
    for k in range(1, CONV_K):
        acc = acc + cw_ref[k:k + 1, :] * ext_ref[pl.ds(HALO - CONV_PAD + k, CHUNK), :]
    return _silu(acc)


def _tri_masks():
    r = lax.broadcasted_iota(jnp.int32, (CHUNK, CHUNK), 0)
    c = lax.broadcasted_iota(jnp.int32, (CHUNK, CHUNK), 1)
    return c <= r, c >= r


def _cumsums(col, row, reverse):
    lower, upper = _tri_masks()
    lo = jnp.where(lower, 1.0, 0.0).astype(BF16)
    up = jnp.where(upper, 1.0, 0.0).astype(BF16)
    if reverse:
        return _dot_exact_by_f32(up, col), _dot_f32_by_exact(row, lo)
    return _dot_exact_by_f32(lo, col), _dot_f32_by_exact(row, up)


def _chunk_order(n_ctx_chunks, n_chunks, reverse):
    if not reverse:
        return jnp.arange(n_chunks, dtype=jnp.int32)
    return jnp.concatenate([jnp.arange(n_ctx_chunks - 1, -1, -1, dtype=jnp.int32),
                            jnp.arange(n_chunks - 1, n_ctx_chunks - 1, -1, dtype=jnp.int32)])


def _halo_specs(width, col_block, n_rows):
    per = CHUNK // HALO
    last_halo = n_rows // HALO - 1
    cur = pl.BlockSpec((CHUNK, width), lambda c, o: (o[c], col_block))
    prev = pl.BlockSpec((HALO, width), lambda c, o: (jnp.maximum(o[c] * per - 1, 0), col_block))
    nxt = pl.BlockSpec((HALO, width), lambda c, o: (jnp.minimum((o[c] + 1) * per, last_halo), col_block))
    return [cur, prev, nxt]


def _ssd_kernel(order_ref, *refs, reverse, final, n_ctx_chunks, n_chunks):
    if final:
        (xc_ref, gates_ref, gbias_row_ref, alog_row_ref, alog_col_ref, gbias_col_ref, expand_ref,
         yb_ref, z_ref, dskip_ref, normw_ref, o_ref, h_ref) = refs
    else:
        (xbc_ref, xprev_ref, xnext_ref, cw_ref, cb_ref, gates_ref, gbias_row_ref, alog_row_ref, alog_col_ref,
         gbias_col_ref, expand_ref, o_ref, xc_ref, ext_ref, h_ref) = refs
    c = pl.program_id(0)
    ci = order_ref[c]

    @pl.when(c == 0)
    def _():
        h_ref[...] = jnp.zeros_like(h_ref)

    if final:
        xbc = xc_ref[...].astype(F32)
    else:
        first = jnp.logical_or(ci == 0, ci == n_ctx_chunks)
        last = jnp.logical_or(ci == n_ctx_chunks - 1, ci == n_chunks - 1)
        xbc = _conv_silu(xbc_ref, xprev_ref, xnext_ref, cw_ref, cb_ref, ext_ref, first, last)
        xc_ref[...] = xbc.astype(xc_ref.dtype)
    xs = xbc[:, :SSD_INNER]
    bm = xbc[:, SSD_INNER:SSD_INNER + SSD_GROUPS * SSD_STATE]
    cm = xbc[:, SSD_INNER + SSD_GROUPS * SSD_STATE:]

    lane0 = SSD_HEADS if reverse else 0
    g = gates_ref[...]
    dt_col = _softplus(g + gbias_row_ref[...])
    dt_row = _softplus(g.T + gbias_col_ref[...])
    acum_col, acum_row = _cumsums(dt_col * -jnp.exp(alog_row_ref[...]),
                                  dt_row * -jnp.exp(alog_col_ref[...]), reverse)
    both_e = _dot_f32_by_exact(jnp.concatenate([dt_col, acum_col], axis=0), expand_ref[...])
    dt_e = both_e[:CHUNK]
    acum_e = both_e[CHUNK:]
    alast_e = acum_e[0:1, :] if reverse else acum_e[CHUNK - 1:CHUNK, :]
    xdt = xs * dt_e
    xdt_b = xdt.astype(BF16)
    xw_b = (xdt * jnp.exp(alast_e - acum_e)).astype(BF16)
    dec_e = jnp.exp(alast_e)
    lower, upper = _tri_masks()
    mask = upper if reverse else lower
    cm_b = cm.astype(BF16)
    bm_b = bm.astype(BF16)
    bm_t = bm.T.astype(BF16)
    lane = lax.broadcasted_iota(jnp.int32, (CHUNK, 2 * SSD_HEAD_DIM), 1)
    gw = SSD_HPG * SSD_HEAD_DIM
    ys, inters = [], []
    for grp in range(SSD_GROUPS):
        c_g = cm_b[:, grp * SSD_STATE:(grp + 1) * SSD_STATE]
        b_g = bm_b[:, grp * SSD_STATE:(grp + 1) * SSD_STATE]
        cb = _dot_nt(c_g, b_g)
        h_t = h_ref[grp]
        inters.append(_dot(c_g, h_t.astype(BF16)))
        for pair in range(SSD_HPG // 2):
            ps = []
            for k in range(2):
                f = lane0 + grp * SSD_HPG + pair * 2 + k
                seg = acum_col[:, f:f + 1] - acum_row[f:f + 1, :]
                ps.append((cb * jnp.exp(jnp.where(mask, seg, -jnp.inf))).astype(BF16))
            col0 = (grp * (SSD_HPG // 2) + pair) * 2 * SSD_HEAD_DIM
            xp = xdt_b[:, col0:col0 + 2 * SSD_HEAD_DIM]
            rhs = jnp.concatenate([jnp.where(lane < SSD_HEAD_DIM, xp, jnp.zeros_like(xp)),
                                   jnp.where(lane >= SSD_HEAD_DIM, xp, jnp.zeros_like(xp))], axis=0)
            ys.append(_dot(jnp.concatenate(ps, axis=1), rhs))
        new = _dot(bm_t[grp * SSD_STATE:(grp + 1) * SSD_STATE, :], xw_b[:, grp * gw:(grp + 1) * gw])
        h_ref[grp] = dec_e[:, grp * gw:(grp + 1) * gw] * h_t + new
    y = jnp.concatenate(ys, axis=1) + jnp.exp(acum_e) * jnp.concatenate(inters, axis=1)
    if final:
        y = y + yb_ref[...].astype(F32) + dskip_ref[...] * xs
        z = z_ref[...].astype(F32)
        y = y * _silu(z)
        y = y * lax.rsqrt(jnp.mean(y * y, axis=-1, keepdims=True) + EPS) * normw_ref[...]
    o_ref[...] = y.astype(o_ref.dtype)


def ssd_pass(proj, gates, lw, n_ctx, xbc_conv=None, y_other=None):
    t = proj.shape[0]
    n_chunks = t // CHUNK
    n_ctx_chunks = n_ctx // CHUNK
    final = xbc_conv is not None
    reverse = not final
    order = _chunk_order(n_ctx_chunks, n_chunks, reverse)
    const = lambda c, o: (0, 0)
    chunk = lambda c, o: (o[c], 0)
    gate_specs = [pl.BlockSpec((CHUNK, GATE_LANES), chunk),
                  pl.BlockSpec((1, GATE_LANES), const),
                  pl.BlockSpec((1, GATE_LANES), const),
                  pl.BlockSpec((GATE_LANES, 1), const),
                  pl.BlockSpec((GATE_LANES, 1), const),
                  pl.BlockSpec((GATE_LANES, SSD_INNER), const)]
    gate_args = [gates, lw['gbias_row'], lw['alog_row'], lw['alog_col'], lw['gbias_col'],
                 lw['expand_b' if reverse else 'expand_f']]
    y_shape = jax.ShapeDtypeStruct((t, SSD_INNER), BF16)
    y_spec = pl.BlockSpec((CHUNK, SSD_INNER), chunk)
    state = pltpu.VMEM((SSD_GROUPS, SSD_STATE, SSD_HPG * SSD_HEAD_DIM), F32)
    if final:
        in_specs = [pl.BlockSpec((CHUNK, SSD_XBC), chunk)] + gate_specs + [
            pl.BlockSpec((CHUNK, SSD_INNER), chunk),
            pl.BlockSpec((CHUNK, SSD_INNER), lambda c, o: (o[c], P_Z // SSD_INNER)),
            pl.BlockSpec((1, SSD_INNER), const),
            pl.BlockSpec((1, SSD_INNER), const)]
        args = [xbc_conv] + gate_args + [y_other, proj, lw['ssd_d_e'], lw['ssd_norm_w']]
        out_shape, out_specs, scratch = y_shape, y_spec, [state]
    else:
        in_specs = _halo_specs(SSD_XBC, P_XBC // SSD_XBC, t) + [
            pl.BlockSpec((8, SSD_XBC), const), pl.BlockSpec((1, SSD_XBC), const)] + gate_specs
        args = [proj, proj, proj, lw['ssd_conv_w'], lw['ssd_conv_b']] + gate_args
        out_shape = (y_shape, jax.ShapeDtypeStruct((t, SSD_XBC), BF16))
        out_specs = (y_spec, pl.BlockSpec((CHUNK, SSD_XBC), chunk))
        scratch = [pltpu.VMEM((CHUNK + 2 * HALO, SSD_XBC), F32), state]
    return pl.pallas_call(
        functools.partial(_ssd_kernel, reverse=reverse, final=final, n_ctx_chunks=n_ctx_chunks,
                          n_chunks=n_chunks),
        out_shape=out_shape,
        grid_spec=pltpu.PrefetchScalarGridSpec(
            num_scalar_prefetch=1, grid=(n_chunks,), in_specs=in_specs, out_specs=out_specs,
            scratch_shapes=scratch),
        compiler_params=_cparams(("arbitrary",)),
        name="ssd_bwd" if reverse else "ssd_fwd",
    )(order, *args)


def _mlstm_kernel(order_ref, *refs, reverse, final, n_ctx_chunks, n_chunks):
    if final:
        (qc_ref, v_ref, gates_ref, gbias_row_ref, gbias_col_ref, hb_ref, mo_ref, normw_ref,
         o_ref, cn_ref, m_ref) = refs
    else:
        (qk_ref, qprev_ref, qnext_ref, cw_ref, cb_ref, v_ref, gates_ref, gbias_row_ref, gbias_col_ref,
         o_ref, qc_ref, ext_ref, cn_ref, m_ref) = refs
    c = pl.program_id(0)
    ci = order_ref[c]

    @pl.when(c == 0)
    def _():
        cn_ref[...] = jnp.zeros_like(cn_ref)
        m_ref[...] = jnp.zeros_like(m_ref)

    if final:
        qk = qc_ref[...].astype(F32)
    else:
        first = jnp.logical_or(ci == 0, ci == n_ctx_chunks)
        last = jnp.logical_or(ci == n_ctx_chunks - 1, ci == n_chunks - 1)
        qk = _conv_silu(qk_ref, qprev_ref, qnext_ref, cw_ref, cb_ref, ext_ref, first, last)
        qc_ref[...] = qk.astype(qc_ref.dtype)
    hq = MLSTM_HEADS * MLSTM_QK
    q_b = qk[:, :hq].astype(BF16)
    k = qk[:, hq:] * (MLSTM_QK ** -0.5)
    k_b = k.astype(BF16)
    k_t = k.T
    v_b = v_ref[...]

    icol = G_MG + (2 * MLSTM_HEADS if reverse else 0)
    fcol = icol + MLSTM_HEADS
    g_col = gates_ref[...] + gbias_row_ref[...]
    g_row = gates_ref[...].T + gbias_col_ref[...]
    bcum_col, bcum_row = _cumsums(_log_sigmoid(g_col), _log_sigmoid(g_row), reverse)
    lower, upper = _tri_masks()
    mask = upper if reverse else lower
    ones_col = (lax.broadcasted_iota(jnp.int32, (CHUNK, MLSTM_V), 1) == 0).astype(BF16)

    for h in range(MLSTM_HEADS):
        bc = bcum_col[:, fcol + h:fcol + h + 1]
        br = bcum_row[fcol + h:fcol + h + 1, :]
        ig = g_row[icol + h:icol + h + 1, :]
        dm = jnp.where(mask, bc - br + ig, -jnp.inf)
        m_in = m_ref[h:h + 1, 0:1]
        w_inter = bc + m_in
        m_t = jnp.maximum(w_inter, jnp.max(dm, axis=1, keepdims=True))
        q_h = q_b[:, h * MLSTM_QK:(h + 1) * MLSTM_QK]
        s = _dot_nt(q_h, k_b[:, h * MLSTM_QK:(h + 1) * MLSTM_QK]) * jnp.exp(dm - m_t)
        v_ext = jnp.concatenate([v_b[:, h * MLSTM_V:(h + 1) * MLSTM_V], ones_col], axis=1)
        cn = cn_ref[h]
        tot = _dot(s.astype(BF16), v_ext) + jnp.exp(w_inter - m_t) * _dot(q_h, cn.astype(BF16))
        num = tot[:, :MLSTM_V]
        den = tot[:, MLSTM_V:MLSTM_V + 1]
        hh = num / jnp.maximum(jnp.abs(den), jnp.exp(-m_t))

        b_last = br[:, 0:1] if reverse else br[:, CHUNK - 1:CHUNK]
        w_end = b_last - br + ig
        m_loc = jnp.max(w_end, axis=1, keepdims=True)
        kw_t = (k_t[h * MLSTM_QK:(h + 1) * MLSTM_QK, :] * jnp.exp(w_end - m_loc)).astype(BF16)
        m_new = jnp.maximum(b_last + m_in, m_loc)
        cn_ref[h] = jnp.exp(b_last + m_in - m_new) * cn + jnp.exp(m_loc - m_new) * _dot(kw_t, v_ext)
        m_ref[h:h + 1, :] = jnp.broadcast_to(m_new, (1, GATE_LANES))

        sl = slice(h * MLSTM_V, (h + 1) * MLSTM_V)
        if final:
            hh = hh + hb_ref[:, sl].astype(F32)
            hh = hh * lax.rsqrt(jnp.mean(hh * hh, axis=-1, keepdims=True) + EPS) * normw_ref[:, sl]
            hh = _sigmoid(mo_ref[:, sl].astype(F32)) * hh
        o_ref[:, sl] = hh.astype(o_ref.dtype)


def mlstm_pass(proj, gates, lw, n_ctx, qk_conv=None, h_other=None):
    t = proj.shape[0]
    n_chunks = t // CHUNK
    n_ctx_chunks = n_ctx // CHUNK
    final = qk_conv is not None
    reverse = not final
    order = _chunk_order(n_ctx_chunks, n_chunks, reverse)
    const = lambda c, o: (0, 0)
    chunk = lambda c, o: (o[c], 0)
    common_specs = [pl.BlockSpec((CHUNK, BRANCH_W), lambda c, o: (o[c], P_MV // BRANCH_W)),
                    pl.BlockSpec((CHUNK, GATE_LANES), chunk),
                    pl.BlockSpec((1, GATE_LANES), const),
                    pl.BlockSpec((GATE_LANES, 1), const)]
    common_args = [proj, gates, lw['gbias_row'], lw['gbias_col']]
    h_shape = jax.ShapeDtypeStruct((t, BRANCH_W), BF16)
    h_spec = pl.BlockSpec((CHUNK, BRANCH_W), chunk)
    state = [pltpu.VMEM((MLSTM_HEADS, MLSTM_QK, 2 * MLSTM_V), F32), pltpu.VMEM((MLSTM_HEADS, GATE_LANES), F32)]
    if final:
        in_specs = [pl.BlockSpec((CHUNK, MLSTM_QKW), chunk)] + common_specs + [
            pl.BlockSpec((CHUNK, BRANCH_W), chunk),
            pl.BlockSpec((CHUNK, BRANCH_W), lambda c, o: (o[c], P_MO // BRANCH_W)),
            pl.BlockSpec((1, BRANCH_W), const)]
        args = [qk_conv] + common_args + [h_other, proj, lw['mlstm_norm_w']]
        out_shape, out_specs, scratch = h_shape, h_spec, state
    else:
        in_specs = _halo_specs(MLSTM_QKW, P_QK // MLSTM_QKW, t) + [
            pl.BlockSpec((8, MLSTM_QKW), const), pl.BlockSpec((1, MLSTM_QKW), const)] + common_specs
        args = [proj, proj, proj, lw['mlstm_conv_w'], lw['mlstm_conv_b']] + common_args
        out_shape = (h_shape, jax.ShapeDtypeStruct((t, MLSTM_QKW), BF16))
        out_specs = (h_spec, pl.BlockSpec((CHUNK, MLSTM_QKW), chunk))
        scratch = [pltpu.VMEM((CHUNK + 2 * HALO, MLSTM_QKW), F32)] + state
    return pl.pallas_call(
        functools.partial(_mlstm_kernel, reverse=reverse, final=final, n_ctx_chunks=n_ctx_chunks,
                          n_chunks=n_chunks),
        out_shape=out_shape,
        grid_spec=pltpu.PrefetchScalarGridSpec(
            num_scalar_prefetch=1, grid=(n_chunks,), in_specs=in_specs, out_specs=out_specs,
            scratch_shapes=scratch),
        compiler_params=_cparams(("arbitrary",)),
        name="mlstm_bwd" if reverse else "mlstm_fwd",
    )(order, *args)


def _fft_split(length):
    l1 = 1 << (int(math.log2(length)) // 2)
    return l1, length // l1


def _fft1_kernel(x_ref, cs_ref, f1_ref, twc_ref, tws_ref, ure_ref, uim_ref, *, l1):
    x = x_ref[...]
    cs = cs_ref[...]
    a, b = [], []
    for grp in range(FOURIER_GROUPS):
        ab = _dot(x[:, grp * FOURIER_GW:(grp + 1) * FOURIER_GW], cs)
        a.append(ab[:, :FOURIER_GW])
        b.append(ab[:, FOURIER_GW:])
    ab = jnp.concatenate(a + b, axis=1).astype(BF16)
    prod = _dot(f1_ref[...], ab)
    c_a = prod[:l1, :BRANCH_W]
    c_b = prod[:l1, BRANCH_W:]
    s_a = prod[l1:, :BRANCH_W]
    s_b = prod[l1:, BRANCH_W:]
    u_re = c_a - s_b
    u_im = -(c_b + s_a)
    twc = twc_ref[...]
    tws = tws_ref[...]
    ure_ref[...] = (u_re * twc + u_im * tws).astype(ure_ref.dtype)
    uim_ref[...] = (u_im * twc - u_re * tws).astype(uim_ref.dtype)


def _fft2_kernel(ure_ref, uim_ref, f2_ref, o_ref):
    u = jnp.concatenate([ure_ref[...], uim_ref[...]], axis=0)
    o_ref[...] = _dot(f2_ref[...], u).astype(o_ref.dtype)


def _fft_tables(length):
    l1, l2 = _fft_split(length)
    scale = 1.0 / math.sqrt(length * FOURIER_GW)
    kc = np.arange(FOURIER_GW)
    ang_c = 2.0 * np.pi * np.outer(kc, kc) / FOURIER_GW
    cs = np.concatenate([np.cos(ang_c), np.sin(ang_c)], axis=1) * scale
    k1 = np.arange(l1)
    ang1 = 2.0 * np.pi * np.outer(k1, k1) / l1
    f1 = np.concatenate([np.cos(ang1), np.sin(ang1)], axis=0)
    n2 = np.arange(l2)
    ang_t = 2.0 * np.pi * np.outer(n2, k1) / length
    ang2 = 2.0 * np.pi * np.outer(n2, n2) / l2
    f2 = np.concatenate([np.cos(ang2), np.sin(ang2)], axis=1)
    return (jnp.asarray(cs, BF16), jnp.asarray(f1, BF16), jnp.asarray(np.cos(ang_t)[:, :, None], F32),
            jnp.asarray(np.sin(ang_t)[:, :, None], F32), jnp.asarray(f2, BF16))


def fourier_mix(fx):
    length = fx.shape[0]
    l1, l2 = _fft_split(length)
    cs, f1, twc, tws, f2 = _fft_tables(length)
    u_shape = jax.ShapeDtypeStruct((l1, l2 * BRANCH_W), BF16)
    ure, uim = pl.pallas_call(
        functools.partial(_fft1_kernel, l1=l1),
        out_shape=(u_shape, u_shape),
        grid=(l2,),
        in_specs=[pl.BlockSpec((l1, BRANCH_W), lambda j: (0, j)),
                  pl.BlockSpec((FOURIER_GW, 2 * FOURIER_GW), lambda j: (0, 0)),
                  pl.BlockSpec((2 * l1, l1), lambda j: (0, 0)),
                  pl.BlockSpec((None, l1, 1), lambda j: (j, 0, 0)),
                  pl.BlockSpec((None, l1, 1), lambda j: (j, 0, 0))],
        out_specs=(pl.BlockSpec((l1, BRANCH_W), lambda j: (0, j)),
                   pl.BlockSpec((l1, BRANCH_W), lambda j: (0, j))),
        compiler_params=_cparams(("arbitrary",)),
        name="fft_stage1",
    )(fx.reshape(l1, l2 * BRANCH_W), cs, f1, twc, tws)
    out = pl.pallas_call(
        _fft2_kernel,
        out_shape=jax.ShapeDtypeStruct((l2, l1 * BRANCH_W), BF16),
        grid=(l1,),
        in_specs=[pl.BlockSpec((None, l2, BRANCH_W), lambda i: (i, 0, 0)),
                  pl.BlockSpec((None, l2, BRANCH_W), lambda i: (i, 0, 0)),
                  pl.BlockSpec((l2, 2 * l2), lambda i: (0, 0))],
        out_specs=pl.BlockSpec((l2, BRANCH_W), lambda i: (0, i)),
        compiler_params=_cparams(("arbitrary",)),
        name="fft_stage2",
    )(ure.reshape(l1, l2, BRANCH_W), uim.reshape(l1, l2, BRANCH_W), f2)
    return out.reshape(length, BRANCH_W)


def _sgu_kernel(su_ref, sv_ref, normw_ref, w_ref, b_ref, o_ref):
    v = _gelu_tanh(sv_ref[...].astype(F32))
    v = (v * lax.rsqrt(jnp.mean(v * v, axis=-1, keepdims=True) + EPS) * normw_ref[...]).astype(BF16)
    mix = []
    for grp in range(SGU_GROUPS):
        mix.append(_dot(w_ref[grp], v[:, grp * SGU_GW:(grp + 1) * SGU_GW]) + b_ref[:, grp:grp + 1])
    o_ref[...] = (_gelu_tanh(su_ref[...].astype(F32)) * jnp.concatenate(mix, axis=1)).astype(o_ref.dtype)


def sgu(proj, lw):
    t = proj.shape[0]
    return pl.pallas_call(
        _sgu_kernel,
        out_shape=jax.ShapeDtypeStruct((t, BRANCH_W), BF16),
        grid=(t // CHUNK,),
        in_specs=[pl.BlockSpec((CHUNK, BRANCH_W), lambda c: (c, P_SU // BRANCH_W)),
                  pl.BlockSpec((CHUNK, BRANCH_W), lambda c: (c, P_SV // BRANCH_W)),
                  pl.BlockSpec((1, BRANCH_W), lambda c: (0, 0)),
                  pl.BlockSpec((SGU_GROUPS, CHUNK, CHUNK), lambda c: (0, 0, 0)),
                  pl.BlockSpec((CHUNK, GATE_LANES), lambda c: (0, 0))],
        out_specs=pl.BlockSpec((CHUNK, BRANCH_W), lambda c: (c, 0)),
        compiler_params=_cparams(("arbitrary",)),
        name="sgu",
    )(proj, proj, lw['sgu_norm_w'], lw['sgu_w'], lw['sgu_b_t'])


def _merge_kernel(gd_ref, b0_ref, b1_ref, b2_ref, b3_ref, wg_ref, wb_ref, o_ref):
    gd = gd_ref[...]
    acc = None
    for n, b_ref in enumerate((b0_ref, b1_ref, b2_ref, b3_ref)):
        term = _sigmoid(_dot(gd, wg_ref[n])) * _dot(b_ref[...], wb_ref[n])
        acc = term if acc is None else acc + term
    o_ref[...] = acc.astype(o_ref.dtype)


def merge(proj, branches, w_gate_up, w_branch, layer):
    t = proj.shape[0]
    tm = _pick_tile(t, MM_TM)
    tn = MM_TN
    row = lambda i, j: (i, 0)
    return pl.pallas_call(
        _merge_kernel,
        out_shape=jax.ShapeDtypeStruct((t, D_MODEL), BF16),
        grid=(t // tm, D_MODEL // tn),
        in_specs=[pl.BlockSpec((tm, GATE_RANK), lambda i, j: (i, P_GD // GATE_RANK))] +
                 [pl.BlockSpec((tm, BRANCH_W), row)] * N_BRANCH +
                 [pl.BlockSpec((None, N_BRANCH, GATE_RANK, tn), lambda i, j: (layer, 0, 0, j)),
                  pl.BlockSpec((None, N_BRANCH, BRANCH_W, tn), lambda i, j: (layer, 0, 0, j))],
        out_specs=pl.BlockSpec((tm, tn), lambda i, j: (i, j)),
        compiler_params=_cparams(("arbitrary", "arbitrary")),
        name="merge",
    )(proj, *branches, w_gate_up, w_branch)


def _outproj_kernel(a_ref, w_ref, x_ref, mods_ref, o_ref, *, tm, n_ctx, gate_idx):
    is_ctx = _row_is_ctx(pl.program_id(0), tm, n_ctx)
    o_ref[...] = x_ref[...] + _mod_row(mods_ref, is_ctx, gate_idx) * _dot(a_ref[...], w_ref[...])


def outproj_residual(a, w_stack, layer, x, mods, n_ctx, gate_idx):
    t = a.shape[0]
    tm = _pick_tile(t, MM_TM)
    tn = MM_TN
    return pl.pallas_call(
        functools.partial(_outproj_kernel, tm=tm, n_ctx=n_ctx, gate_idx=gate_idx),
        out_shape=jax.ShapeDtypeStruct((t, D_MODEL), F32),
        grid=(t // tm, D_MODEL // tn),
        in_specs=[pl.BlockSpec((tm, D_MODEL), lambda i, j: (i, 0)),
                  pl.BlockSpec((None, D_MODEL, tn), lambda i, j: (layer, 0, j)),
                  pl.BlockSpec((tm, tn), lambda i, j: (i, j)),
                  pl.BlockSpec((2, MOD_ROWS, tn), lambda i, j: (0, 0, j))],
        out_specs=pl.BlockSpec((tm, tn), lambda i, j: (i, j)),
        compiler_params=_cparams(("arbitrary", "arbitrary")),
        name="outproj_residual",
    )(a, w_stack, x, mods)


def _router_kernel(x_ref, w_ref, mods_ref, wr_ref, rb_ref, h_ref, idx_ref, wts_ref, rank_ref, cnt_ref, base_ref,
                   *, tm, n_ctx):
    @pl.when(pl.program_id(0) == 0)
    def _():
        base_ref[...] = jnp.zeros_like(base_ref)

    x = x_ref[...]
    is_ctx = _row_is_ctx(pl.program_id(0), tm, n_ctx)
    y = x * lax.rsqrt(jnp.mean(x * x, axis=-1, keepdims=True) + EPS) * w_ref[...]
    h = y * (1.0 + _mod_row(mods_ref, is_ctx, 4)) + _mod_row(mods_ref, is_ctx, 3)
    h_ref[...] = _pack_halves(h)
    logits = lax.dot_general(wr_ref[...], h, (((1,), (1,)), ((), ())), preferred_element_type=F32,
                             precision=HIGHEST)
    aff = _sigmoid(logits)
    sel = aff + rb_ref[...]
    rows = [sel[e:e + 1, :] for e in range(N_EXPERTS)]
    affs = [aff[e:e + 1, :] for e in range(N_EXPERTS)]
    gscore = []
    for grp in range(N_EXPERT_GROUPS):
        r = rows[grp * EXPERTS_PER_GROUP:(grp + 1) * EXPERTS_PER_GROUP]
        best = None
        for a in range(EXPERTS_PER_GROUP):
            for b in range(a + 1, EXPERTS_PER_GROUP):
                pair = r[a] + r[b]
                best = pair if best is None else jnp.maximum(best, pair)
        gscore.append(best)
    g_best = gscore[0]
    g_idx = jnp.zeros_like(g_best, dtype=jnp.int32)
    for grp in range(1, N_EXPERT_GROUPS):
        better = gscore[grp] > g_best
        g_best = jnp.where(better, gscore[grp], g_best)
        g_idx = jnp.where(better, grp, g_idx)
    masked = [jnp.where(g_idx == e // EXPERTS_PER_GROUP, rows[e], -jnp.inf) for e in range(N_EXPERTS)]

    def argbest(exclude):
        best_v = jnp.full_like(g_best, -jnp.inf)
        best_i = jnp.full_like(g_idx, -1)
        best_a = jnp.zeros_like(g_best)
        for e in range(N_EXPERTS):
            ok = masked[e] > best_v
            if exclude is not None:
                ok = jnp.logical_and(ok, exclude != e)
            best_v = jnp.where(ok, masked[e], best_v)
            best_i = jnp.where(ok, e, best_i)
            best_a = jnp.where(ok, affs[e], best_a)
        return best_i, best_a

    i1, a1 = argbest(None)
    i2, a2 = argbest(i1)
    tot = a1 + a2
    idx_ref[...] = jnp.concatenate([i1, i2], axis=0)
    wts_ref[...] = jnp.concatenate([a1 / tot, a2 / tot], axis=0)

    e_iota = lax.broadcasted_iota(jnp.int32, (N_EXPERTS, tm), 0)
    hit1 = e_iota == i1
    hit2 = e_iota == i2
    onehot = jnp.where(jnp.logical_or(hit1, hit2), 1.0, 0.0)
    r = lax.broadcasted_iota(jnp.int32, (tm, tm), 0)
    cidx = lax.broadcasted_iota(jnp.int32, (tm, tm), 1)
    before = jnp.where(r < cidx, 1.0, 0.0).astype(BF16)
    prior = base_ref[:, 0:1] + _dot(onehot.astype(BF16), before)
    rank1 = jnp.sum(jnp.where(hit1, prior, 0.0), axis=0, keepdims=True)
    rank2 = jnp.sum(jnp.where(hit2, prior, 0.0), axis=0, keepdims=True)
    rank_ref[...] = jnp.concatenate([rank1, rank2], axis=0).astype(jnp.int32)
    total = base_ref[...] + jnp.sum(onehot, axis=1, keepdims=True)
    base_ref[...] = total
    cnt_ref[...] = total


def norm_router(x, norm_w, mods, w_router_t, router_bias, n_ctx):
    t = x.shape[0]
    tm = ROW_TILE
    sel = lambda i: (0, i)
    return pl.pallas_call(
        functools.partial(_router_kernel, tm=tm, n_ctx=n_ctx),
        out_shape=(jax.ShapeDtypeStruct((t, D_MODEL // 2), jnp.uint32),
                   jax.ShapeDtypeStruct((TOP_K, t), jnp.int32),
                   jax.ShapeDtypeStruct((TOP_K, t), F32),
                   jax.ShapeDtypeStruct((TOP_K, t), jnp.int32),
                   jax.ShapeDtypeStruct((N_EXPERTS, GATE_LANES), F32)),
        grid=(t // tm,),
        in_specs=[pl.BlockSpec((tm, D_MODEL), lambda i: (i, 0)),
                  pl.BlockSpec((1, D_MODEL), lambda i: (0, 0)),
                  pl.BlockSpec((2, MOD_ROWS, D_MODEL), lambda i: (0, 0, 0)),
                  pl.BlockSpec((N_EXPERTS, D_MODEL), lambda i: (0, 0)),
                  pl.BlockSpec((N_EXPERTS, 1), lambda i: (0, 0))],
        out_specs=(pl.BlockSpec((tm, D_MODEL // 2), lambda i: (i, 0)),
                   pl.BlockSpec((TOP_K, tm), sel), pl.BlockSpec((TOP_K, tm), sel), pl.BlockSpec((TOP_K, tm), sel),
                   pl.BlockSpec((N_EXPERTS, GATE_LANES), lambda i: (0, 0))),
        scratch_shapes=[pltpu.VMEM((N_EXPERTS, GATE_LANES), F32)],
        compiler_params=_cparams(("arbitrary",)),
        name="norm_router",
    )(x, norm_w.reshape(1, D_MODEL), mods, w_router_t, router_bias.reshape(N_EXPERTS, 1))


def _routing_metadata(idx, rank, counts, n_tiles):
    tm = EXPERT_TM
    counts = counts[:, 0].astype(jnp.int32)
    padded = ((counts + tm - 1) // tm) * tm
    ends = jnp.cumsum(padded)
    starts = ends - padded
    experts_iota = jnp.arange(N_EXPERTS, dtype=jnp.int32)
    start_of = jnp.sum(jnp.where(idx[:, :, None] == experts_iota, starts, 0), axis=-1)
    pos = (start_of + rank).reshape(-1).astype(jnp.int32)
    n_used = (ends[-1] // tm).astype(jnp.int32)
    tile_start = jnp.arange(n_tiles, dtype=jnp.int32) * tm
    tile_expert = jnp.sum((tile_start[:, None] >= ends[None, :]).astype(jnp.int32), axis=1)
    last_used = jnp.sum((jnp.maximum(n_used - 1, 0) * tm >= ends).astype(jnp.int32))
    tile_expert = jnp.minimum(jnp.where(tile_start < n_used * tm, tile_expert, last_used), N_EXPERTS - 1)
    return pos, tile_expert.astype(jnp.int32), n_used.reshape(1), (starts + counts).astype(jnp.int32), \
        (padded - counts).astype(jnp.int32)


ZERO_ROWS = 64
DMA_UNROLL = 4
COMBINE_ROWS = 32
EXPERT_SUBTILES = 2


def _dispatch_kernel(pos_ref, padstart_ref, npad_ref, nused_ref, h_ref, o_hbm, stage_ref, zero_ref, sems,
                     *, tm, n_tok, n_rows):
    i = pl.program_id(0)
    n = pl.num_programs(0)
    slot = lax.rem(i, 2)
    used_rows = nused_ref[0] * EXPERT_TM
    n_trail = (n_rows - used_rows) // ZERO_ROWS

    def row_copy(s, src_row, dst_row):
        return pltpu.make_async_copy(stage_ref.at[s, pl.ds(src_row, 1), :], o_hbm.at[pl.ds(dst_row, 1), :],
                                     sems.at[s])

    def pad_copy(dst_row):
        return pltpu.make_async_copy(zero_ref.at[pl.ds(0, 1), :], o_hbm.at[pl.ds(dst_row, 1), :], sems.at[2])

    def trail_copy(j):
        row = pl.multiple_of(used_rows + j * ZERO_ROWS, ZERO_ROWS)
        return pltpu.make_async_copy(zero_ref, o_hbm.at[pl.ds(row, ZERO_ROWS), :], sems.at[2])

    def pad_loop(fn, trail_fn):
        for e in range(N_EXPERTS):
            def body(r, carry, e=e):
                fn(padstart_ref[e] + r)
                return carry
            lax.fori_loop(0, npad_ref[e], body, 0)

        def trail_body(j, carry):
            trail_fn(j)
            return carry
        lax.fori_loop(0, n_trail, trail_body, 0)

    @pl.when(i == 0)
    def _():
        zero_ref[...] = jnp.zeros_like(zero_ref)
        pad_loop(lambda row: pad_copy(row).start(), lambda j: trail_copy(j).start())

    def issue(r, carry):
        tok = i * tm + r
        for k in range(TOP_K):
            row_copy(slot, r, pos_ref[k * n_tok + tok]).start()
        return carry

    def wait_tile(s):
        def body(r, carry):
            for k in range(TOP_K):
                row_copy(s, 0, 0).wait()
            return carry
        lax.fori_loop(0, tm, body, 0, unroll=DMA_UNROLL)

    stage_ref[slot] = h_ref[...]
    lax.fori_loop(0, tm, issue, 0, unroll=DMA_UNROLL)

    @pl.when(i > 0)
    def _():
        wait_tile(1 - slot)

    @pl.when(i == n - 1)
    def _():
        wait_tile(slot)
        pad_loop(lambda row: pad_copy(0).wait(), lambda j: trail_copy(0).wait())


def dispatch(h_packed, pos, pad_start, n_pad, n_used, n_rows):
    t = h_packed.shape[0]
    tm = ROW_TILE
    return pl.pallas_call(
        functools.partial(_dispatch_kernel, tm=tm, n_tok=t, n_rows=n_rows),
        out_shape=jax.ShapeDtypeStruct((n_rows, D_MODEL // 2), jnp.uint32),
        grid_spec=pltpu.PrefetchScalarGridSpec(
            num_scalar_prefetch=4, grid=(t // tm,),
            in_specs=[pl.BlockSpec((tm, D_MODEL // 2), lambda i, *_: (i, 0))],
            out_specs=pl.BlockSpec(memory_space=pl.ANY),
            scratch_shapes=[pltpu.VMEM((2, tm, D_MODEL // 2), jnp.uint32),
                            pltpu.VMEM((ZERO_ROWS, D_MODEL // 2), jnp.uint32), pltpu.SemaphoreType.DMA((3,))]),
        compiler_params=_cparams(("arbitrary",)),
        name="dispatch",
    )(pos, pad_start, n_pad, n_used, h_packed)


def _experts_kernel(texp_ref, nused_ref, x_ref, win_ref, wout_ref, o_ref):
    i = pl.program_id(0)
    half = D_MODEL // 2

    @pl.when(i < nused_ref[0])
    def _():
        sub = x_ref.shape[0] // EXPERT_SUBTILES
        for s in range(EXPERT_SUBTILES):
            rs = slice(s * sub, (s + 1) * sub)
            xp = x_ref[rs, :]
            gu = (_dot(_unpack_lo(xp).astype(BF16), win_ref[:half, :]) +
                  _dot(_unpack_hi(xp).astype(BF16), win_ref[half:, :]))
            act = (_silu(gu[:, :D_EXPERT]) * gu[:, D_EXPERT:]).astype(BF16)
            o_ref[rs, :] = _pack_halves(_dot(act, wout_ref[...]))

    @pl.when(i >= nused_ref[0])
    def _():
        o_ref[...] = jnp.zeros_like(o_ref)


def experts(x_sorted, tile_expert, n_used, w_exp_in, w_exp_out, layer):
    tm = EXPERT_TM
    n_tiles = tile_expert.shape[0]
    half = D_MODEL // 2
    return pl.pallas_call(
        _experts_kernel,
        out_shape=jax.ShapeDtypeStruct((n_tiles * tm, half), jnp.uint32),
        grid_spec=pltpu.PrefetchScalarGridSpec(
            num_scalar_prefetch=2, grid=(n_tiles,),
            in_specs=[pl.BlockSpec((tm, half), lambda i, te, nu: (jnp.minimum(i, nu[0] - 1), 0)),
                      pl.BlockSpec((None, None, D_MODEL, 2 * D_EXPERT), lambda i, te, nu: (layer, te[i], 0, 0)),
                      pl.BlockSpec((None, None, D_EXPERT, D_MODEL), lambda i, te, nu: (layer, te[i], 0, 0))],
            out_specs=pl.BlockSpec((tm, half), lambda i, te, nu: (i, 0))),
        compiler_params=_cparams(("arbitrary",)),
        name="experts",
    )(tile_expert, n_used, x_sorted, w_exp_in, w_exp_out)


def _combine_kernel(pos_ref, x_ref, wts_ref, mods_ref, nw_ref, nmods_ref, y_hbm, *rest, tm, n_ctx, n_tok, last):
    if last:
        f_ref, ybuf, sems = rest
    else:
        xo_ref, h_ref, ybuf, sems = rest
    i = pl.program_id(0)
    n = pl.num_programs(0)
    slot = lax.rem(i, 2)
    half = D_MODEL // 2

    def row_copy(src_row, s, k, r):
        return pltpu.make_async_copy(y_hbm.at[pl.ds(src_row, 1), :], ybuf.at[s, k, pl.ds(r, 1), :], sems.at[s])

    def issue(tile, s):
        def body(r, carry):
            for k in range(TOP_K):
                row_copy(pos_ref[k * n_tok + tile * tm + r], s, k, r).start()
            return carry
        lax.fori_loop(0, tm, body, 0, unroll=DMA_UNROLL)

    @pl.when(i == 0)
    def _():
        issue(0, 0)

    @pl.when(i + 1 < n)
    def _():
        issue(i + 1, 1 - slot)

    def wait_body(r, carry):
        for k in range(TOP_K):
            row_copy(0, slot, k, r).wait()
        return carry
    lax.fori_loop(0, tm, wait_body, 0, unroll=DMA_UNROLL)

    sel = jnp.where(i * tm < n_ctx, 0, 1)
    lo, hi = slice(0, half), slice(half, D_MODEL)

    def rows(b, carry):
        rs = pl.ds(pl.multiple_of(b * COMBINE_ROWS, COMBINE_ROWS), COMBINE_ROWS)
        w = wts_ref[rs, :]
        y0 = ybuf[slot, 0, rs, :]
        y1 = ybuf[slot, 1, rs, :]
        gate = mods_ref[sel, 5:6, :]
        halves = []
        for sl, unpack in ((lo, _unpack_lo), (hi, _unpack_hi)):
            moe = w[:, 0:1] * unpack(y0) + w[:, 1:2] * unpack(y1)
            halves.append(x_ref[rs, sl] + gate[:, sl] * moe)
        ssq = sum(jnp.sum(v * v, axis=-1, keepdims=True) for v in halves)
        inv = lax.rsqrt(ssq / D_MODEL + EPS)
        for v, sl in zip(halves, (lo, hi)):
            normed = v * inv * nw_ref[:, sl]
            if last:
                f_ref[rs, sl] = normed
            else:
                xo_ref[rs, sl] = v
                scale = nmods_ref[sel, 1:2, :][:, sl]
                shift = nmods_ref[sel, 0:1, :][:, sl]
                h_ref[rs, sl] = (normed * (1.0 + scale) + shift).astype(h_ref.dtype)
        return carry
    lax.fori_loop(0, tm // COMBINE_ROWS, rows, 0)


def combine(x, y_sorted, pos, wts_t, mods, next_norm_w, next_mods, n_ctx, last):
    t = x.shape[0]
    tm = CHUNK
    full = lambda i, p: (i, 0)
    if last:
        ctx_tiles = n_ctx // tm
        out_shape = jax.ShapeDtypeStruct((t - n_ctx, D_MODEL), F32)
        out_specs = pl.BlockSpec((tm, D_MODEL), lambda i, p: (jnp.maximum(i - ctx_tiles, 0), 0))
    else:
        out_shape = (jax.ShapeDtypeStruct((t, D_MODEL), F32), jax.ShapeDtypeStruct((t, D_MODEL), BF16))
        out_specs = (pl.BlockSpec((tm, D_MODEL), full), pl.BlockSpec((tm, D_MODEL), full))
    return pl.pallas_call(
        functools.partial(_combine_kernel, tm=tm, n_ctx=n_ctx, n_tok=t, last=last),
        out_shape=out_shape,
        grid_spec=pltpu.PrefetchScalarGridSpec(
            num_scalar_prefetch=1, grid=(t // tm,),
            in_specs=[pl.BlockSpec((tm, D_MODEL), full),
                      pl.BlockSpec((tm, TOP_K), full),
                      pl.BlockSpec((2, MOD_ROWS, D_MODEL), lambda i, p: (0, 0, 0)),
                      pl.BlockSpec((1, D_MODEL), lambda i, p: (0, 0)),
                      pl.BlockSpec((2, MOD_ROWS, D_MODEL), lambda i, p: (0, 0, 0)),
                      pl.BlockSpec(memory_space=pl.ANY)],
            out_specs=out_specs,
            scratch_shapes=[pltpu.VMEM((2, TOP_K, tm, D_MODEL // 2), jnp.uint32),
                            pltpu.SemaphoreType.DMA((2,))]),
        compiler_params=_cparams(("arbitrary",)),
        name="combine_last" if last else "combine",
    )(pos, x, wts_t, mods, next_norm_w.reshape(1, D_MODEL), next_mods, y_sorted)


_MAIN_PIECES = ((O_XBC, SSD_XBC), (O_GD, GATE_RANK), (O_Z, SSD_INNER), (O_FX, BRANCH_W), (O_MQ, MLSTM_QKW),
                (O_MV, BRANCH_W), (O_MO, BRANCH_W), (O_SU, BRANCH_W), (O_SV, BRANCH_W))


def _relayout_w_in_kernel(w_ref, main_ref, gates_ref):
    col = 0
    for off, width in _MAIN_PIECES:
        main_ref[:, col:col + width] = w_ref[:, off:off + width].astype(BF16)
        col += width
    gates_ref[...] = jnp.zeros_like(gates_ref)
    gates_ref[:, G_DT:G_DT + 2 * SSD_HEADS] = w_ref[:, O_DT:O_DT + 2 * SSD_HEADS].astype(BF16)
    gates_ref[:, G_MG:G_MG + 4 * MLSTM_HEADS] = w_ref[:, O_MG:O_MG + 4 * MLSTM_HEADS].astype(BF16)


def relayout_w_in(w_in):
    depth, d, d_in = w_in.shape
    tk = ROW_TILE
    return pl.pallas_call(
        _relayout_w_in_kernel,
        out_shape=(jax.ShapeDtypeStruct((depth, d, P_TOTAL), BF16),
                   jax.ShapeDtypeStruct((depth, d, GATE_LANES), BF16)),
        grid=(depth, d // tk),
        in_specs=[pl.BlockSpec((None, tk, d_in), lambda l, i: (l, i, 0))],
        out_specs=(pl.BlockSpec((None, tk, P_TOTAL), lambda l, i: (l, i, 0)),
                   pl.BlockSpec((None, tk, GATE_LANES), lambda l, i: (l, i, 0))),
        compiler_params=_cparams(("arbitrary", "arbitrary")),
        name="relayout_w_in",
    )(w_in)


def _prep_layer_params(l, p):
    pad = GATE_LANES - 2 * SSD_HEADS - 4 * MLSTM_HEADS
    gbias = jnp.concatenate([p['ssd_dt_bias'][l].reshape(-1), p['mlstm_gate_b'][l].reshape(-1),
                             jnp.zeros((pad,), F32)])
    alog = jnp.concatenate([p['ssd_a_log'][l].reshape(-1), jnp.zeros((GATE_LANES - 2 * SSD_HEADS,), F32)])
    feat = np.arange(GATE_LANES)[:, None]
    head = np.arange(SSD_INNER)[None, :] // SSD_HEAD_DIM
    pad_k = lambda w: jnp.concatenate([w, jnp.zeros((8 - CONV_K, w.shape[1]), F32)], axis=0)
    return {
        'gbias_row': gbias.reshape(1, -1), 'gbias_col': gbias.reshape(-1, 1),
        'alog_row': alog.reshape(1, -1), 'alog_col': alog.reshape(-1, 1),
        'expand_f': jnp.asarray(feat == head, BF16), 'expand_b': jnp.asarray(feat == head + SSD_HEADS, BF16),
        'ssd_conv_w': pad_k(p['ssd_conv_w'][l]), 'ssd_conv_b': p['ssd_conv_b'][l].reshape(1, -1),
        'ssd_d_e': jnp.repeat(p['ssd_d'][l], SSD_HEAD_DIM).reshape(1, -1),
        'ssd_norm_w': p['ssd_norm_w'][l].reshape(1, -1),
        'mlstm_conv_w': pad_k(p['mlstm_conv_w'][l]), 'mlstm_conv_b': p['mlstm_conv_b'][l].reshape(1, -1),
        'mlstm_norm_w': p['mlstm_norm_w'][l].reshape(1, -1),
        'sgu_norm_w': p['sgu_norm_w'][l].reshape(1, -1),
        'sgu_w': p['sgu_w'][l].astype(BF16),
        'sgu_b_t': jnp.concatenate([p['sgu_b'][l].T, jnp.zeros((CHUNK, GATE_LANES - SGU_GROUPS), F32)], axis=1),
    }


def kernel(x, c, ctx, c_ctx, ada_down, ada_up, ada_b, norm1_w, norm2_w, w_in, ssd_conv_w, ssd_conv_b,
           ssd_dt_bias, ssd_a_log, ssd_d, ssd_norm_w, mlstm_conv_w, mlstm_conv_b, mlstm_gate_b, mlstm_norm_w,
           sgu_norm_w, sgu_w, sgu_b, w_gate_up, w_branch, w_out, w_router, router_bias, w_exp_in, w_exp_out,
           final_norm_w):
    assert x.shape[0] == 1 and ctx.shape[0] == 1, "single-sequence kernel"
    w_main, w_gates = relayout_w_in(w_in)
    p = dict(w_main=w_main, w_gates=w_gates, ssd_conv_w=ssd_conv_w, ssd_conv_b=ssd_conv_b,
             ssd_dt_bias=ssd_dt_bias, ssd_a_log=ssd_a_log, ssd_d=ssd_d, ssd_norm_w=ssd_norm_w,
             mlstm_conv_w=mlstm_conv_w, mlstm_conv_b=mlstm_conv_b, mlstm_gate_b=mlstm_gate_b,
             mlstm_norm_w=mlstm_norm_w, sgu_norm_w=sgu_norm_w, sgu_w=sgu_w, sgu_b=sgu_b,
             w_gate_up=w_gate_up.astype(BF16), w_branch=w_branch.astype(BF16), w_out=w_out.astype(BF16),
             w_exp_in=w_exp_in.astype(BF16), w_exp_out=w_exp_out.astype(BF16))
    depth = w_in.shape[0]
    seq = x.shape[1]
    n_ctx = ctx.shape[1]
    t = n_ctx + seq
    assert n_ctx % ROW_TILE == 0 and seq % ROW_TILE == 0
    n_tiles = -(-(TOP_K * t) // EXPERT_TM) + N_EXPERTS

    xs = jnp.concatenate([ctx[0], x[0]], axis=0)
    cvec = jnp.concatenate([c_ctx[None], c, jnp.zeros((6, D_MODEL), F32)], axis=0)
    mods_all = ada_mods(cvec, ada_down, ada_up, ada_b)
    mods_all = mods_all[:, :2].reshape(depth, 2, N_MOD, D_MODEL)
    mods_all = jnp.concatenate([mods_all, jnp.zeros((depth, 2, MOD_ROWS - N_MOD, D_MODEL), F32)], axis=2)
    w_router_t = w_router.T

    h = norm_mod(xs, norm1_w[0], mods_all[0], n_ctx, 0, 1, BF16)
    for l in range(depth):
        lw = _prep_layer_params(l, p)
        mods = mods_all[l]
        last = l == depth - 1
        proj = matmul(h, p['w_main'], l, BF16, MM_TN)
        gates = matmul(h, p['w_gates'], l, F32, GATE_LANES)

        ys_b, xbc_conv = ssd_pass(proj, gates, lw, n_ctx)
        y_ssd = ssd_pass(proj, gates, lw, n_ctx, xbc_conv, ys_b)
        hm_b, qk_conv = mlstm_pass(proj, gates, lw, n_ctx)
        y_ml = mlstm_pass(proj, gates, lw, n_ctx, qk_conv, hm_b)
        fx = proj[:, P_FX:P_FX + BRANCH_W]
        y_four = jnp.concatenate([fourier_mix(fx[:n_ctx]), fourier_mix(fx[n_ctx:])], axis=0)
        y_sgu = sgu(proj, lw)

        merged = merge(proj, (y_ssd, y_four, y_ml, y_sgu), p['w_gate_up'], p['w_branch'], l)
        xs = outproj_residual(merged, p['w_out'], l, xs, mods, n_ctx, 2)

        h_packed, idx, wts, rank, counts = norm_router(xs, norm2_w[l], mods, w_router_t, router_bias, n_ctx)
        pos, tile_expert, n_used, pad_start, n_pad = _routing_metadata(idx, rank, counts, n_tiles)
        x_sorted = dispatch(h_packed, pos, pad_start, n_pad, n_used, n_tiles * EXPERT_TM)
        y_sorted = experts(x_sorted, tile_expert, n_used, p['w_exp_in'], p['w_exp_out'], l)
        if last:
            out = combine(xs, y_sorted, pos, wts.T, mods, final_norm_w, mods, n_ctx, True)
        else:
            xs, h = combine(xs, y_sorted, pos, wts.T, mods, norm1_w[l + 1], mods_all[l + 1], n_ctx, False)
    return out[None]
```

```python
import functools
import math

import numpy as np
import jax
import jax.numpy as jnp
from jax import lax
from jax.experimental import pallas as pl
from jax.experimental.pallas import tpu as pltpu

F32 = jnp.float32
BF16 = jnp.bfloat16
HIGHEST = lax.Precision.HIGHEST

D_MODEL = 4096
CHUNK = 128
EPS = 1e-6
ADA_RANK = 256
N_MOD = 6
MOD_ROWS = 8
N_BRANCH = 4
BRANCH_W = 1024
GATE_RANK = 512
SSD_HEADS = 16
SSD_HEAD_DIM = 64
SSD_INNER = SSD_HEADS * SSD_HEAD_DIM
SSD_GROUPS = 4
SSD_HPG = SSD_HEADS // SSD_GROUPS
SSD_STATE = 64
SSD_XBC = SSD_INNER + 2 * SSD_GROUPS * SSD_STATE
CONV_K = 5
CONV_PAD = CONV_K // 2
FOURIER_GROUPS = 4
FOURIER_GW = BRANCH_W // FOURIER_GROUPS
MLSTM_HEADS = 8
MLSTM_QK = 64
MLSTM_V = 128
MLSTM_QKW = 2 * MLSTM_HEADS * MLSTM_QK
SGU_GROUPS = 4
SGU_GW = BRANCH_W // SGU_GROUPS
N_EXPERTS = 16
N_EXPERT_GROUPS = 4
EXPERTS_PER_GROUP = N_EXPERTS // N_EXPERT_GROUPS
TOP_K = 2
D_EXPERT = 640

IN_SPLITS = (SSD_INNER, SSD_XBC, 2 * SSD_HEADS, BRANCH_W, MLSTM_HEADS * MLSTM_QK, MLSTM_HEADS * MLSTM_QK,
             MLSTM_HEADS * MLSTM_V, MLSTM_HEADS * MLSTM_V, 4 * MLSTM_HEADS, BRANCH_W, BRANCH_W, GATE_RANK)
_OFF = [0] + [int(v) for v in np.cumsum(IN_SPLITS)]
(O_Z, O_XBC, O_DT, O_FX, O_MQ, O_MK, O_MV, O_MO, O_MG, O_SU, O_SV, O_GD) = _OFF[:-1]

P_XBC = 0
P_GD = SSD_XBC
P_Z = P_GD + GATE_RANK
P_FX = P_Z + BRANCH_W
P_QK = P_FX + BRANCH_W
P_MV = P_QK + MLSTM_QKW
P_MO = P_MV + BRANCH_W
P_SU = P_MO + BRANCH_W
P_SV = P_SU + BRANCH_W
P_TOTAL = P_SV + BRANCH_W
GATE_LANES = 128
G_DT = 0
G_MG = 2 * SSD_HEADS

HALO = 16
VMEM_LIMIT = 56 * 1024 * 1024

ROW_TILE = 256
MM_TM = 1280
MM_TN = 512
EXPERT_TM = 512


def _cparams(sem):
    return pltpu.CompilerParams(dimension_semantics=sem, vmem_limit_bytes=VMEM_LIMIT)


def _pick_tile(n, pref):
    t = min(pref, n)
    while n % t:
        t -= ROW_TILE if t > ROW_TILE else 8
    return t


def _sigmoid(x):
    return 1.0 / (1.0 + jnp.exp(-x))


def _silu(x):
    return x * _sigmoid(x)


def _softplus(x):
    return jnp.maximum(x, 0.0) + jnp.log(1.0 + jnp.exp(-jnp.abs(x)))


def _log_sigmoid(x):
    return -_softplus(-x)


def _gelu_tanh(x):
    return 0.5 * x * (1.0 + jnp.tanh(math.sqrt(2.0 / math.pi) * (x + 0.044715 * (x * x * x))))


def _dot(a, b):
    return jnp.dot(a, b, preferred_element_type=F32)


def _dot_nt(a, b):
    return lax.dot_general(a, b, (((1,), (1,)), ((), ())), preferred_element_type=F32)


def _dot_hi(a, b):
    return jnp.dot(a, b, preferred_element_type=F32, precision=HIGHEST)


def _split_bf16(x, parts=3):
    out, rest = [], x
    for _ in range(parts):
        piece = rest.astype(BF16)
        out.append(piece)
        rest = rest - piece.astype(F32)
    return out


def _dot_f32_by_exact(a, b_exact):
    terms = [_dot(piece, b_exact) for piece in _split_bf16(a)]
    return terms[0] + terms[1] + terms[2]


def _dot_exact_by_f32(a_exact, b):
    terms = [_dot(a_exact, piece) for piece in _split_bf16(b)]
    return terms[0] + terms[1] + terms[2]


HI_MASK = 0xFFFF0000


def _pack_halves(v):
    bits = lax.bitcast_convert_type(v.astype(BF16).astype(F32), jnp.uint32)
    n = v.shape[1] // 2
    return (bits[:, :n] >> 16) | (bits[:, n:] & jnp.uint32(HI_MASK))


def _unpack_lo(p):
    return lax.bitcast_convert_type(p << 16, F32)


def _unpack_hi(p):
    return lax.bitcast_convert_type(p & jnp.uint32(HI_MASK), F32)


def _row_is_ctx(i, tm, n_ctx):
    row = i * tm + lax.broadcasted_iota(jnp.int32, (tm, 1), 0)
    return row < n_ctx


def _mod_row(mods_ref, is_ctx, idx):
    return jnp.where(is_ctx, mods_ref[0, idx:idx + 1, :], mods_ref[1, idx:idx + 1, :])


def _ada_kernel(c_ref, down_ref, up_ref, b_ref, o_ref, low_ref):
    @pl.when(pl.program_id(1) == 0)
    def _():
        low_ref[...] = _dot_hi(_silu(c_ref[...]), down_ref[...])

    o_ref[...] = _dot_hi(low_ref[...], up_ref[...]) + b_ref[...]


def ada_mods(cvec, ada_down, ada_up, ada_b):
    depth = ada_down.shape[0]
    n_out = ada_up.shape[2]
    tn = 2048
    return pl.pallas_call(
        _ada_kernel,
        out_shape=jax.ShapeDtypeStruct((depth, 8, n_out), F32),
        grid=(depth, n_out // tn),
        in_specs=[pl.BlockSpec((8, D_MODEL), lambda l, j: (0, 0)),
                  pl.BlockSpec((None, D_MODEL, ADA_RANK), lambda l, j: (l, 0, 0)),
                  pl.BlockSpec((None, ADA_RANK, tn), lambda l, j: (l, 0, j)),
                  pl.BlockSpec((None, 1, tn), lambda l, j: (l, 0, j))],
        out_specs=pl.BlockSpec((None, 8, tn), lambda l, j: (l, 0, j)),
        scratch_shapes=[pltpu.VMEM((8, ADA_RANK), F32)],
        compiler_params=_cparams(("arbitrary", "arbitrary")),
        name="ada_mods",
    )(cvec, ada_down, ada_up, ada_b.reshape(depth, 1, n_out))


def _norm_kernel(x_ref, w_ref, mods_ref, o_ref, *, tm, n_ctx, shift_idx, scale_idx, row_off):
    x = x_ref[...]
    y = x * lax.rsqrt(jnp.mean(x * x, axis=-1, keepdims=True) + EPS) * w_ref[...]
    if shift_idx is not None:
        is_ctx = _row_is_ctx(pl.program_id(0) + row_off, tm, n_ctx)
        y = y * (1.0 + _mod_row(mods_ref, is_ctx, scale_idx)) + _mod_row(mods_ref, is_ctx, shift_idx)
    o_ref[...] = y.astype(o_ref.dtype)


def norm_mod(x, w, mods, n_ctx, shift_idx, scale_idx, out_dtype, row_start=0):
    t = x.shape[0] - row_start
    tm = ROW_TILE
    row_off = row_start // tm
    return pl.pallas_call(
        functools.partial(_norm_kernel, tm=tm, n_ctx=n_ctx, shift_idx=shift_idx, scale_idx=scale_idx,
                          row_off=row_off),
        out_shape=jax.ShapeDtypeStruct((t, D_MODEL), out_dtype),
        grid=(t // tm,),
        in_specs=[pl.BlockSpec((tm, D_MODEL), lambda i: (i + row_off, 0)),
                  pl.BlockSpec((1, D_MODEL), lambda i: (0, 0)),
                  pl.BlockSpec((2, MOD_ROWS, D_MODEL), lambda i: (0, 0, 0))],
        out_specs=pl.BlockSpec((tm, D_MODEL), lambda i: (i, 0)),
        compiler_params=_cparams(("arbitrary",)),
        name="norm_mod",
    )(x, w.reshape(1, D_MODEL), mods)


def _mm_kernel(a_ref, b_ref, o_ref):
    o_ref[...] = _dot(a_ref[...], b_ref[...]).astype(o_ref.dtype)


def matmul(a, b_stack, layer, out_dtype, tn):
    m, k = a.shape
    n = b_stack.shape[2]
    tm = _pick_tile(m, MM_TM)
    return pl.pallas_call(
        _mm_kernel,
        out_shape=jax.ShapeDtypeStruct((m, n), out_dtype),
        grid=(m // tm, n // tn),
        in_specs=[pl.BlockSpec((tm, k), lambda i, j: (i, 0)),
                  pl.BlockSpec((None, k, tn), lambda i, j: (layer, 0, j))],
        out_specs=pl.BlockSpec((tm, tn), lambda i, j: (i, j)),
        compiler_params=_cparams(("arbitrary", "arbitrary")),
        name="matmul",
    )(a, b_stack)


def _conv_silu(cur_ref, prev_ref, next_ref, cw_ref, cb_ref, ext_ref, first, last):
    prev = jnp.where(first, 0.0, prev_ref[...].astype(F32))
    nxt = jnp.where(last, 0.0, next_ref[...].astype(F32))
    ext_ref[0:HALO, :] = prev
    ext_ref[HALO:HALO + CHUNK, :] = cur_ref[...].astype(F32)
    ext_ref[HALO + CHUNK:2 * HALO + CHUNK, :] = nxt
    acc = cb_ref[...] + cw_ref[0:1, :] * ext_ref[pl.ds(HALO - CONV_PAD, CHUNK), :]
    for k in range(1, CONV_K):
        acc = acc + cw_ref[k:k + 1, :] * ext_ref[pl.ds(HALO - CONV_PAD + k, CHUNK), :]
    return _silu(acc)


def _tri_masks():
    r = lax.broadcasted_iota(jnp.int32, (CHUNK, CHUNK), 0)
    c = lax.broadcasted_iota(jnp.int32, (CHUNK, CHUNK), 1)
    return c <= r, c >= r


def _cumsums(col, row, reverse):
    lower, upper = _tri_masks()
    lo = jnp.where(lower, 1.0, 0.0).astype(BF16)
    up = jnp.where(upper, 1.0, 0.0).astype(BF16)
    if reverse:
        return _dot_exact_by_f32(up, col), _dot_f32_by_exact(row, lo)
    return _dot_exact_by_f32(lo, col), _dot_f32_by_exact(row, up)


def _chunk_order(n_ctx_chunks, n_chunks, reverse):
    if not reverse:
        return jnp.arange(n_chunks, dtype=jnp.int32)
    return jnp.concatenate([jnp.arange(n_ctx_chunks - 1, -1, -1, dtype=jnp.int32),
                            jnp.arange(n_chunks - 1, n_ctx_chunks - 1, -1, dtype=jnp.int32)])


def _halo_specs(width, col_block, n_rows):
    per = CHUNK // HALO
    last_halo = n_rows // HALO - 1
    cur = pl.BlockSpec((CHUNK, width), lambda c, o: (o[c], col_block))
    prev = pl.BlockSpec((HALO, width), lambda c, o: (jnp.maximum(o[c] * per - 1, 0), col_block))
    nxt = pl.BlockSpec((HALO, width), lambda c, o: (jnp.minimum((o[c] + 1) * per, last_halo), col_block))
    return [cur, prev, nxt]


def _ssd_kernel(order_ref, *refs, reverse, final, n_ctx_chunks, n_chunks, init=True):
    if final:
        (xc_ref, gates_ref, gbias_row_ref, alog_row_ref, alog_col_ref, gbias_col_ref, expand_ref,
         yb_ref, z_ref, dskip_ref, normw_ref, o_ref, h_ref) = refs
    else:
        (xbc_ref, xprev_ref, xnext_ref, cw_ref, cb_ref, gates_ref, gbias_row_ref, alog_row_ref, alog_col_ref,
         gbias_col_ref, expand_ref, o_ref, xc_ref, ext_ref, h_ref) = refs
    c = pl.program_id(0)
    ci = order_ref[c]

    if init:
        @pl.when(c == 0)
        def _():
            h_ref[...] = jnp.zeros_like(h_ref)

    if final:
        xbc = xc_ref[...].astype(F32)
    else:
        first = jnp.logical_or(ci == 0, ci == n_ctx_chunks)
        last = jnp.logical_or(ci == n_ctx_chunks - 1, ci == n_chunks - 1)
        xbc = _conv_silu(xbc_ref, xprev_ref, xnext_ref, cw_ref, cb_ref, ext_ref, first, last)
        xc_ref[...] = xbc.astype(xc_ref.dtype)
    xs = xbc[:, :SSD_INNER]
    bm = xbc[:, SSD_INNER:SSD_INNER + SSD_GROUPS * SSD_STATE]
    cm = xbc[:, SSD_INNER + SSD_GROUPS * SSD_STATE:]

    lane0 = SSD_HEADS if reverse else 0
    g = gates_ref[...]
    dt_col = _softplus(g + gbias_row_ref[...])
    dt_row = _softplus(g.T + gbias_col_ref[...])
    acum_col, acum_row = _cumsums(dt_col * -jnp.exp(alog_row_ref[...]),
                                  dt_row * -jnp.exp(alog_col_ref[...]), reverse)
    both_e = _dot_f32_by_exact(jnp.concatenate([dt_col, acum_col], axis=0), expand_ref[...])
    dt_e = both_e[:CHUNK]
    acum_e = both_e[CHUNK:]
    alast_e = acum_e[0:1, :] if reverse else acum_e[CHUNK - 1:CHUNK, :]
    xdt = xs * dt_e
    xdt_b = xdt.astype(BF16)
    xw_b = (xdt * jnp.exp(alast_e - acum_e)).astype(BF16)
    dec_e = jnp.exp(alast_e)
    lower, upper = _tri_masks()
    mask = upper if reverse else lower
    cm_b = cm.astype(BF16)
    bm_b = bm.astype(BF16)
    bm_t = bm.T.astype(BF16)
    lane = lax.broadcasted_iota(jnp.int32, (CHUNK, 2 * SSD_HEAD_DIM), 1)
    gw = SSD_HPG * SSD_HEAD_DIM
    ys, inters = [], []
    for grp in range(SSD_GROUPS):
        c_g = cm_b[:, grp * SSD_STATE:(grp + 1) * SSD_STATE]
        b_g = bm_b[:, grp * SSD_STATE:(grp + 1) * SSD_STATE]
        cb = _dot_nt(c_g, b_g)
        h_t = h_ref[grp]
        inters.append(_dot(c_g, h_t.astype(BF16)))
        for pair in range(SSD_HPG // 2):
            ps = []
            for k in range(2):
                f = lane0 + grp * SSD_HPG + pair * 2 + k
                seg = acum_col[:, f:f + 1] - acum_row[f:f + 1, :]
                ps.append((cb * jnp.exp(jnp.where(mask, seg, -jnp.inf))).astype(BF16))
            col0 = (grp * (SSD_HPG // 2) + pair) * 2 * SSD_HEAD_DIM
            xp = xdt_b[:, col0:col0 + 2 * SSD_HEAD_DIM]
            rhs = jnp.concatenate([jnp.where(lane < SSD_HEAD_DIM, xp, jnp.zeros_like(xp)),
                                   jnp.where(lane >= SSD_HEAD_DIM, xp, jnp.zeros_like(xp))], axis=0)
            ys.append(_dot(jnp.concatenate(ps, axis=1), rhs))
        new = _dot(bm_t[grp * SSD_STATE:(grp + 1) * SSD_STATE, :], xw_b[:, grp * gw:(grp + 1) * gw])
        h_ref[grp] = dec_e[:, grp * gw:(grp + 1) * gw] * h_t + new
    y = jnp.concatenate(ys, axis=1) + jnp.exp(acum_e) * jnp.concatenate(inters, axis=1)
    if final:
        y = y + yb_ref[...].astype(F32) + dskip_ref[...] * xs
        z = z_ref[...].astype(F32)
        y = y * _silu(z)
        y = y * lax.rsqrt(jnp.mean(y * y, axis=-1, keepdims=True) + EPS) * normw_ref[...]
    o_ref[...] = y.astype(o_ref.dtype)


def ssd_pass(proj, gates, lw, n_ctx, xbc_conv=None, y_other=None):
    t = proj.shape[0]
    n_chunks = t // CHUNK
    n_ctx_chunks = n_ctx // CHUNK
    final = xbc_conv is not None
    reverse = not final
    order = _chunk_order(n_ctx_chunks, n_chunks, reverse)
    const = lambda c, o: (0, 0)
    chunk = lambda c, o: (o[c], 0)
    gate_specs = [pl.BlockSpec((CHUNK, GATE_LANES), chunk),
                  pl.BlockSpec((1, GATE_LANES), const),
                  pl.BlockSpec((1, GATE_LANES), const),
                  pl.BlockSpec((GATE_LANES, 1), const),
                  pl.BlockSpec((GATE_LANES, 1), const),
                  pl.BlockSpec((GATE_LANES, SSD_INNER), const)]
    gate_args = [gates, lw['gbias_row'], lw['alog_row'], lw['alog_col'], lw['gbias_col'],
                 lw['expand_b' if reverse else 'expand_f']]
    y_shape = jax.ShapeDtypeStruct((t, SSD_INNER), BF16)
    y_spec = pl.BlockSpec((CHUNK, SSD_INNER), chunk)
    state = pltpu.VMEM((SSD_GROUPS, SSD_STATE, SSD_HPG * SSD_HEAD_DIM), F32)
    if final:
        in_specs = [pl.BlockSpec((CHUNK, SSD_XBC), chunk)] + gate_specs + [
            pl.BlockSpec((CHUNK, SSD_INNER), chunk),
            pl.BlockSpec((CHUNK, SSD_INNER), lambda c, o: (o[c], P_Z // SSD_INNER)),
            pl.BlockSpec((1, SSD_INNER), const),
            pl.BlockSpec((1, SSD_INNER), const)]
        args = [xbc_conv] + gate_args + [y_other, proj, lw['ssd_d_e'], lw['ssd_norm_w']]
        out_shape, out_specs, scratch = [y_shape], [y_spec], [state]
    else:
        in_specs = _halo_specs(SSD_XBC, P_XBC // SSD_XBC, t) + [
            pl.BlockSpec((8, SSD_XBC), const), pl.BlockSpec((1, SSD_XBC), const)] + gate_specs
        args = [proj, proj, proj, lw['ssd_conv_w'], lw['ssd_conv_b']] + gate_args
        out_shape = [y_shape, jax.ShapeDtypeStruct((t, SSD_XBC), BF16)]
        out_specs = [y_spec, pl.BlockSpec((CHUNK, SSD_XBC), chunk)]
        scratch = [pltpu.VMEM((CHUNK + 2 * HALO, SSD_XBC), F32), state]
    return dict(in_specs=in_specs, args=args, out_shape=out_shape, out_specs=out_specs, scratch=scratch,
                n_state=1)


def _mlstm_kernel(order_ref, *refs, reverse, final, n_ctx_chunks, n_chunks, init=True):
    if final:
        (qc_ref, v_ref, gates_ref, gbias_row_ref, gbias_col_ref, hb_ref, mo_ref, normw_ref,
         o_ref, cn_ref, m_ref) = refs
    else:
        (qk_ref, qprev_ref, qnext_ref, cw_ref, cb_ref, v_ref, gates_ref, gbias_row_ref, gbias_col_ref,
         o_ref, qc_ref, ext_ref, cn_ref, m_ref) = refs
    c = pl.program_id(0)
    ci = order_ref[c]

    if init:
        @pl.when(c == 0)
        def _():
            cn_ref[...] = jnp.zeros_like(cn_ref)
            m_ref[...] = jnp.zeros_like(m_ref)

    if final:
        qk = qc_ref[...].astype(F32)
    else:
        first = jnp.logical_or(ci == 0, ci == n_ctx_chunks)
        last = jnp.logical_or(ci == n_ctx_chunks - 1, ci == n_chunks - 1)
        qk = _conv_silu(qk_ref, qprev_ref, qnext_ref, cw_ref, cb_ref, ext_ref, first, last)
        qc_ref[...] = qk.astype(qc_ref.dtype)
    hq = MLSTM_HEADS * MLSTM_QK
    q_b = qk[:, :hq].astype(BF16)
    k = qk[:, hq:] * (MLSTM_QK ** -0.5)
    k_b = k.astype(BF16)
    k_t = k.T
    v_b = v_ref[...]

    icol = G_MG + (2 * MLSTM_HEADS if reverse else 0)
    fcol = icol + MLSTM_HEADS
    g_col = gates_ref[...] + gbias_row_ref[...]
    g_row = gates_ref[...].T + gbias_col_ref[...]
    bcum_col, bcum_row = _cumsums(_log_sigmoid(g_col), _log_sigmoid(g_row), reverse)
    lower, upper = _tri_masks()
    mask = upper if reverse else lower
    ones_col = (lax.broadcasted_iota(jnp.int32, (CHUNK, MLSTM_V), 1) == 0).astype(BF16)

    for h in range(MLSTM_HEADS):
        bc = bcum_col[:, fcol + h:fcol + h + 1]
        br = bcum_row[fcol + h:fcol + h + 1, :]
        ig = g_row[icol + h:icol + h + 1, :]
        dm = jnp.where(mask, bc - br + ig, -jnp.inf)
        m_in = m_ref[h:h + 1, 0:1]
        w_inter = bc + m_in
        m_t = jnp.maximum(w_inter, jnp.max(dm, axis=1, keepdims=True))
        q_h = q_b[:, h * MLSTM_QK:(h + 1) * MLSTM_QK]
        s = _dot_nt(q_h, k_b[:, h * MLSTM_QK:(h + 1) * MLSTM_QK]) * jnp.exp(dm - m_t)
        v_ext = jnp.concatenate([v_b[:, h * MLSTM_V:(h + 1) * MLSTM_V], ones_col], axis=1)
        cn = cn_ref[h]
        tot = _dot(s.astype(BF16), v_ext) + jnp.exp(w_inter - m_t) * _dot(q_h, cn.astype(BF16))
        num = tot[:, :MLSTM_V]
        den = tot[:, MLSTM_V:MLSTM_V + 1]
        hh = num / jnp.maximum(jnp.abs(den), jnp.exp(-m_t))

        b_last = br[:, 0:1] if reverse else br[:, CHUNK - 1:CHUNK]
        w_end = b_last - br + ig
        m_loc = jnp.max(w_end, axis=1, keepdims=True)
        kw_t = (k_t[h * MLSTM_QK:(h + 1) * MLSTM_QK, :] * jnp.exp(w_end - m_loc)).astype(BF16)
        m_new = jnp.maximum(b_last + m_in, m_loc)
        cn_ref[h] = jnp.exp(b_last + m_in - m_new) * cn + jnp.exp(m_loc - m_new) * _dot(kw_t, v_ext)
        m_ref[h:h + 1, :] = jnp.broadcast_to(m_new, (1, GATE_LANES))

        sl = slice(h * MLSTM_V, (h + 1) * MLSTM_V)
        if final:
            hh = hh + hb_ref[:, sl].astype(F32)
            hh = hh * lax.rsqrt(jnp.mean(hh * hh, axis=-1, keepdims=True) + EPS) * normw_ref[:, sl]
            hh = _sigmoid(mo_ref[:, sl].astype(F32)) * hh
        o_ref[:, sl] = hh.astype(o_ref.dtype)


def mlstm_pass(proj, gates, lw, n_ctx, qk_conv=None, h_other=None):
    t = proj.shape[0]
    n_chunks = t // CHUNK
    n_ctx_chunks = n_ctx // CHUNK
    final = qk_conv is not None
    reverse = not final
    order = _chunk_order(n_ctx_chunks, n_chunks, reverse)
    const = lambda c, o: (0, 0)
    chunk = lambda c, o: (o[c], 0)
    common_specs = [pl.BlockSpec((CHUNK, BRANCH_W), lambda c, o: (o[c], P_MV // BRANCH_W)),
                    pl.BlockSpec((CHUNK, GATE_LANES), chunk),
                    pl.BlockSpec((1, GATE_LANES), const),
                    pl.BlockSpec((GATE_LANES, 1), const)]
    common_args = [proj, gates, lw['gbias_row'], lw['gbias_col']]
    h_shape = jax.ShapeDtypeStruct((t, BRANCH_W), BF16)
    h_spec = pl.BlockSpec((CHUNK, BRANCH_W), chunk)
    state = [pltpu.VMEM((MLSTM_HEADS, MLSTM_QK, 2 * MLSTM_V), F32), pltpu.VMEM((MLSTM_HEADS, GATE_LANES), F32)]
    if final:
        in_specs = [pl.BlockSpec((CHUNK, MLSTM_QKW), chunk)] + common_specs + [
            pl.BlockSpec((CHUNK, BRANCH_W), chunk),
            pl.BlockSpec((CHUNK, BRANCH_W), lambda c, o: (o[c], P_MO // BRANCH_W)),
            pl.BlockSpec((1, BRANCH_W), const)]
        args = [qk_conv] + common_args + [h_other, proj, lw['mlstm_norm_w']]
        out_shape, out_specs, scratch = [h_shape], [h_spec], state
    else:
        in_specs = _halo_specs(MLSTM_QKW, P_QK // MLSTM_QKW, t) + [
            pl.BlockSpec((8, MLSTM_QKW), const), pl.BlockSpec((1, MLSTM_QKW), const)] + common_specs
        args = [proj, proj, proj, lw['mlstm_conv_w'], lw['mlstm_conv_b']] + common_args
        out_shape = [h_shape, jax.ShapeDtypeStruct((t, MLSTM_QKW), BF16)]
        out_specs = [h_spec, pl.BlockSpec((CHUNK, MLSTM_QKW), chunk)]
        scratch = [pltpu.VMEM((CHUNK + 2 * HALO, MLSTM_QKW), F32)] + state
    return dict(in_specs=in_specs, args=args, out_shape=out_shape, out_specs=out_specs, scratch=scratch,
                n_state=2)


def _scan_pair_kernel(order_ref, *refs, n_in, n_out, n_scr, n_state, **static):
    groups, at = [], 0
    for counts in (n_in, n_out, n_scr):
        pair = []
        for cnt in counts:
            pair.append(refs[at:at + cnt])
            at += cnt
        groups.append(pair)
    (in_s, in_m), (out_s, out_m), (scr_s, scr_m) = groups

    @pl.when(pl.program_id(0) == 0)
    def _():
        for ref in scr_s[len(scr_s) - n_state[0]:] + scr_m[len(scr_m) - n_state[1]:]:
            ref[...] = jnp.zeros_like(ref)

    _ssd_kernel(order_ref, *in_s, *out_s, *scr_s, init=False, **static)
    _mlstm_kernel(order_ref, *in_m, *out_m, *scr_m, init=False, **static)


def scan_pair(proj, gates, lw, n_ctx, convs=None, others=None):
    t = proj.shape[0]
    n_chunks = t // CHUNK
    n_ctx_chunks = n_ctx // CHUNK
    final = convs is not None
    reverse = not final
    order = _chunk_order(n_ctx_chunks, n_chunks, reverse)
    ssd = ssd_pass(proj, gates, lw, n_ctx, *((convs[0], others[0]) if final else ()))
    mls = mlstm_pass(proj, gates, lw, n_ctx, *((convs[1], others[1]) if final else ()))
    count = lambda key: (len(ssd[key]), len(mls[key]))
    return pl.pallas_call(
        functools.partial(_scan_pair_kernel, n_in=count('in_specs'), n_out=count('out_shape'),
                          n_scr=count('scratch'), n_state=(ssd['n_state'], mls['n_state']),
                          reverse=reverse, final=final, n_ctx_chunks=n_ctx_chunks, n_chunks=n_chunks),
        out_shape=tuple(ssd['out_shape'] + mls['out_shape']),
        grid_spec=pltpu.PrefetchScalarGridSpec(
            num_scalar_prefetch=1, grid=(n_chunks,), in_specs=ssd['in_specs'] + mls['in_specs'],
            out_specs=tuple(ssd['out_specs'] + mls['out_specs']),
            scratch_shapes=ssd['scratch'] + mls['scratch']),
        compiler_params=_cparams(("arbitrary",)),
        name="scan_bwd" if reverse else "scan_fwd",
    )(order, *ssd['args'], *mls['args'])


def _fft_split(length):
    l1 = 1 << (int(math.log2(length)) // 2)
    return l1, length // l1


def _fft1_kernel(x_ref, cs_ref, f1_ref, twc_ref, tws_ref, ure_ref, uim_ref, *, l1):
    x = x_ref[...]
    cs = cs_ref[...]
    a, b = [], []
    for grp in range(FOURIER_GROUPS):
        ab = _dot(x[:, grp * FOURIER_GW:(grp + 1) * FOURIER_GW], cs)
        a.append(ab[:, :FOURIER_GW])
        b.append(ab[:, FOURIER_GW:])
    ab = jnp.concatenate(a + b, axis=1).astype(BF16)
    prod = _dot(f1_ref[...], ab)
    c_a = prod[:l1, :BRANCH_W]
    c_b = prod[:l1, BRANCH_W:]
    s_a = prod[l1:, :BRANCH_W]
    s_b = prod[l1:, BRANCH_W:]
    u_re = c_a - s_b
    u_im = -(c_b + s_a)
    twc = twc_ref[...]
    tws = tws_ref[...]
    ure_ref[...] = (u_re * twc + u_im * tws).astype(ure_ref.dtype)
    uim_ref[...] = (u_im * twc - u_re * tws).astype(uim_ref.dtype)


def _fft2_kernel(ure_ref, uim_ref, f2_ref, o_ref):
    u = jnp.concatenate([ure_ref[...], uim_ref[...]], axis=0)
    o_ref[...] = _dot(f2_ref[...], u).astype(o_ref.dtype)


def _fft_tables(length):
    l1, l2 = _fft_split(length)
    scale = 1.0 / math.sqrt(length * FOURIER_GW)
    kc = np.arange(FOURIER_GW)
    ang_c = 2.0 * np.pi * np.outer(kc, kc) / FOURIER_GW
    cs = np.concatenate([np.cos(ang_c), np.sin(ang_c)], axis=1) * scale
    k1 = np.arange(l1)
    ang1 = 2.0 * np.pi * np.outer(k1, k1) / l1
    f1 = np.concatenate([np.cos(ang1), np.sin(ang1)], axis=0)
    n2 = np.arange(l2)
    ang_t = 2.0 * np.pi * np.outer(n2, k1) / length
    ang2 = 2.0 * np.pi * np.outer(n2, n2) / l2
    f2 = np.concatenate([np.cos(ang2), np.sin(ang2)], axis=1)
    return (jnp.asarray(cs, BF16), jnp.asarray(f1, BF16), jnp.asarray(np.cos(ang_t)[:, :, None], F32),
            jnp.asarray(np.sin(ang_t)[:, :, None], F32), jnp.asarray(f2, BF16))


def fourier_mix(fx):
    length = fx.shape[0]
    l1, l2 = _fft_split(length)
    cs, f1, twc, tws, f2 = _fft_tables(length)
    u_shape = jax.ShapeDtypeStruct((l1, l2 * BRANCH_W), BF16)
    ure, uim = pl.pallas_call(
        functools.partial(_fft1_kernel, l1=l1),
        out_shape=(u_shape, u_shape),
        grid=(l2,),
        in_specs=[pl.BlockSpec((l1, BRANCH_W), lambda j: (0, j)),
                  pl.BlockSpec((FOURIER_GW, 2 * FOURIER_GW), lambda j: (0, 0)),
                  pl.BlockSpec((2 * l1, l1), lambda j: (0, 0)),
                  pl.BlockSpec((None, l1, 1), lambda j: (j, 0, 0)),
                  pl.BlockSpec((None, l1, 1), lambda j: (j, 0, 0))],
        out_specs=(pl.BlockSpec((l1, BRANCH_W), lambda j: (0, j)),
                   pl.BlockSpec((l1, BRANCH_W), lambda j: (0, j))),
        compiler_params=_cparams(("arbitrary",)),
        name="fft_stage1",
    )(fx.reshape(l1, l2 * BRANCH_W), cs, f1, twc, tws)
    out = pl.pallas_call(
        _fft2_kernel,
        out_shape=jax.ShapeDtypeStruct((l2, l1 * BRANCH_W), BF16),
        grid=(l1,),
        in_specs=[pl.BlockSpec((None, l2, BRANCH_W), lambda i: (i, 0, 0)),
                  pl.BlockSpec((None, l2, BRANCH_W), lambda i: (i, 0, 0)),
                  pl.BlockSpec((l2, 2 * l2), lambda i: (0, 0))],
        out_specs=pl.BlockSpec((l2, BRANCH_W), lambda i: (0, i)),
        compiler_params=_cparams(("arbitrary",)),
        name="fft_stage2",
    )(ure.reshape(l1, l2, BRANCH_W), uim.reshape(l1, l2, BRANCH_W), f2)
    return out.reshape(length, BRANCH_W)


def _sgu_kernel(su_ref, sv_ref, normw_ref, w_ref, b_ref, o_ref):
    v = _gelu_tanh(sv_ref[...].astype(F32))
    v = (v * lax.rsqrt(jnp.mean(v * v, axis=-1, keepdims=True) + EPS) * normw_ref[...]).astype(BF16)
    mix = []
    for grp in range(SGU_GROUPS):
        mix.append(_dot(w_ref[grp], v[:, grp * SGU_GW:(grp + 1) * SGU_GW]) + b_ref[:, grp:grp + 1])
    o_ref[...] = (_gelu_tanh(su_ref[...].astype(F32)) * jnp.concatenate(mix, axis=1)).astype(o_ref.dtype)


def sgu(proj, lw):
    t = proj.shape[0]
    return pl.pallas_call(
        _sgu_kernel,
        out_shape=jax.ShapeDtypeStruct((t, BRANCH_W), BF16),
        grid=(t // CHUNK,),
        in_specs=[pl.BlockSpec((CHUNK, BRANCH_W), lambda c: (c, P_SU // BRANCH_W)),
                  pl.BlockSpec((CHUNK, BRANCH_W), lambda c: (c, P_SV // BRANCH_W)),
                  pl.BlockSpec((1, BRANCH_W), lambda c: (0, 0)),
                  pl.BlockSpec((SGU_GROUPS, CHUNK, CHUNK), lambda c: (0, 0, 0)),
                  pl.BlockSpec((CHUNK, GATE_LANES), lambda c: (0, 0))],
        out_specs=pl.BlockSpec((CHUNK, BRANCH_W), lambda c: (c, 0)),
        compiler_params=_cparams(("arbitrary",)),
        name="sgu",
    )(proj, proj, lw['sgu_norm_w'], lw['sgu_w'], lw['sgu_b_t'])


def _merge_kernel(gd_ref, b0_ref, b1_ref, b2_ref, b3_ref, wg_ref, wb_ref, o_ref):
    gd = gd_ref[...]
    acc = None
    for n, b_ref in enumerate((b0_ref, b1_ref, b2_ref, b3_ref)):
        term = _sigmoid(_dot(gd, wg_ref[n])) * _dot(b_ref[...], wb_ref[n])
        acc = term if acc is None else acc + term
    o_ref[...] = acc.astype(o_ref.dtype)


def merge(proj, branches, w_gate_up, w_branch, layer):
    t = proj.shape[0]
    tm = _pick_tile(t, MM_TM)
    tn = MM_TN
    row = lambda i, j: (i, 0)
    return pl.pallas_call(
        _merge_kernel,
        out_shape=jax.ShapeDtypeStruct((t, D_MODEL), BF16),
        grid=(t // tm, D_MODEL // tn),
        in_specs=[pl.BlockSpec((tm, GATE_RANK), lambda i, j: (i, P_GD // GATE_RANK))] +
                 [pl.BlockSpec((tm, BRANCH_W), row)] * N_BRANCH +
                 [pl.BlockSpec((None, N_BRANCH, GATE_RANK, tn), lambda i, j: (layer, 0, 0, j)),
                  pl.BlockSpec((None, N_BRANCH, BRANCH_W, tn), lambda i, j: (layer, 0, 0, j))],
        out_specs=pl.BlockSpec((tm, tn), lambda i, j: (i, j)),
        compiler_params=_cparams(("arbitrary", "arbitrary")),
        name="merge",
    )(proj, *branches, w_gate_up, w_branch)


def _outproj_kernel(a_ref, w_ref, x_ref, mods_ref, o_ref, *, tm, n_ctx, gate_idx):
    is_ctx = _row_is_ctx(pl.program_id(0), tm, n_ctx)
    o_ref[...] = x_ref[...] + _mod_row(mods_ref, is_ctx, gate_idx) * _dot(a_ref[...], w_ref[...])


def outproj_residual(a, w_stack, layer, x, mods, n_ctx, gate_idx):
    t = a.shape[0]
    tm = _pick_tile(t, MM_TM)
    tn = MM_TN
    return pl.pallas_call(
        functools.partial(_outproj_kernel, tm=tm, n_ctx=n_ctx, gate_idx=gate_idx),
        out_shape=jax.ShapeDtypeStruct((t, D_MODEL), F32),
        grid=(t // tm, D_MODEL // tn),
        in_specs=[pl.BlockSpec((tm, D_MODEL), lambda i, j: (i, 0)),
                  pl.BlockSpec((None, D_MODEL, tn), lambda i, j: (layer, 0, j)),
                  pl.BlockSpec((tm, tn), lambda i, j: (i, j)),
                  pl.BlockSpec((2, MOD_ROWS, tn), lambda i, j: (0, 0, j))],
        out_specs=pl.BlockSpec((tm, tn), lambda i, j: (i, j)),
        compiler_params=_cparams(("arbitrary", "arbitrary")),
        name="outproj_residual",
    )(a, w_stack, x, mods)


def _router_kernel(x_ref, w_ref, mods_ref, wr_ref, rb_ref, h_ref, idx_ref, wts_ref, rank_ref, cnt_ref, base_ref,
                   *, tm, n_ctx):
    @pl.when(pl.program_id(0) == 0)
    def _():
        base_ref[...] = jnp.zeros_like(base_ref)

    x = x_ref[...]
    is_ctx = _row_is_ctx(pl.program_id(0), tm, n_ctx)
    y = x * lax.rsqrt(jnp.mean(x * x, axis=-1, keepdims=True) + EPS) * w_ref[...]
    h = y * (1.0 + _mod_row(mods_ref, is_ctx, 4)) + _mod_row(mods_ref, is_ctx, 3)
    h_ref[...] = _pack_halves(h)
    logits = lax.dot_general(wr_ref[...], h, (((1,), (1,)), ((), ())), preferred_element_type=F32,
                             precision=HIGHEST)
    aff = _sigmoid(logits)
    sel = aff + rb_ref[...]
    rows = [sel[e:e + 1, :] for e in range(N_EXPERTS)]
    affs = [aff[e:e + 1, :] for e in range(N_EXPERTS)]
    gscore = []
    for grp in range(N_EXPERT_GROUPS):
        r = rows[grp * EXPERTS_PER_GROUP:(grp + 1) * EXPERTS_PER_GROUP]
        best = None
        for a in range(EXPERTS_PER_GROUP):
            for b in range(a + 1, EXPERTS_PER_GROUP):
                pair = r[a] + r[b]
                best = pair if best is None else jnp.maximum(best, pair)
        gscore.append(best)
    g_best = gscore[0]
    g_idx = jnp.zeros_like(g_best, dtype=jnp.int32)
    for grp in range(1, N_EXPERT_GROUPS):
        better = gscore[grp] > g_best
        g_best = jnp.where(better, gscore[grp], g_best)
        g_idx = jnp.where(better, grp, g_idx)
    masked = [jnp.where(g_idx == e // EXPERTS_PER_GROUP, rows[e], -jnp.inf) for e in range(N_EXPERTS)]

    def argbest(exclude):
        best_v = jnp.full_like(g_best, -jnp.inf)
        best_i = jnp.full_like(g_idx, -1)
        best_a = jnp.zeros_like(g_best)
        for e in range(N_EXPERTS):
            ok = masked[e] > best_v
            if exclude is not None:
                ok = jnp.logical_and(ok, exclude != e)
            best_v = jnp.where(ok, masked[e], best_v)
            best_i = jnp.where(ok, e, best_i)
            best_a = jnp.where(ok, affs[e], best_a)
        return best_i, best_a

    i1, a1 = argbest(None)
    i2, a2 = argbest(i1)
    tot = a1 + a2
    idx_ref[...] = jnp.concatenate([i1, i2], axis=0)
    wts_ref[...] = jnp.concatenate([a1 / tot, a2 / tot], axis=0)

    e_iota = lax.broadcasted_iota(jnp.int32, (N_EXPERTS, tm), 0)
    hit1 = e_iota == i1
    hit2 = e_iota == i2
    onehot = jnp.where(jnp.logical_or(hit1, hit2), 1.0, 0.0)
    r = lax.broadcasted_iota(jnp.int32, (tm, tm), 0)
    cidx = lax.broadcasted_iota(jnp.int32, (tm, tm), 1)
    before = jnp.where(r < cidx, 1.0, 0.0).astype(BF16)
    prior = base_ref[:, 0:1] + _dot(onehot.astype(BF16), before)
    rank1 = jnp.sum(jnp.where(hit1, prior, 0.0), axis=0, keepdims=True)
    rank2 = jnp.sum(jnp.where(hit2, prior, 0.0), axis=0, keepdims=True)
    rank_ref[...] = jnp.concatenate([rank1, rank2], axis=0).astype(jnp.int32)
    total = base_ref[...] + jnp.sum(onehot, axis=1, keepdims=True)
    base_ref[...] = total
    cnt_ref[...] = total


def norm_router(x, norm_w, mods, w_router_t, router_bias, n_ctx):
    t = x.shape[0]
    tm = ROW_TILE
    sel = lambda i: (0, i)
    return pl.pallas_call(
        functools.partial(_router_kernel, tm=tm, n_ctx=n_ctx),
        out_shape=(jax.ShapeDtypeStruct((t, D_MODEL // 2), jnp.uint32),
                   jax.ShapeDtypeStruct((TOP_K, t), jnp.int32),
                   jax.ShapeDtypeStruct((TOP_K, t), F32),
                   jax.ShapeDtypeStruct((TOP_K, t), jnp.int32),
                   jax.ShapeDtypeStruct((N_EXPERTS, GATE_LANES), F32)),
        grid=(t // tm,),
        in_specs=[pl.BlockSpec((tm, D_MODEL), lambda i: (i, 0)),
                  pl.BlockSpec((1, D_MODEL), lambda i: (0, 0)),
                  pl.BlockSpec((2, MOD_ROWS, D_MODEL), lambda i: (0, 0, 0)),
                  pl.BlockSpec((N_EXPERTS, D_MODEL), lambda i: (0, 0)),
                  pl.BlockSpec((N_EXPERTS, 1), lambda i: (0, 0))],
        out_specs=(pl.BlockSpec((tm, D_MODEL // 2), lambda i: (i, 0)),
                   pl.BlockSpec((TOP_K, tm), sel), pl.BlockSpec((TOP_K, tm), sel), pl.BlockSpec((TOP_K, tm), sel),
                   pl.BlockSpec((N_EXPERTS, GATE_LANES), lambda i: (0, 0))),
        scratch_shapes=[pltpu.VMEM((N_EXPERTS, GATE_LANES), F32)],
        compiler_params=_cparams(("arbitrary",)),
        name="norm_router",
    )(x, norm_w.reshape(1, D_MODEL), mods, w_router_t, router_bias.reshape(N_EXPERTS, 1))


def _routing_metadata(idx, rank, counts, n_tiles):
    tm = EXPERT_TM
    counts = counts[:, 0].astype(jnp.int32)
    padded = ((counts + tm - 1) // tm) * tm
    ends = jnp.cumsum(padded)
    starts = ends - padded
    experts_iota = jnp.arange(N_EXPERTS, dtype=jnp.int32)
    start_of = jnp.sum(jnp.where(idx[:, :, None] == experts_iota, starts, 0), axis=-1)
    pos = (start_of + rank).reshape(-1).astype(jnp.int32)
    n_used = (ends[-1] // tm).astype(jnp.int32)
    tile_start = jnp.arange(n_tiles, dtype=jnp.int32) * tm
    tile_expert = jnp.sum((tile_start[:, None] >= ends[None, :]).astype(jnp.int32), axis=1)
    last_used = jnp.sum((jnp.maximum(n_used - 1, 0) * tm >= ends).astype(jnp.int32))
    tile_expert = jnp.minimum(jnp.where(tile_start < n_used * tm, tile_expert, last_used), N_EXPERTS - 1)
    return pos, tile_expert.astype(jnp.int32), n_used.reshape(1), (starts + counts).astype(jnp.int32), \
        (padded - counts).astype(jnp.int32)


ZERO_ROWS = 64
DMA_UNROLL = 4
COMBINE_ROWS = 32
EXPERT_SUBTILES = 2


def _dispatch_kernel(pos_ref, padstart_ref, npad_ref, nused_ref, h_ref, o_hbm, stage_ref, zero_ref, sems,
                     *, tm, n_tok, n_rows):
    i = pl.program_id(0)
    n = pl.num_programs(0)
    slot = lax.rem(i, 2)
    used_rows = nused_ref[0] * EXPERT_TM
    n_trail = (n_rows - used_rows) // ZERO_ROWS

    def row_copy(s, src_row, dst_row):
        return pltpu.make_async_copy(stage_ref.at[s, pl.ds(src_row, 1), :], o_hbm.at[pl.ds(dst_row, 1), :],
                                     sems.at[s])

    def pad_copy(dst_row):
        return pltpu.make_async_copy(zero_ref.at[pl.ds(0, 1), :], o_hbm.at[pl.ds(dst_row, 1), :], sems.at[2])

    def trail_copy(j):
        row = pl.multiple_of(used_rows + j * ZERO_ROWS, ZERO_ROWS)
        return pltpu.make_async_copy(zero_ref, o_hbm.at[pl.ds(row, ZERO_ROWS), :], sems.at[2])

    def pad_loop(fn, trail_fn):
        for e in range(N_EXPERTS):
            def body(r, carry, e=e):
                fn(padstart_ref[e] + r)
                return carry
            lax.fori_loop(0, npad_ref[e], body, 0)

        def trail_body(j, carry):
            trail_fn(j)
            return carry
        lax.fori_loop(0, n_trail, trail_body, 0)

    @pl.when(i == 0)
    def _():
        zero_ref[...] = jnp.zeros_like(zero_ref)
        pad_loop(lambda row: pad_copy(row).start(), lambda j: trail_copy(j).start())

    def issue(r, carry):
        tok = i * tm + r
        for k in range(TOP_K):
            row_copy(slot, r, pos_ref[k * n_tok + tok]).start()
        return carry

    def wait_tile(s):
        def body(r, carry):
            for k in range(TOP_K):
                row_copy(s, 0, 0).wait()
            return carry
        lax.fori_loop(0, tm, body, 0, unroll=DMA_UNROLL)

    stage_ref[slot] = h_ref[...]
    lax.fori_loop(0, tm, issue, 0, unroll=DMA_UNROLL)

    @pl.when(i > 0)
    def _():
        wait_tile(1 - slot)

    @pl.when(i == n - 1)
    def _():
        wait_tile(slot)
        pad_loop(lambda row: pad_copy(0).wait(), lambda j: trail_copy(0).wait())


def dispatch(h_packed, pos, pad_start, n_pad, n_used, n_rows):
    t = h_packed.shape[0]
    tm = ROW_TILE
    return pl.pallas_call(
        functools.partial(_dispatch_kernel, tm=tm, n_tok=t, n_rows=n_rows),
        out_shape=jax.ShapeDtypeStruct((n_rows, D_MODEL // 2), jnp.uint32),
        grid_spec=pltpu.PrefetchScalarGridSpec(
            num_scalar_prefetch=4, grid=(t // tm,),
            in_specs=[pl.BlockSpec((tm, D_MODEL // 2), lambda i, *_: (i, 0))],
            out_specs=pl.BlockSpec(memory_space=pl.ANY),
            scratch_shapes=[pltpu.VMEM((2, tm, D_MODEL // 2), jnp.uint32),
                            pltpu.VMEM((ZERO_ROWS, D_MODEL // 2), jnp.uint32), pltpu.SemaphoreType.DMA((3,))]),
        compiler_params=_cparams(("arbitrary",)),
        name="dispatch",
    )(pos, pad_start, n_pad, n_used, h_packed)


def _experts_kernel(texp_ref, nused_ref, x_ref, win_ref, wout_ref, o_ref):
    i = pl.program_id(0)
    half = D_MODEL // 2

    @pl.when(i < nused_ref[0])
    def _():
        sub = x_ref.shape[0] // EXPERT_SUBTILES
        for s in range(EXPERT_SUBTILES):
            rs = slice(s * sub, (s + 1) * sub)
            xp = x_ref[rs, :]
            gu = (_dot(_unpack_lo(xp).astype(BF16), win_ref[:half, :]) +
                  _dot(_unpack_hi(xp).astype(BF16), win_ref[half:, :]))
            act = (_silu(gu[:, :D_EXPERT]) * gu[:, D_EXPERT:]).astype(BF16)
            o_ref[rs, :] = _pack_halves(_dot(act, wout_ref[...]))

    @pl.when(i >= nused_ref[0])
    def _():
        o_ref[...] = jnp.zeros_like(o_ref)


def experts(x_sorted, tile_expert, n_used, w_exp_in, w_exp_out, layer):
    tm = EXPERT_TM
    n_tiles = tile_expert.shape[0]
    half = D_MODEL // 2
    return pl.pallas_call(
        _experts_kernel,
        out_shape=jax.ShapeDtypeStruct((n_tiles * tm, half), jnp.uint32),
        grid_spec=pltpu.PrefetchScalarGridSpec(
            num_scalar_prefetch=2, grid=(n_tiles,),
            in_specs=[pl.BlockSpec((tm, half), lambda i, te, nu: (jnp.minimum(i, nu[0] - 1), 0)),
                      pl.BlockSpec((None, None, D_MODEL, 2 * D_EXPERT), lambda i, te, nu: (layer, te[i], 0, 0)),
                      pl.BlockSpec((None, None, D_EXPERT, D_MODEL), lambda i, te, nu: (layer, te[i], 0, 0))],
            out_specs=pl.BlockSpec((tm, half), lambda i, te, nu: (i, 0))),
        compiler_params=_cparams(("arbitrary",)),
        name="experts",
    )(tile_expert, n_used, x_sorted, w_exp_in, w_exp_out)


def _combine_kernel(pos_ref, x_ref, wts_ref, mods_ref, nw_ref, nmods_ref, y_hbm, *rest, tm, n_ctx, n_tok, last):
    if last:
        f_ref, ybuf, sems = rest
    else:
        xo_ref, h_ref, ybuf, sems = rest
    i = pl.program_id(0)
    n = pl.num_programs(0)
    slot = lax.rem(i, 2)
    half = D_MODEL // 2

    def row_copy(src_row, s, k, r):
        return pltpu.make_async_copy(y_hbm.at[pl.ds(src_row, 1), :], ybuf.at[s, k, pl.ds(r, 1), :], sems.at[s])

    def issue(tile, s):
        def body(r, carry):
            for k in range(TOP_K):
                row_copy(pos_ref[k * n_tok + tile * tm + r], s, k, r).start()
            return carry
        lax.fori_loop(0, tm, body, 0, unroll=DMA_UNROLL)

    @pl.when(i == 0)
    def _():
        issue(0, 0)

    @pl.when(i + 1 < n)
    def _():
        issue(i + 1, 1 - slot)

    def wait_body(r, carry):
        for k in range(TOP_K):
            row_copy(0, slot, k, r).wait()
        return carry
    lax.fori_loop(0, tm, wait_body, 0, unroll=DMA_UNROLL)

    sel = jnp.where(i * tm < n_ctx, 0, 1)
    lo, hi = slice(0, half), slice(half, D_MODEL)

    def rows(b, carry):
        rs = pl.ds(pl.multiple_of(b * COMBINE_ROWS, COMBINE_ROWS), COMBINE_ROWS)
        w = wts_ref[rs, :]
        y0 = ybuf[slot, 0, rs, :]
        y1 = ybuf[slot, 1, rs, :]
        gate = mods_ref[sel, 5:6, :]
        halves = []
        for sl, unpack in ((lo, _unpack_lo), (hi, _unpack_hi)):
            moe = w[:, 0:1] * unpack(y0) + w[:, 1:2] * unpack(y1)
            halves.append(x_ref[rs, sl] + gate[:, sl] * moe)
        ssq = sum(jnp.sum(v * v, axis=-1, keepdims=True) for v in halves)
        inv = lax.rsqrt(ssq / D_MODEL + EPS)
        for v, sl in zip(halves, (lo, hi)):
            normed = v * inv * nw_ref[:, sl]
            if last:
                f_ref[rs, sl] = normed
            else:
                xo_ref[rs, sl] = v
                scale = nmods_ref[sel, 1:2, :][:, sl]
                shift = nmods_ref[sel, 0:1, :][:, sl]
                h_ref[rs, sl] = (normed * (1.0 + scale) + shift).astype(h_ref.dtype)
        return carry
    lax.fori_loop(0, tm // COMBINE_ROWS, rows, 0)


def combine(x, y_sorted, pos, wts_t, mods, next_norm_w, next_mods, n_ctx, last):
    t = x.shape[0]
    tm = CHUNK
    full = lambda i, p: (i, 0)
    if last:
        ctx_tiles = n_ctx // tm
        out_shape = jax.ShapeDtypeStruct((t - n_ctx, D_MODEL), F32)
        out_specs = pl.BlockSpec((tm, D_MODEL), lambda i, p: (jnp.maximum(i - ctx_tiles, 0), 0))
    else:
        out_shape = (jax.ShapeDtypeStruct((t, D_MODEL), F32), jax.ShapeDtypeStruct((t, D_MODEL), BF16))
        out_specs = (pl.BlockSpec((tm, D_MODEL), full), pl.BlockSpec((tm, D_MODEL), full))
    return pl.pallas_call(
        functools.partial(_combine_kernel, tm=tm, n_ctx=n_ctx, n_tok=t, last=last),
        out_shape=out_shape,
        grid_spec=pltpu.PrefetchScalarGridSpec(
            num_scalar_prefetch=1, grid=(t // tm,),
            in_specs=[pl.BlockSpec((tm, D_MODEL), full),
                      pl.BlockSpec((tm, TOP_K), full),
                      pl.BlockSpec((2, MOD_ROWS, D_MODEL), lambda i, p: (0, 0, 0)),
                      pl.BlockSpec((1, D_MODEL), lambda i, p: (0, 0)),
                      pl.BlockSpec((2, MOD_ROWS, D_MODEL), lambda i, p: (0, 0, 0)),
                      pl.BlockSpec(memory_space=pl.ANY)],
            out_specs=out_specs,
            scratch_shapes=[pltpu.VMEM((2, TOP_K, tm, D_MODEL // 2), jnp.uint32),
                            pltpu.SemaphoreType.DMA((2,))]),
        compiler_params=_cparams(("arbitrary",)),
        name="combine_last" if last else "combine",
    )(pos, x, wts_t, mods, next_norm_w.reshape(1, D_MODEL), next_mods, y_sorted)


_MAIN_PIECES = ((O_XBC, SSD_XBC), (O_GD, GATE_RANK), (O_Z, SSD_INNER), (O_FX, BRANCH_W), (O_MQ, MLSTM_QKW),
                (O_MV, BRANCH_W), (O_MO, BRANCH_W), (O_SU, BRANCH_W), (O_SV, BRANCH_W))


def _relayout_w_in_kernel(wt_ref, main_ref, gates_ref):
    col = 0
    for off, width in _MAIN_PIECES:
        main_ref[:, col:col + width] = wt_ref[off:off + width, :].T.astype(BF16)
        col += width
    tk = wt_ref.shape[1]
    pad = GATE_LANES - 2 * SSD_HEADS - 4 * MLSTM_HEADS
    narrow = jnp.concatenate([wt_ref[O_DT:O_DT + 2 * SSD_HEADS, :], wt_ref[O_MG:O_MG + 4 * MLSTM_HEADS, :],
                              jnp.zeros((pad, tk), F32)], axis=0)
    gates_ref[...] = narrow.T.astype(BF16)


def relayout_w_in(w_in):
    depth, d, d_in = w_in.shape
    tk = ROW_TILE
    return pl.pallas_call(
        _relayout_w_in_kernel,
        out_shape=(jax.ShapeDtypeStruct((depth, d, P_TOTAL), BF16),
                   jax.ShapeDtypeStruct((depth, d, GATE_LANES), BF16)),
        grid=(depth, d // tk),
        in_specs=[pl.BlockSpec((None, d_in, tk), lambda l, i: (l, 0, i))],
        out_specs=(pl.BlockSpec((None, tk, P_TOTAL), lambda l, i: (l, i, 0)),
                   pl.BlockSpec((None, tk, GATE_LANES), lambda l, i: (l, i, 0))),
        compiler_params=_cparams(("arbitrary", "arbitrary")),
        name="relayout_w_in",
    )(jnp.swapaxes(w_in, 1, 2))


def _prep_layer_params(l, p):
    pad = GATE_LANES - 2 * SSD_HEADS - 4 * MLSTM_HEADS
    gbias = jnp.concatenate([p['ssd_dt_bias'][l].reshape(-1), p['mlstm_gate_b'][l].reshape(-1),
                             jnp.zeros((pad,), F32)])
    alog = jnp.concatenate([p['ssd_a_log'][l].reshape(-1), jnp.zeros((GATE_LANES - 2 * SSD_HEADS,), F32)])
    feat = np.arange(GATE_LANES)[:, None]
    head = np.arange(SSD_INNER)[None, :] // SSD_HEAD_DIM
    pad_k = lambda w: jnp.concatenate([w, jnp.zeros((8 - CONV_K, w.shape[1]), F32)], axis=0)
    return {
        'gbias_row': gbias.reshape(1, -1), 'gbias_col': gbias.reshape(-1, 1),
        'alog_row': alog.reshape(1, -1), 'alog_col': alog.reshape(-1, 1),
        'expand_f': jnp.asarray(feat == head, BF16), 'expand_b': jnp.asarray(feat == head + SSD_HEADS, BF16),
        'ssd_conv_w': pad_k(p['ssd_conv_w'][l]), 'ssd_conv_b': p['ssd_conv_b'][l].reshape(1, -1),
        'ssd_d_e': jnp.repeat(p['ssd_d'][l], SSD_HEAD_DIM).reshape(1, -1),
        'ssd_norm_w': p['ssd_norm_w'][l].reshape(1, -1),
        'mlstm_conv_w': pad_k(p['mlstm_conv_w'][l]), 'mlstm_conv_b': p['mlstm_conv_b'][l].reshape(1, -1),
        'mlstm_norm_w': p['mlstm_norm_w'][l].reshape(1, -1),
        'sgu_norm_w': p['sgu_norm_w'][l].reshape(1, -1),
        'sgu_w': p['sgu_w'][l].astype(BF16),
        'sgu_b_t': jnp.concatenate([p['sgu_b'][l].T, jnp.zeros((CHUNK, GATE_LANES - SGU_GROUPS), F32)], axis=1),
    }


def kernel(x, c, ctx, c_ctx, ada_down, ada_up, ada_b, norm1_w, norm2_w, w_in, ssd_conv_w, ssd_conv_b,
           ssd_dt_bias, ssd_a_log, ssd_d, ssd_norm_w, mlstm_conv_w, mlstm_conv_b, mlstm_gate_b, mlstm_norm_w,
           sgu_norm_w, sgu_w, sgu_b, w_gate_up, w_branch, w_out, w_router, router_bias, w_exp_in, w_exp_out,
           final_norm_w):
    assert x.shape[0] == 1 and ctx.shape[0] == 1, "single-sequence kernel"
    w_main, w_gates = relayout_w_in(w_in)
    p = dict(w_main=w_main, w_gates=w_gates, ssd_conv_w=ssd_conv_w, ssd_conv_b=ssd_conv_b,
             ssd_dt_bias=ssd_dt_bias, ssd_a_log=ssd_a_log, ssd_d=ssd_d, ssd_norm_w=ssd_norm_w,
             mlstm_conv_w=mlstm_conv_w, mlstm_conv_b=mlstm_conv_b, mlstm_gate_b=mlstm_gate_b,
             mlstm_norm_w=mlstm_norm_w, sgu_norm_w=sgu_norm_w, sgu_w=sgu_w, sgu_b=sgu_b,
             w_gate_up=w_gate_up.astype(BF16), w_branch=w_branch.astype(BF16), w_out=w_out.astype(BF16),
             w_exp_in=w_exp_in.astype(BF16), w_exp_out=w_exp_out.astype(BF16))
    depth = w_in.shape[0]
    seq = x.shape[1]
    n_ctx = ctx.shape[1]
    t = n_ctx + seq
    assert n_ctx % ROW_TILE == 0 and seq % ROW_TILE == 0
    n_tiles = -(-(TOP_K * t) // EXPERT_TM) + N_EXPERTS

    xs = jnp.concatenate([ctx[0], x[0]], axis=0)
    cvec = jnp.concatenate([c_ctx[None], c, jnp.zeros((6, D_MODEL), F32)], axis=0)
    mods_all = ada_mods(cvec, ada_down, ada_up, ada_b)
    mods_all = mods_all[:, :2].reshape(depth, 2, N_MOD, D_MODEL)
    mods_all = jnp.concatenate([mods_all, jnp.zeros((depth, 2, MOD_ROWS - N_MOD, D_MODEL), F32)], axis=2)
    w_router_t = w_router.T

    h = norm_mod(xs, norm1_w[0], mods_all[0], n_ctx, 0, 1, BF16)
    for l in range(depth):
        lw = _prep_layer_params(l, p)
        mods = mods_all[l]
        last = l == depth - 1
        proj = matmul(h, p['w_main'], l, BF16, MM_TN)
        gates = matmul(h, p['w_gates'], l, F32, GATE_LANES)

        ys_b, xbc_conv, hm_b, qk_conv = scan_pair(proj, gates, lw, n_ctx)
        y_ssd, y_ml = scan_pair(proj, gates, lw, n_ctx, (xbc_conv, qk_conv), (ys_b, hm_b))
        fx = proj[:, P_FX:P_FX + BRANCH_W]
        y_four = jnp.concatenate([fourier_mix(fx[:n_ctx]), fourier_mix(fx[n_ctx:])], axis=0)
        y_sgu = sgu(proj, lw)

        merged = merge(proj, (y_ssd, y_four, y_ml, y_sgu), p['w_gate_up'], p['w_branch'], l)
        xs = outproj_residual(merged, p['w_out'], l, xs, mods, n_ctx, 2)

        h_packed, idx, wts, rank, counts = norm_router(xs, norm2_w[l], mods, w_router_t, router_bias, n_ctx)
        pos, tile_expert, n_used, pad_start, n_pad = _routing_metadata(idx, rank, counts, n_tiles)
        x_sorted = dispatch(h_packed, pos, pad_start, n_pad, n_used, n_tiles * EXPERT_TM)
        y_sorted = experts(x_sorted, tile_expert, n_used, p['w_exp_in'], p['w_exp_out'], l)
        if last:
            out = combine(xs, y_sorted, pos, wts.T, mods, final_norm_w, mods, n_ctx, True)
        else:
            xs, h = combine(xs, y_sorted, pos, wts.T, mods, norm1_w[l + 1], mods_all[l + 1], n_ctx, False)
    return out[None]
```

```python
import functools
import math

import numpy as np
import jax
import jax.numpy as jnp
from jax import lax
from jax.experimental import pallas as pl
from jax.experimental.pallas import tpu as pltpu

F32 = jnp.float32
BF16 = jnp.bfloat16
HIGHEST = lax.Precision.HIGHEST

D_MODEL = 4096
CHUNK = 128
EPS = 1e-6
ADA_RANK = 256
N_MOD = 6
MOD_ROWS = 8
N_BRANCH = 4
BRANCH_W = 1024
GATE_RANK = 512
SSD_HEADS = 16
SSD_HEAD_DIM = 64
SSD_INNER = SSD_HEADS * SSD_HEAD_DIM
SSD_GROUPS = 4
SSD_HPG = SSD_HEADS // SSD_GROUPS
SSD_STATE = 64
SSD_XBC = SSD_INNER + 2 * SSD_GROUPS * SSD_STATE
CONV_K = 5
CONV_PAD = CONV_K // 2
FOURIER_GROUPS = 4
FOURIER_GW = BRANCH_W // FOURIER_GROUPS
MLSTM_HEADS = 8
MLSTM_QK = 64
MLSTM_V = 128
MLSTM_QKW = 2 * MLSTM_HEADS * MLSTM_QK
SGU_GROUPS = 4
SGU_GW = BRANCH_W // SGU_GROUPS
N_EXPERTS = 16
N_EXPERT_GROUPS = 4
EXPERTS_PER_GROUP = N_EXPERTS // N_EXPERT_GROUPS
TOP_K = 2
D_EXPERT = 640

IN_SPLITS = (SSD_INNER, SSD_XBC, 2 * SSD_HEADS, BRANCH_W, MLSTM_HEADS * MLSTM_QK, MLSTM_HEADS * MLSTM_QK,
             MLSTM_HEADS * MLSTM_V, MLSTM_HEADS * MLSTM_V, 4 * MLSTM_HEADS, BRANCH_W, BRANCH_W, GATE_RANK)
_OFF = [0] + [int(v) for v in np.cumsum(IN_SPLITS)]
(O_Z, O_XBC, O_DT, O_FX, O_MQ, O_MK, O_MV, O_MO, O_MG, O_SU, O_SV, O_GD) = _OFF[:-1]

P_XBC = 0
P_GD = SSD_XBC
P_Z = P_GD + GATE_RANK
P_FX = P_Z + BRANCH_W
P_QK = P_FX + BRANCH_W
P_MV = P_QK + MLSTM_QKW
P_MO = P_MV + BRANCH_W
P_SU = P_MO + BRANCH_W
P_SV = P_SU + BRANCH_W
P_TOTAL = P_SV + BRANCH_W
GATE_LANES = 128
G_DT = 0
G_MG = 2 * SSD_HEADS

HALO = 16
VMEM_LIMIT = 56 * 1024 * 1024

ROW_TILE = 256
MM_TM = 1280
MM_TN = 512
EXPERT_TM = 512


def _cparams(sem):
    return pltpu.CompilerParams(dimension_semantics=sem, vmem_limit_bytes=VMEM_LIMIT)


def _pick_tile(n, pref):
    t = min(pref, n)
    while n % t:
        t -= ROW_TILE if t > ROW_TILE else 8
    return t


def _sigmoid(x):
    return 1.0 / (1.0 + jnp.exp(-x))


def _silu(x):
    return x * _sigmoid(x)


def _softplus(x):
    return jnp.maximum(x, 0.0) + jnp.log(1.0 + jnp.exp(-jnp.abs(x)))


def _log_sigmoid(x):
    return -_softplus(-x)


def _gelu_tanh(x):
    return 0.5 * x * (1.0 + jnp.tanh(math.sqrt(2.0 / math.pi) * (x + 0.044715 * (x * x * x))))


def _dot(a, b):
    return jnp.dot(a, b, preferred_element_type=F32)


def _dot_nt(a, b):
    return lax.dot_general(a, b, (((1,), (1,)), ((), ())), preferred_element_type=F32)


def _dot_hi(a, b):
    return jnp.dot(a, b, preferred_element_type=F32, precision=HIGHEST)


def _split_bf16(x, parts=3):
    out, rest = [], x
    for _ in range(parts):
        piece = rest.astype(BF16)
        out.append(piece)
        rest = rest - piece.astype(F32)
    return out


def _dot_f32_by_exact(a, b_exact):
    terms = [_dot(piece, b_exact) for piece in _split_bf16(a)]
    return terms[0] + terms[1] + terms[2]


def _dot_exact_by_f32(a_exact, b):
    terms = [_dot(a_exact, piece) for piece in _split_bf16(b)]
    return terms[0] + terms[1] + terms[2]


HI_MASK = 0xFFFF0000


def _pack_halves(v):
    bits = lax.bitcast_convert_type(v.astype(BF16).astype(F32), jnp.uint32)
    n = v.shape[1] // 2
    return (bits[:, :n] >> 16) | (bits[:, n:] & jnp.uint32(HI_MASK))


def _unpack_lo(p):
    return lax.bitcast_convert_type(p << 16, F32)


def _unpack_hi(p):
    return lax.bitcast_convert_type(p & jnp.uint32(HI_MASK), F32)


def _row_is_ctx(i, tm, n_ctx):
    row = i * tm + lax.broadcasted_iota(jnp.int32, (tm, 1), 0)
    return row < n_ctx


def _mod_row(mods_ref, is_ctx, idx):
    return jnp.where(is_ctx, mods_ref[0, idx:idx + 1, :], mods_ref[1, idx:idx + 1, :])


def _ada_kernel(c_ref, down_ref, up_ref, b_ref, o_ref, low_ref):
    @pl.when(pl.program_id(1) == 0)
    def _():
        low_ref[...] = _dot_hi(_silu(c_ref[...]), down_ref[...])

    o_ref[...] = _dot_hi(low_ref[...], up_ref[...]) + b_ref[...]


def ada_mods(cvec, ada_down, ada_up, ada_b):
    depth = ada_down.shape[0]
    n_out = ada_up.shape[2]
    tn = 2048
    return pl.pallas_call(
        _ada_kernel,
        out_shape=jax.ShapeDtypeStruct((depth, 8, n_out), F32),
        grid=(depth, n_out // tn),
        in_specs=[pl.BlockSpec((8, D_MODEL), lambda l, j: (0, 0)),
                  pl.BlockSpec((None, D_MODEL, ADA_RANK), lambda l, j: (l, 0, 0)),
                  pl.BlockSpec((None, ADA_RANK, tn), lambda l, j: (l, 0, j)),
                  pl.BlockSpec((None, 1, tn), lambda l, j: (l, 0, j))],
        out_specs=pl.BlockSpec((None, 8, tn), lambda l, j: (l, 0, j)),
        scratch_shapes=[pltpu.VMEM((8, ADA_RANK), F32)],
        compiler_params=_cparams(("arbitrary", "arbitrary")),
        name="ada_mods",
    )(cvec, ada_down, ada_up, ada_b.reshape(depth, 1, n_out))


def _norm_kernel(x_ref, w_ref, mods_ref, o_ref, *, tm, n_ctx, shift_idx, scale_idx, row_off):
    x = x_ref[...]
    y = x * lax.rsqrt(jnp.mean(x * x, axis=-1, keepdims=True) + EPS) * w_ref[...]
    if shift_idx is not None:
        is_ctx = _row_is_ctx(pl.program_id(0) + row_off, tm, n_ctx)
        y = y * (1.0 + _mod_row(mods_ref, is_ctx, scale_idx)) + _mod_row(mods_ref, is_ctx, shift_idx)
    o_ref[...] = y.astype(o_ref.dtype)


def norm_mod(x, w, mods, n_ctx, shift_idx, scale_idx, out_dtype, row_start=0):
    t = x.shape[0] - row_start
    tm = ROW_TILE
    row_off = row_start // tm
    return pl.pallas_call(
        functools.partial(_norm_kernel, tm=tm, n_ctx=n_ctx, shift_idx=shift_idx, scale_idx=scale_idx,
                          row_off=row_off),
        out_shape=jax.ShapeDtypeStruct((t, D_MODEL), out_dtype),
        grid=(t // tm,),
        in_specs=[pl.BlockSpec((tm, D_MODEL), lambda i: (i + row_off, 0)),
                  pl.BlockSpec((1, D_MODEL), lambda i: (0, 0)),
                  pl.BlockSpec((2, MOD_ROWS, D_MODEL), lambda i: (0, 0, 0))],
        out_specs=pl.BlockSpec((tm, D_MODEL), lambda i: (i, 0)),
        compiler_params=_cparams(("arbitrary",)),
        name="norm_mod",
    )(x, w.reshape(1, D_MODEL), mods)


def _mm_kernel(a_ref, b_ref, o_ref):
    o_ref[...] = _dot(a_ref[...], b_ref[...]).astype(o_ref.dtype)


def matmul(a, b_stack, layer, out_dtype, tn):
    m, k = a.shape
    n = b_stack.shape[2]
    tm = _pick_tile(m, MM_TM)
    return pl.pallas_call(
        _mm_kernel,
        out_shape=jax.ShapeDtypeStruct((m, n), out_dtype),
        grid=(m // tm, n // tn),
        in_specs=[pl.BlockSpec((tm, k), lambda i, j: (i, 0)),
                  pl.BlockSpec((None, k, tn), lambda i, j: (layer, 0, j))],
        out_specs=pl.BlockSpec((tm, tn), lambda i, j: (i, j)),
        compiler_params=_cparams(("arbitrary", "arbitrary")),
        name="matmul",
    )(a, b_stack)


def _conv_silu(cur_ref, prev_ref, next_ref, cw_ref, cb_ref, ext_ref, first, last):
    prev = jnp.where(first, 0.0, prev_ref[...].astype(F32))
    nxt = jnp.where(last, 0.0, next_ref[...].astype(F32))
    ext_ref[0:HALO, :] = prev
    ext_ref[HALO:HALO + CHUNK, :] = cur_ref[...].astype(F32)
    ext_ref[HALO + CHUNK:2 * HALO + CHUNK, :] = nxt
    acc = cb_ref[...] + cw_ref[0:1, :] * ext_ref[pl.ds(HALO - CONV_PAD, CHUNK), :]
    for k in range(1, CONV_K):
        acc = acc + cw_ref[k:k + 1, :] * ext_ref[pl.ds(HALO - CONV_PAD + k, CHUNK), :]
    return _silu(acc)


def _tri_masks():
    r = lax.broadcasted_iota(jnp.int32, (CHUNK, CHUNK), 0)
    c = lax.broadcasted_iota(jnp.int32, (CHUNK, CHUNK), 1)
    return c <= r, c >= r


def _cumsums(col, row, reverse):
    lower, upper = _tri_masks()
    lo = jnp.where(lower, 1.0, 0.0).astype(BF16)
    up = jnp.where(upper, 1.0, 0.0).astype(BF16)
    if reverse:
        return _dot_exact_by_f32(up, col), _dot_f32_by_exact(row, lo)
    return _dot_exact_by_f32(lo, col), _dot_f32_by_exact(row, up)


def _chunk_order(n_ctx_chunks, n_chunks, reverse):
    if not reverse:
        return jnp.arange(n_chunks, dtype=jnp.int32)
    return jnp.concatenate([jnp.arange(n_ctx_chunks - 1, -1, -1, dtype=jnp.int32),
                            jnp.arange(n_chunks - 1, n_ctx_chunks - 1, -1, dtype=jnp.int32)])


def _halo_specs(width, col_block, n_rows):
    per = CHUNK // HALO
    last_halo = n_rows // HALO - 1
    cur = pl.BlockSpec((CHUNK, width), lambda c, o: (o[c], col_block))
    prev = pl.BlockSpec((HALO, width), lambda c, o: (jnp.maximum(o[c] * per - 1, 0), col_block))
    nxt = pl.BlockSpec((HALO, width), lambda c, o: (jnp.minimum((o[c] + 1) * per, last_halo), col_block))
    return [cur, prev, nxt]


def _ssd_kernel(order_ref, *refs, reverse, final, n_ctx_chunks, n_chunks, init=True):
    if final:
        (xc_ref, gates_ref, gbias_row_ref, alog_row_ref, alog_col_ref, gbias_col_ref, expand_ref,
         yb_ref, z_ref, dskip_ref, normw_ref, o_ref, h_ref) = refs
    else:
        (xbc_ref, xprev_ref, xnext_ref, cw_ref, cb_ref, gates_ref, gbias_row_ref, alog_row_ref, alog_col_ref,
         gbias_col_ref, expand_ref, o_ref, xc_ref, ext_ref, h_ref) = refs
    c = pl.program_id(0)
    ci = order_ref[c]

    if init:
        @pl.when(c == 0)
        def _():
            h_ref[...] = jnp.zeros_like(h_ref)

    if final:
        xbc = xc_ref[...].astype(F32)
    else:
        first = jnp.logical_or(ci == 0, ci == n_ctx_chunks)
        last = jnp.logical_or(ci == n_ctx_chunks - 1, ci == n_chunks - 1)
        xbc = _conv_silu(xbc_ref, xprev_ref, xnext_ref, cw_ref, cb_ref, ext_ref, first, last)
        xc_ref[...] = xbc.astype(xc_ref.dtype)
    xs = xbc[:, :SSD_INNER]
    bm = xbc[:, SSD_INNER:SSD_INNER + SSD_GROUPS * SSD_STATE]
    cm = xbc[:, SSD_INNER + SSD_GROUPS * SSD_STATE:]

    lane0 = SSD_HEADS if reverse else 0
    g = gates_ref[...]
    dt_col = _softplus(g + gbias_row_ref[...])
    dt_row = _softplus(g.T + gbias_col_ref[...])
    acum_col, acum_row = _cumsums(dt_col * -jnp.exp(alog_row_ref[...]),
                                  dt_row * -jnp.exp(alog_col_ref[...]), reverse)
    both_e = _dot_f32_by_exact(jnp.concatenate([dt_col, acum_col], axis=0), expand_ref[...])
    dt_e = both_e[:CHUNK]
    acum_e = both_e[CHUNK:]
    alast_e = acum_e[0:1, :] if reverse else acum_e[CHUNK - 1:CHUNK, :]
    xdt = xs * dt_e
    xdt_b = xdt.astype(BF16)
    xw_b = (xdt * jnp.exp(alast_e - acum_e)).astype(BF16)
    dec_e = jnp.exp(alast_e)
    lower, upper = _tri_masks()
    mask = upper if reverse else lower
    cm_b = cm.astype(BF16)
    bm_b = bm.astype(BF16)
    bm_t = bm.T.astype(BF16)
    lane = lax.broadcasted_iota(jnp.int32, (CHUNK, 2 * SSD_HEAD_DIM), 1)
    gw = SSD_HPG * SSD_HEAD_DIM
    ys, inters = [], []
    for grp in range(SSD_GROUPS):
        c_g = cm_b[:, grp * SSD_STATE:(grp + 1) * SSD_STATE]
        b_g = bm_b[:, grp * SSD_STATE:(grp + 1) * SSD_STATE]
        cb = _dot_nt(c_g, b_g)
        h_t = h_ref[grp]
        inters.append(_dot(c_g, h_t.astype(BF16)))
        for pair in range(SSD_HPG // 2):
            ps = []
            for k in range(2):
                f = lane0 + grp * SSD_HPG + pair * 2 + k
                seg = acum_col[:, f:f + 1] - acum_row[f:f + 1, :]
                ps.append((cb * jnp.exp(jnp.where(mask, seg, -jnp.inf))).astype(BF16))
            col0 = (grp * (SSD_HPG // 2) + pair) * 2 * SSD_HEAD_DIM
            xp = xdt_b[:, col0:col0 + 2 * SSD_HEAD_DIM]
            rhs = jnp.concatenate([jnp.where(lane < SSD_HEAD_DIM, xp, jnp.zeros_like(xp)),
                                   jnp.where(lane >= SSD_HEAD_DIM, xp, jnp.zeros_like(xp))], axis=0)
            ys.append(_dot(jnp.concatenate(ps, axis=1), rhs))
        new = _dot(bm_t[grp * SSD_STATE:(grp + 1) * SSD_STATE, :], xw_b[:, grp * gw:(grp + 1) * gw])
        h_ref[grp] = dec_e[:, grp * gw:(grp + 1) * gw] * h_t + new
    y = jnp.concatenate(ys, axis=1) + jnp.exp(acum_e) * jnp.concatenate(inters, axis=1)
    if final:
        y = y + yb_ref[...].astype(F32) + dskip_ref[...] * xs
        z = z_ref[...].astype(F32)
        y = y * _silu(z)
        y = y * lax.rsqrt(jnp.mean(y * y, axis=-1, keepdims=True) + EPS) * normw_ref[...]
    o_ref[...] = y.astype(o_ref.dtype)


def ssd_pass(proj, gates, lw, n_ctx, xbc_conv=None, y_other=None):
    t = proj.shape[0]
    n_chunks = t // CHUNK
    n_ctx_chunks = n_ctx // CHUNK
    final = xbc_conv is not None
    reverse = not final
    order = _chunk_order(n_ctx_chunks, n_chunks, reverse)
    const = lambda c, o: (0, 0)
    chunk = lambda c, o: (o[c], 0)
    gate_specs = [pl.BlockSpec((CHUNK, GATE_LANES), chunk),
                  pl.BlockSpec((1, GATE_LANES), const),
                  pl.BlockSpec((1, GATE_LANES), const),
                  pl.BlockSpec((GATE_LANES, 1), const),
                  pl.BlockSpec((GATE_LANES, 1), const),
                  pl.BlockSpec((GATE_LANES, SSD_INNER), const)]
    gate_args = [gates, lw['gbias_row'], lw['alog_row'], lw['alog_col'], lw['gbias_col'],
                 lw['expand_b' if reverse else 'expand_f']]
    y_shape = jax.ShapeDtypeStruct((t, SSD_INNER), BF16)
    y_spec = pl.BlockSpec((CHUNK, SSD_INNER), chunk)
    state = pltpu.VMEM((SSD_GROUPS, SSD_STATE, SSD_HPG * SSD_HEAD_DIM), F32)
    if final:
        in_specs = [pl.BlockSpec((CHUNK, SSD_XBC), chunk)] + gate_specs + [
            pl.BlockSpec((CHUNK, SSD_INNER), chunk),
            pl.BlockSpec((CHUNK, SSD_INNER), lambda c, o: (o[c], P_Z // SSD_INNER)),
            pl.BlockSpec((1, SSD_INNER), const),
            pl.BlockSpec((1, SSD_INNER), const)]
        args = [xbc_conv] + gate_args + [y_other, proj, lw['ssd_d_e'], lw['ssd_norm_w']]
        out_shape, out_specs, scratch = [y_shape], [y_spec], [state]
    else:
        in_specs = _halo_specs(SSD_XBC, P_XBC // SSD_XBC, t) + [
            pl.BlockSpec((8, SSD_XBC), const), pl.BlockSpec((1, SSD_XBC), const)] + gate_specs
        args = [proj, proj, proj, lw['ssd_conv_w'], lw['ssd_conv_b']] + gate_args
        out_shape = [y_shape, jax.ShapeDtypeStruct((t, SSD_XBC), BF16)]
        out_specs = [y_spec, pl.BlockSpec((CHUNK, SSD_XBC), chunk)]
        scratch = [pltpu.VMEM((CHUNK + 2 * HALO, SSD_XBC), F32), state]
    return dict(in_specs=in_specs, args=args, out_shape=out_shape, out_specs=out_specs, scratch=scratch,
                n_state=1)


def _mlstm_kernel(order_ref, *refs, reverse, final, n_ctx_chunks, n_chunks, init=True):
    if final:
        (qc_ref, v_ref, gates_ref, gbias_row_ref, gbias_col_ref, hb_ref, mo_ref, normw_ref,
         o_ref, cn_ref, m_ref) = refs
    else:
        (qk_ref, qprev_ref, qnext_ref, cw_ref, cb_ref, v_ref, gates_ref, gbias_row_ref, gbias_col_ref,
         o_ref, qc_ref, ext_ref, cn_ref, m_ref) = refs
    c = pl.program_id(0)
    ci = order_ref[c]

    if init:
        @pl.when(c == 0)
        def _():
            cn_ref[...] = jnp.zeros_like(cn_ref)
            m_ref[...] = jnp.zeros_like(m_ref)

    if final:
        qk = qc_ref[...].astype(F32)
    else:
        first = jnp.logical_or(ci == 0, ci == n_ctx_chunks)
        last = jnp.logical_or(ci == n_ctx_chunks - 1, ci == n_chunks - 1)
        qk = _conv_silu(qk_ref, qprev_ref, qnext_ref, cw_ref, cb_ref, ext_ref, first, last)
        qc_ref[...] = qk.astype(qc_ref.dtype)
    hq = MLSTM_HEADS * MLSTM_QK
    q_b = qk[:, :hq].astype(BF16)
    k = qk[:, hq:] * (MLSTM_QK ** -0.5)
    k_b = k.astype(BF16)
    k_t = k.T
    v_b = v_ref[...]

    icol = G_MG + (2 * MLSTM_HEADS if reverse else 0)
    fcol = icol + MLSTM_HEADS
    g_col = gates_ref[...] + gbias_row_ref[...]
    g_row = gates_ref[...].T + gbias_col_ref[...]
    bcum_col, bcum_row = _cumsums(_log_sigmoid(g_col), _log_sigmoid(g_row), reverse)
    lower, upper = _tri_masks()
    mask = upper if reverse else lower
    ones_col = (lax.broadcasted_iota(jnp.int32, (CHUNK, MLSTM_V), 1) == 0).astype(BF16)

    for h in range(MLSTM_HEADS):
        bc = bcum_col[:, fcol + h:fcol + h + 1]
        br = bcum_row[fcol + h:fcol + h + 1, :]
        ig = g_row[icol + h:icol + h + 1, :]
        dm = jnp.where(mask, bc - br + ig, -jnp.inf)
        m_in = m_ref[h:h + 1, 0:1]
        w_inter = bc + m_in
        m_t = jnp.maximum(w_inter, jnp.max(dm, axis=1, keepdims=True))
        q_h = q_b[:, h * MLSTM_QK:(h + 1) * MLSTM_QK]
        s = _dot_nt(q_h, k_b[:, h * MLSTM_QK:(h + 1) * MLSTM_QK]) * jnp.exp(dm - m_t)
        v_ext = jnp.concatenate([v_b[:, h * MLSTM_V:(h + 1) * MLSTM_V], ones_col], axis=1)
        cn = cn_ref[h]
        tot = _dot(s.astype(BF16), v_ext) + jnp.exp(w_inter - m_t) * _dot(q_h, cn.astype(BF16))
        num = tot[:, :MLSTM_V]
        den = tot[:, MLSTM_V:MLSTM_V + 1]
        hh = num / jnp.maximum(jnp.abs(den), jnp.exp(-m_t))

        b_last = br[:, 0:1] if reverse else br[:, CHUNK - 1:CHUNK]
        w_end = b_last - br + ig
        m_loc = jnp.max(w_end, axis=1, keepdims=True)
        kw_t = (k_t[h * MLSTM_QK:(h + 1) * MLSTM_QK, :] * jnp.exp(w_end - m_loc)).astype(BF16)
        m_new = jnp.maximum(b_last + m_in, m_loc)
        cn_ref[h] = jnp.exp(b_last + m_in - m_new) * cn + jnp.exp(m_loc - m_new) * _dot(kw_t, v_ext)
        m_ref[h:h + 1, :] = jnp.broadcast_to(m_new, (1, GATE_LANES))

        sl = slice(h * MLSTM_V, (h + 1) * MLSTM_V)
        if final:
            hh = hh + hb_ref[:, sl].astype(F32)
            hh = hh * lax.rsqrt(jnp.mean(hh * hh, axis=-1, keepdims=True) + EPS) * normw_ref[:, sl]
            hh = _sigmoid(mo_ref[:, sl].astype(F32)) * hh
        o_ref[:, sl] = hh.astype(o_ref.dtype)


def mlstm_pass(proj, gates, lw, n_ctx, qk_conv=None, h_other=None):
    t = proj.shape[0]
    n_chunks = t // CHUNK
    n_ctx_chunks = n_ctx // CHUNK
    final = qk_conv is not None
    reverse = not final
    order = _chunk_order(n_ctx_chunks, n_chunks, reverse)
    const = lambda c, o: (0, 0)
    chunk = lambda c, o: (o[c], 0)
    common_specs = [pl.BlockSpec((CHUNK, BRANCH_W), lambda c, o: (o[c], P_MV // BRANCH_W)),
                    pl.BlockSpec((CHUNK, GATE_LANES), chunk),
                    pl.BlockSpec((1, GATE_LANES), const),
                    pl.BlockSpec((GATE_LANES, 1), const)]
    common_args = [proj, gates, lw['gbias_row'], lw['gbias_col']]
    h_shape = jax.ShapeDtypeStruct((t, BRANCH_W), BF16)
    h_spec = pl.BlockSpec((CHUNK, BRANCH_W), chunk)
    state = [pltpu.VMEM((MLSTM_HEADS, MLSTM_QK, 2 * MLSTM_V), F32), pltpu.VMEM((MLSTM_HEADS, GATE_LANES), F32)]
    if final:
        in_specs = [pl.BlockSpec((CHUNK, MLSTM_QKW), chunk)] + common_specs + [
            pl.BlockSpec((CHUNK, BRANCH_W), chunk),
            pl.BlockSpec((CHUNK, BRANCH_W), lambda c, o: (o[c], P_MO // BRANCH_W)),
            pl.BlockSpec((1, BRANCH_W), const)]
        args = [qk_conv] + common_args + [h_other, proj, lw['mlstm_norm_w']]
        out_shape, out_specs, scratch = [h_shape], [h_spec], state
    else:
        in_specs = _halo_specs(MLSTM_QKW, P_QK // MLSTM_QKW, t) + [
            pl.BlockSpec((8, MLSTM_QKW), const), pl.BlockSpec((1, MLSTM_QKW), const)] + common_specs
        args = [proj, proj, proj, lw['mlstm_conv_w'], lw['mlstm_conv_b']] + common_args
        out_shape = [h_shape, jax.ShapeDtypeStruct((t, MLSTM_QKW), BF16)]
        out_specs = [h_spec, pl.BlockSpec((CHUNK, MLSTM_QKW), chunk)]
        scratch = [pltpu.VMEM((CHUNK + 2 * HALO, MLSTM_QKW), F32)] + state
    return dict(in_specs=in_specs, args=args, out_shape=out_shape, out_specs=out_specs, scratch=scratch,
                n_state=2)


def _scan_pair_kernel(order_ref, *refs, n_in, n_out, n_scr, n_state, **static):
    groups, at = [], 0
    for counts in (n_in, n_out, n_scr):
        pair = []
        for cnt in counts:
            pair.append(refs[at:at + cnt])
            at += cnt
        groups.append(pair)
    (in_s, in_m), (out_s, out_m), (scr_s, scr_m) = groups

    @pl.when(pl.program_id(0) == 0)
    def _():
        for ref in scr_s[len(scr_s) - n_state[0]:] + scr_m[len(scr_m) - n_state[1]:]:
            ref[...] = jnp.zeros_like(ref)

    _ssd_kernel(order_ref, *in_s, *out_s, *scr_s, init=False, **static)
    _mlstm_kernel(order_ref, *in_m, *out_m, *scr_m, init=False, **static)


def scan_pair(proj, gates, lw, n_ctx, convs=None, others=None):
    t = proj.shape[0]
    n_chunks = t // CHUNK
    n_ctx_chunks = n_ctx // CHUNK
    final = convs is not None
    reverse = not final
    order = _chunk_order(n_ctx_chunks, n_chunks, reverse)
    ssd = ssd_pass(proj, gates, lw, n_ctx, *((convs[0], others[0]) if final else ()))
    mls = mlstm_pass(proj, gates, lw, n_ctx, *((convs[1], others[1]) if final else ()))
    count = lambda key: (len(ssd[key]), len(mls[key]))
    return pl.pallas_call(
        functools.partial(_scan_pair_kernel, n_in=count('in_specs'), n_out=count('out_shape'),
                          n_scr=count('scratch'), n_state=(ssd['n_state'], mls['n_state']),
                          reverse=reverse, final=final, n_ctx_chunks=n_ctx_chunks, n_chunks=n_chunks),
        out_shape=tuple(ssd['out_shape'] + mls['out_shape']),
        grid_spec=pltpu.PrefetchScalarGridSpec(
            num_scalar_prefetch=1, grid=(n_chunks,), in_specs=ssd['in_specs'] + mls['in_specs'],
            out_specs=tuple(ssd['out_specs'] + mls['out_specs']),
            scratch_shapes=ssd['scratch'] + mls['scratch']),
        compiler_params=_cparams(("arbitrary",)),
        name="scan_bwd" if reverse else "scan_fwd",
    )(order, *ssd['args'], *mls['args'])


def _fft_split(length):
    l1 = 1 << (int(math.log2(length)) // 2)
    return l1, length // l1


def _fft1_kernel(x_ref, cs_ref, f1_ref, twc_ref, tws_ref, ure_ref, uim_ref, *, l1):
    x = x_ref[...]
    cs = cs_ref[...]
    a, b = [], []
    for grp in range(FOURIER_GROUPS):
        ab = _dot(x[:, grp * FOURIER_GW:(grp + 1) * FOURIER_GW], cs)
        a.append(ab[:, :FOURIER_GW])
        b.append(ab[:, FOURIER_GW:])
    ab = jnp.concatenate(a + b, axis=1).astype(BF16)
    prod = _dot(f1_ref[...], ab)
    c_a = prod[:l1, :BRANCH_W]
    c_b = prod[:l1, BRANCH_W:]
    s_a = prod[l1:, :BRANCH_W]
    s_b = prod[l1:, BRANCH_W:]
    u_re = c_a - s_b
    u_im = -(c_b + s_a)
    twc = twc_ref[...]
    tws = tws_ref[...]
    ure_ref[...] = (u_re * twc + u_im * tws).astype(ure_ref.dtype)
    uim_ref[...] = (u_im * twc - u_re * tws).astype(uim_ref.dtype)


def _fft2_kernel(ure_ref, uim_ref, f2_ref, o_ref):
    u = jnp.concatenate([ure_ref[...], uim_ref[...]], axis=0)
    o_ref[...] = _dot(f2_ref[...], u).astype(o_ref.dtype)


def _fft_tables(length):
    l1, l2 = _fft_split(length)
    scale = 1.0 / math.sqrt(length * FOURIER_GW)
    kc = np.arange(FOURIER_GW)
    ang_c = 2.0 * np.pi * np.outer(kc, kc) / FOURIER_GW
    cs = np.concatenate([np.cos(ang_c), np.sin(ang_c)], axis=1) * scale
    k1 = np.arange(l1)
    ang1 = 2.0 * np.pi * np.outer(k1, k1) / l1
    f1 = np.concatenate([np.cos(ang1), np.sin(ang1)], axis=0)
    n2 = np.arange(l2)
    ang_t = 2.0 * np.pi * np.outer(n2, k1) / length
    ang2 = 2.0 * np.pi * np.outer(n2, n2) / l2
    f2 = np.concatenate([np.cos(ang2), np.sin(ang2)], axis=1)
    return (jnp.asarray(cs, BF16), jnp.asarray(f1, BF16), jnp.asarray(np.cos(ang_t)[:, :, None], F32),
            jnp.asarray(np.sin(ang_t)[:, :, None], F32), jnp.asarray(f2, BF16))


def fourier_mix(fx):
    length = fx.shape[0]
    l1, l2 = _fft_split(length)
    cs, f1, twc, tws, f2 = _fft_tables(length)
    u_shape = jax.ShapeDtypeStruct((l1, l2 * BRANCH_W), BF16)
    ure, uim = pl.pallas_call(
        functools.partial(_fft1_kernel, l1=l1),
        out_shape=(u_shape, u_shape),
        grid=(l2,),
        in_specs=[pl.BlockSpec((l1, BRANCH_W), lambda j: (0, j)),
                  pl.BlockSpec((FOURIER_GW, 2 * FOURIER_GW), lambda j: (0, 0)),
                  pl.BlockSpec((2 * l1, l1), lambda j: (0, 0)),
                  pl.BlockSpec((None, l1, 1), lambda j: (j, 0, 0)),
                  pl.BlockSpec((None, l1, 1), lambda j: (j, 0, 0))],
        out_specs=(pl.BlockSpec((l1, BRANCH_W), lambda j: (0, j)),
                   pl.BlockSpec((l1, BRANCH_W), lambda j: (0, j))),
        compiler_params=_cparams(("arbitrary",)),
        name="fft_stage1",
    )(fx.reshape(l1, l2 * BRANCH_W), cs, f1, twc, tws)
    out = pl.pallas_call(
        _fft2_kernel,
        out_shape=jax.ShapeDtypeStruct((l2, l1 * BRANCH_W), BF16),
        grid=(l1,),
        in_specs=[pl.BlockSpec((None, l2, BRANCH_W), lambda i: (i, 0, 0)),
                  pl.BlockSpec((None, l2, BRANCH_W), lambda i: (i, 0, 0)),
                  pl.BlockSpec((l2, 2 * l2), lambda i: (0, 0))],
        out_specs=pl.BlockSpec((l2, BRANCH_W), lambda i: (0, i)),
        compiler_params=_cparams(("arbitrary",)),
        name="fft_stage2",
    )(ure.reshape(l1, l2, BRANCH_W), uim.reshape(l1, l2, BRANCH_W), f2)
    return out.reshape(length, BRANCH_W)


def _sgu_kernel(su_ref, sv_ref, normw_ref, w_ref, b_ref, o_ref):
    v = _gelu_tanh(sv_ref[...].astype(F32))
    v = (v * lax.rsqrt(jnp.mean(v * v, axis=-1, keepdims=True) + EPS) * normw_ref[...]).astype(BF16)
    mix = []
    for grp in range(SGU_GROUPS):
        mix.append(_dot(w_ref[grp], v[:, grp * SGU_GW:(grp + 1) * SGU_GW]) + b_ref[:, grp:grp + 1])
    o_ref[...] = (_gelu_tanh(su_ref[...].astype(F32)) * jnp.concatenate(mix, axis=1)).astype(o_ref.dtype)


def sgu(proj, lw):
    t = proj.shape[0]
    return pl.pallas_call(
        _sgu_kernel,
        out_shape=jax.ShapeDtypeStruct((t, BRANCH_W), BF16),
        grid=(t // CHUNK,),
        in_specs=[pl.BlockSpec((CHUNK, BRANCH_W), lambda c: (c, P_SU // BRANCH_W)),
                  pl.BlockSpec((CHUNK, BRANCH_W), lambda c: (c, P_SV // BRANCH_W)),
                  pl.BlockSpec((1, BRANCH_W), lambda c: (0, 0)),
                  pl.BlockSpec((SGU_GROUPS, CHUNK, CHUNK), lambda c: (0, 0, 0)),
                  pl.BlockSpec((CHUNK, GATE_LANES), lambda c: (0, 0))],
        out_specs=pl.BlockSpec((CHUNK, BRANCH_W), lambda c: (c, 0)),
        compiler_params=_cparams(("arbitrary",)),
        name="sgu",
    )(proj, proj, lw['sgu_norm_w'], lw['sgu_w'], lw['sgu_b_t'])


def _merge_kernel(gd_ref, b0_ref, b1_ref, b2_ref, b3_ref, wg_ref, wb_ref, o_ref):
    gd = gd_ref[...]
    acc = None
    for n, b_ref in enumerate((b0_ref, b1_ref, b2_ref, b3_ref)):
        term = _sigmoid(_dot(gd, wg_ref[n])) * _dot(b_ref[...], wb_ref[n])
        acc = term if acc is None else acc + term
    o_ref[...] = acc.astype(o_ref.dtype)


def merge(proj, branches, w_gate_up, w_branch, layer):
    t = proj.shape[0]
    tm = _pick_tile(t, MM_TM)
    tn = MM_TN
    row = lambda i, j: (i, 0)
    return pl.pallas_call(
        _merge_kernel,
        out_shape=jax.ShapeDtypeStruct((t, D_MODEL), BF16),
        grid=(t // tm, D_MODEL // tn),
        in_specs=[pl.BlockSpec((tm, GATE_RANK), lambda i, j: (i, P_GD // GATE_RANK))] +
                 [pl.BlockSpec((tm, BRANCH_W), row)] * N_BRANCH +
                 [pl.BlockSpec((None, N_BRANCH, GATE_RANK, tn), lambda i, j: (layer, 0, 0, j)),
                  pl.BlockSpec((None, N_BRANCH, BRANCH_W, tn), lambda i, j: (layer, 0, 0, j))],
        out_specs=pl.BlockSpec((tm, tn), lambda i, j: (i, j)),
        compiler_params=_cparams(("arbitrary", "arbitrary")),
        name="merge",
    )(proj, *branches, w_gate_up, w_branch)


def _outproj_kernel(a_ref, w_ref, x_ref, mods_ref, o_ref, *, tm, n_ctx, gate_idx):
    is_ctx = _row_is_ctx(pl.program_id(0), tm, n_ctx)
    o_ref[...] = x_ref[...] + _mod_row(mods_ref, is_ctx, gate_idx) * _dot(a_ref[...], w_ref[...])


def outproj_residual(a, w_stack, layer, x, mods, n_ctx, gate_idx):
    t = a.shape[0]
    tm = _pick_tile(t, MM_TM)
    tn = MM_TN
    return pl.pallas_call(
        functools.partial(_outproj_kernel, tm=tm, n_ctx=n_ctx, gate_idx=gate_idx),
        out_shape=jax.ShapeDtypeStruct((t, D_MODEL), F32),
        grid=(t // tm, D_MODEL // tn),
        in_specs=[pl.BlockSpec((tm, D_MODEL), lambda i, j: (i, 0)),
                  pl.BlockSpec((None, D_MODEL, tn), lambda i, j: (layer, 0, j)),
                  pl.BlockSpec((tm, tn), lambda i, j: (i, j)),
                  pl.BlockSpec((2, MOD_ROWS, tn), lambda i, j: (0, 0, j))],
        out_specs=pl.BlockSpec((tm, tn), lambda i, j: (i, j)),
        compiler_params=_cparams(("arbitrary", "arbitrary")),
        name="outproj_residual",
    )(a, w_stack, x, mods)


def _router_kernel(x_ref, w_ref, mods_ref, wr_ref, rb_ref, h_ref, idx_ref, wts_ref, rank_ref, cnt_ref, base_ref,
                   *, tm, n_ctx):
    @pl.when(pl.program_id(0) == 0)
    def _():
        base_ref[...] = jnp.zeros_like(base_ref)

    x = x_ref[...]
    is_ctx = _row_is_ctx(pl.program_id(0), tm, n_ctx)
    y = x * lax.rsqrt(jnp.mean(x * x, axis=-1, keepdims=True) + EPS) * w_ref[...]
    h = y * (1.0 + _mod_row(mods_ref, is_ctx, 4)) + _mod_row(mods_ref, is_ctx, 3)
    h_ref[...] = _pack_halves(h)
    logits = lax.dot_general(wr_ref[...], h, (((1,), (1,)), ((), ())), preferred_element_type=F32,
                             precision=HIGHEST)
    aff = _sigmoid(logits)
    sel = aff + rb_ref[...]
    rows = [sel[e:e + 1, :] for e in range(N_EXPERTS)]
    affs = [aff[e:e + 1, :] for e in range(N_EXPERTS)]
    gscore = []
    for grp in range(N_EXPERT_GROUPS):
        r = rows[grp * EXPERTS_PER_GROUP:(grp + 1) * EXPERTS_PER_GROUP]
        best = None
        for a in range(EXPERTS_PER_GROUP):
            for b in range(a + 1, EXPERTS_PER_GROUP):
                pair = r[a] + r[b]
                best = pair if best is None else jnp.maximum(best, pair)
        gscore.append(best)
    g_best = gscore[0]
    g_idx = jnp.zeros_like(g_best, dtype=jnp.int32)
    for grp in range(1, N_EXPERT_GROUPS):
        better = gscore[grp] > g_best
        g_best = jnp.where(better, gscore[grp], g_best)
        g_idx = jnp.where(better, grp, g_idx)
    masked = [jnp.where(g_idx == e // EXPERTS_PER_GROUP, rows[e], -jnp.inf) for e in range(N_EXPERTS)]

    def argbest(exclude):
        best_v = jnp.full_like(g_best, -jnp.inf)
        best_i = jnp.full_like(g_idx, -1)
        best_a = jnp.zeros_like(g_best)
        for e in range(N_EXPERTS):
            ok = masked[e] > best_v
            if exclude is not None:
                ok = jnp.logical_and(ok, exclude != e)
            best_v = jnp.where(ok, masked[e], best_v)
            best_i = jnp.where(ok, e, best_i)
            best_a = jnp.where(ok, affs[e], best_a)
        return best_i, best_a

    i1, a1 = argbest(None)
    i2, a2 = argbest(i1)
    tot = a1 + a2
    idx_ref[...] = jnp.concatenate([i1, i2], axis=0)
    wts_ref[...] = jnp.concatenate([a1 / tot, a2 / tot], axis=0)

    e_iota = lax.broadcasted_iota(jnp.int32, (N_EXPERTS, tm), 0)
    hit1 = e_iota == i1
    hit2 = e_iota == i2
    onehot = jnp.where(jnp.logical_or(hit1, hit2), 1.0, 0.0)
    r = lax.broadcasted_iota(jnp.int32, (tm, tm), 0)
    cidx = lax.broadcasted_iota(jnp.int32, (tm, tm), 1)
    before = jnp.where(r < cidx, 1.0, 0.0).astype(BF16)
    prior = base_ref[:, 0:1] + _dot(onehot.astype(BF16), before)
    rank1 = jnp.sum(jnp.where(hit1, prior, 0.0), axis=0, keepdims=True)
    rank2 = jnp.sum(jnp.where(hit2, prior, 0.0), axis=0, keepdims=True)
    rank_ref[...] = jnp.concatenate([rank1, rank2], axis=0).astype(jnp.int32)
    total = base_ref[...] + jnp.sum(onehot, axis=1, keepdims=True)
    base_ref[...] = total
    cnt_ref[...] = total


def norm_router(x, norm_w, mods, w_router_t, router_bias, n_ctx):
    t = x.shape[0]
    tm = ROW_TILE
    sel = lambda i: (0, i)
    return pl.pallas_call(
        functools.partial(_router_kernel, tm=tm, n_ctx=n_ctx),
        out_shape=(jax.ShapeDtypeStruct((t, D_MODEL // 2), jnp.uint32),
                   jax.ShapeDtypeStruct((TOP_K, t), jnp.int32),
                   jax.ShapeDtypeStruct((TOP_K, t), F32),
                   jax.ShapeDtypeStruct((TOP_K, t), jnp.int32),
                   jax.ShapeDtypeStruct((N_EXPERTS, GATE_LANES), F32)),
        grid=(t // tm,),
        in_specs=[pl.BlockSpec((tm, D_MODEL), lambda i: (i, 0)),
                  pl.BlockSpec((1, D_MODEL), lambda i: (0, 0)),
                  pl.BlockSpec((2, MOD_ROWS, D_MODEL), lambda i: (0, 0, 0)),
                  pl.BlockSpec((N_EXPERTS, D_MODEL), lambda i: (0, 0)),
                  pl.BlockSpec((N_EXPERTS, 1), lambda i: (0, 0))],
        out_specs=(pl.BlockSpec((tm, D_MODEL // 2), lambda i: (i, 0)),
                   pl.BlockSpec((TOP_K, tm), sel), pl.BlockSpec((TOP_K, tm), sel), pl.BlockSpec((TOP_K, tm), sel),
                   pl.BlockSpec((N_EXPERTS, GATE_LANES), lambda i: (0, 0))),
        scratch_shapes=[pltpu.VMEM((N_EXPERTS, GATE_LANES), F32)],
        compiler_params=_cparams(("arbitrary",)),
        name="norm_router",
    )(x, norm_w.reshape(1, D_MODEL), mods, w_router_t, router_bias.reshape(N_EXPERTS, 1))


def _routing_metadata(idx, rank, counts, n_tiles):
    tm = EXPERT_TM
    counts = counts[:, 0].astype(jnp.int32)
    padded = ((counts + tm - 1) // tm) * tm
    ends = jnp.cumsum(padded)
    starts = ends - padded
    experts_iota = jnp.arange(N_EXPERTS, dtype=jnp.int32)
    start_of = jnp.sum(jnp.where(idx[:, :, None] == experts_iota, starts, 0), axis=-1)
    pos = (start_of + rank).reshape(-1).astype(jnp.int32)
    n_used = (ends[-1] // tm).astype(jnp.int32)
    tile_start = jnp.arange(n_tiles, dtype=jnp.int32) * tm
    tile_expert = jnp.sum((tile_start[:, None] >= ends[None, :]).astype(jnp.int32), axis=1)
    last_used = jnp.sum((jnp.maximum(n_used - 1, 0) * tm >= ends).astype(jnp.int32))
    tile_expert = jnp.minimum(jnp.where(tile_start < n_used * tm, tile_expert, last_used), N_EXPERTS - 1)
    return pos, tile_expert.astype(jnp.int32), n_used.reshape(1), (starts + counts).astype(jnp.int32), \
        (padded - counts).astype(jnp.int32)


ZERO_ROWS = 64
DMA_UNROLL = 4
COMBINE_ROWS = 32
EXPERT_SUBTILES = 2


def _dispatch_kernel(pos_ref, padstart_ref, npad_ref, nused_ref, h_ref, win_ref, wout_ref, o_hbm, win_o_ref,
                     wout_o_ref, stage_ref, zero_ref, sems, *, tm, n_tok, n_rows, w_steps):
    i = pl.program_id(0)
    n = pl.num_programs(0)
    slot = lax.rem(i, 2)
    used_rows = nused_ref[0] * EXPERT_TM
    n_trail = (n_rows - used_rows) // ZERO_ROWS

    def row_copy(s, src_row, dst_row):
        return pltpu.make_async_copy(stage_ref.at[s, pl.ds(src_row, 1), :], o_hbm.at[pl.ds(dst_row, 1), :],
                                     sems.at[s])

    def pad_copy(dst_row):
        return pltpu.make_async_copy(zero_ref.at[pl.ds(0, 1), :], o_hbm.at[pl.ds(dst_row, 1), :], sems.at[2])

    def trail_copy(j):
        row = pl.multiple_of(used_rows + j * ZERO_ROWS, ZERO_ROWS)
        return pltpu.make_async_copy(zero_ref, o_hbm.at[pl.ds(row, ZERO_ROWS), :], sems.at[2])

    def pad_loop(fn, trail_fn):
        for e in range(N_EXPERTS):
            def body(r, carry, e=e):
                fn(padstart_ref[e] + r)
                return carry
            lax.fori_loop(0, npad_ref[e], body, 0)

        def trail_body(j, carry):
            trail_fn(j)
            return carry
        lax.fori_loop(0, n_trail, trail_body, 0)

    @pl.when(i == 0)
    def _():
        zero_ref[...] = jnp.zeros_like(zero_ref)
        pad_loop(lambda row: pad_copy(row).start(), lambda j: trail_copy(j).start())

    def issue(r, carry):
        tok = i * tm + r
        for k in range(TOP_K):
            row_copy(slot, r, pos_ref[k * n_tok + tok]).start()
        return carry

    def wait_tile(s):
        def body(r, carry):
            for k in range(TOP_K):
                row_copy(s, 0, 0).wait()
            return carry
        lax.fori_loop(0, tm, body, 0, unroll=DMA_UNROLL)

    @pl.when(i < w_steps)
    def _():
        win_o_ref[...] = win_ref[...].astype(BF16)
        wout_o_ref[...] = wout_ref[...].astype(BF16)

    stage_ref[slot] = h_ref[...]
    lax.fori_loop(0, tm, issue, 0, unroll=DMA_UNROLL)

    @pl.when(i > 0)
    def _():
        wait_tile(1 - slot)

    @pl.when(i == n - 1)
    def _():
        wait_tile(slot)
        pad_loop(lambda row: pad_copy(0).wait(), lambda j: trail_copy(0).wait())


def dispatch(h_packed, pos, pad_start, n_pad, n_used, n_rows, w_exp_in, w_exp_out, layer):
    t = h_packed.shape[0]
    tm = ROW_TILE
    steps = t // tm
    w_steps = 1 << (steps.bit_length() - 1)
    depth = w_exp_in.shape[0]
    win_flat = w_exp_in.reshape(depth, N_EXPERTS * D_MODEL, 2 * D_EXPERT)
    wout_flat = w_exp_out.reshape(depth, N_EXPERTS * D_EXPERT, D_MODEL)
    rows_in = win_flat.shape[1] // w_steps
    rows_out = wout_flat.shape[1] // w_steps
    slab = lambda i, *_: (jnp.minimum(i, w_steps - 1), 0)
    slab_l = lambda i, *_: (layer, jnp.minimum(i, w_steps - 1), 0)
    x_sorted, win_b, wout_b = pl.pallas_call(
        functools.partial(_dispatch_kernel, tm=tm, n_tok=t, n_rows=n_rows, w_steps=w_steps),
        out_shape=(jax.ShapeDtypeStruct((n_rows, D_MODEL // 2), jnp.uint32),
                   jax.ShapeDtypeStruct(win_flat.shape[1:], BF16),
                   jax.ShapeDtypeStruct(wout_flat.shape[1:], BF16)),
        grid_spec=pltpu.PrefetchScalarGridSpec(
            num_scalar_prefetch=4, grid=(steps,),
            in_specs=[pl.BlockSpec((tm, D_MODEL // 2), lambda i, *_: (i, 0)),
                      pl.BlockSpec((None, rows_in, 2 * D_EXPERT), slab_l),
                      pl.BlockSpec((None, rows_out, D_MODEL), slab_l)],
            out_specs=(pl.BlockSpec(memory_space=pl.ANY),
                       pl.BlockSpec((rows_in, 2 * D_EXPERT), slab),
                       pl.BlockSpec((rows_out, D_MODEL), slab)),
            scratch_shapes=[pltpu.VMEM((2, tm, D_MODEL // 2), jnp.uint32),
                            pltpu.VMEM((ZERO_ROWS, D_MODEL // 2), jnp.uint32), pltpu.SemaphoreType.DMA((3,))]),
        compiler_params=_cparams(("arbitrary",)),
        name="dispatch",
    )(pos, pad_start, n_pad, n_used, h_packed, win_flat, wout_flat)
    return (x_sorted, win_b.reshape(1, N_EXPERTS, D_MODEL, 2 * D_EXPERT),
            wout_b.reshape(1, N_EXPERTS, D_EXPERT, D_MODEL))


def _experts_kernel(texp_ref, nused_ref, x_ref, win_ref, wout_ref, o_ref):
    i = pl.program_id(0)
    half = D_MODEL // 2

    @pl.when(i < nused_ref[0])
    def _():
        sub = x_ref.shape[0] // EXPERT_SUBTILES
        for s in range(EXPERT_SUBTILES):
            rs = slice(s * sub, (s + 1) * sub)
            xp = x_ref[rs, :]
            gu = (_dot(_unpack_lo(xp).astype(BF16), win_ref[:half, :]) +
                  _dot(_unpack_hi(xp).astype(BF16), win_ref[half:, :]))
            act = (_silu(gu[:, :D_EXPERT]) * gu[:, D_EXPERT:]).astype(BF16)
            o_ref[rs, :] = _pack_halves(_dot(act, wout_ref[...]))

    @pl.when(i >= nused_ref[0])
    def _():
        o_ref[...] = jnp.zeros_like(o_ref)


def experts(x_sorted, tile_expert, n_used, w_exp_in, w_exp_out, layer):
    tm = EXPERT_TM
    n_tiles = tile_expert.shape[0]
    half = D_MODEL // 2
    return pl.pallas_call(
        _experts_kernel,
        out_shape=jax.ShapeDtypeStruct((n_tiles * tm, half), jnp.uint32),
        grid_spec=pltpu.PrefetchScalarGridSpec(
            num_scalar_prefetch=2, grid=(n_tiles,),
            in_specs=[pl.BlockSpec((tm, half), lambda i, te, nu: (jnp.minimum(i, nu[0] - 1), 0)),
                      pl.BlockSpec((None, None, D_MODEL, 2 * D_EXPERT), lambda i, te, nu: (layer, te[i], 0, 0)),
                      pl.BlockSpec((None, None, D_EXPERT, D_MODEL), lambda i, te, nu: (layer, te[i], 0, 0))],
            out_specs=pl.BlockSpec((tm, half), lambda i, te, nu: (i, 0))),
        compiler_params=_cparams(("arbitrary",)),
        name="experts",
    )(tile_expert, n_used, x_sorted, w_exp_in, w_exp_out)


def _combine_kernel(pos_ref, x_ref, wts_ref, mods_ref, nw_ref, nmods_ref, y_hbm, *rest, tm, n_ctx, n_tok, last):
    if last:
        f_ref, ybuf, sems = rest
    else:
        xo_ref, h_ref, ybuf, sems = rest
    i = pl.program_id(0)
    n = pl.num_programs(0)
    slot = lax.rem(i, 2)
    half = D_MODEL // 2

    def row_copy(src_row, s, k, r):
        return pltpu.make_async_copy(y_hbm.at[pl.ds(src_row, 1), :], ybuf.at[s, k, pl.ds(r, 1), :], sems.at[s])

    def issue(tile, s):
        def body(r, carry):
            for k in range(TOP_K):
                row_copy(pos_ref[k * n_tok + tile * tm + r], s, k, r).start()
            return carry
        lax.fori_loop(0, tm, body, 0, unroll=DMA_UNROLL)

    @pl.when(i == 0)
    def _():
        issue(0, 0)

    @pl.when(i + 1 < n)
    def _():
        issue(i + 1, 1 - slot)

    def wait_body(r, carry):
        for k in range(TOP_K):
            row_copy(0, slot, k, r).wait()
        return carry
    lax.fori_loop(0, tm, wait_body, 0, unroll=DMA_UNROLL)

    sel = jnp.where(i * tm < n_ctx, 0, 1)
    lo, hi = slice(0, half), slice(half, D_MODEL)

    def rows(b, carry):
        rs = pl.ds(pl.multiple_of(b * COMBINE_ROWS, COMBINE_ROWS), COMBINE_ROWS)
        w = wts_ref[rs, :]
        y0 = ybuf[slot, 0, rs, :]
        y1 = ybuf[slot, 1, rs, :]
        gate = mods_ref[sel, 5:6, :]
        halves = []
        for sl, unpack in ((lo, _unpack_lo), (hi, _unpack_hi)):
            moe = w[:, 0:1] * unpack(y0) + w[:, 1:2] * unpack(y1)
            halves.append(x_ref[rs, sl] + gate[:, sl] * moe)
        ssq = sum(jnp.sum(v * v, axis=-1, keepdims=True) for v in halves)
        inv = lax.rsqrt(ssq / D_MODEL + EPS)
        for v, sl in zip(halves, (lo, hi)):
            normed = v * inv * nw_ref[:, sl]
            if last:
                f_ref[rs, sl] = normed
            else:
                xo_ref[rs, sl] = v
                scale = nmods_ref[sel, 1:2, :][:, sl]
                shift = nmods_ref[sel, 0:1, :][:, sl]
                h_ref[rs, sl] = (normed * (1.0 + scale) + shift).astype(h_ref.dtype)
        return carry
    lax.fori_loop(0, tm // COMBINE_ROWS, rows, 0)


def combine(x, y_sorted, pos, wts_t, mods, next_norm_w, next_mods, n_ctx, last):
    t = x.shape[0]
    tm = CHUNK
    full = lambda i, p: (i, 0)
    if last:
        ctx_tiles = n_ctx // tm
        out_shape = jax.ShapeDtypeStruct((t - n_ctx, D_MODEL), F32)
        out_specs = pl.BlockSpec((tm, D_MODEL), lambda i, p: (jnp.maximum(i - ctx_tiles, 0), 0))
    else:
        out_shape = (jax.ShapeDtypeStruct((t, D_MODEL), F32), jax.ShapeDtypeStruct((t, D_MODEL), BF16))
        out_specs = (pl.BlockSpec((tm, D_MODEL), full), pl.BlockSpec((tm, D_MODEL), full))
    return pl.pallas_call(
        functools.partial(_combine_kernel, tm=tm, n_ctx=n_ctx, n_tok=t, last=last),
        out_shape=out_shape,
        grid_spec=pltpu.PrefetchScalarGridSpec(
            num_scalar_prefetch=1, grid=(t // tm,),
            in_specs=[pl.BlockSpec((tm, D_MODEL), full),
                      pl.BlockSpec((tm, TOP_K), full),
                      pl.BlockSpec((2, MOD_ROWS, D_MODEL), lambda i, p: (0, 0, 0)),
                      pl.BlockSpec((1, D_MODEL), lambda i, p: (0, 0)),
                      pl.BlockSpec((2, MOD_ROWS, D_MODEL), lambda i, p: (0, 0, 0)),
                      pl.BlockSpec(memory_space=pl.ANY)],
            out_specs=out_specs,
            scratch_shapes=[pltpu.VMEM((2, TOP_K, tm, D_MODEL // 2), jnp.uint32),
                            pltpu.SemaphoreType.DMA((2,))]),
        compiler_params=_cparams(("arbitrary",)),
        name="combine_last" if last else "combine",
    )(pos, x, wts_t, mods, next_norm_w.reshape(1, D_MODEL), next_mods, y_sorted)


_MAIN_PIECES = ((O_XBC, SSD_XBC), (O_GD, GATE_RANK), (O_Z, SSD_INNER), (O_FX, BRANCH_W), (O_MQ, MLSTM_QKW),
                (O_MV, BRANCH_W), (O_MO, BRANCH_W), (O_SU, BRANCH_W), (O_SV, BRANCH_W))


def _relayout_w_in_kernel(wt_ref, main_ref, gates_ref):
    col = 0
    for off, width in _MAIN_PIECES:
        main_ref[:, col:col + width] = wt_ref[off:off + width, :].T.astype(BF16)
        col += width
    tk = wt_ref.shape[1]
    pad = GATE_LANES - 2 * SSD_HEADS - 4 * MLSTM_HEADS
    narrow = jnp.concatenate([wt_ref[O_DT:O_DT + 2 * SSD_HEADS, :], wt_ref[O_MG:O_MG + 4 * MLSTM_HEADS, :],
                              jnp.zeros((pad, tk), F32)], axis=0)
    gates_ref[...] = narrow.T.astype(BF16)


def relayout_w_in(w_in):
    depth, d, d_in = w_in.shape
    tk = ROW_TILE
    return pl.pallas_call(
        _relayout_w_in_kernel,
        out_shape=(jax.ShapeDtypeStruct((depth, d, P_TOTAL), BF16),
                   jax.ShapeDtypeStruct((depth, d, GATE_LANES), BF16)),
        grid=(depth, d // tk),
        in_specs=[pl.BlockSpec((None, d_in, tk), lambda l, i: (l, 0, i))],
        out_specs=(pl.BlockSpec((None, tk, P_TOTAL), lambda l, i: (l, i, 0)),
                   pl.BlockSpec((None, tk, GATE_LANES), lambda l, i: (l, i, 0))),
        compiler_params=_cparams(("arbitrary", "arbitrary")),
        name="relayout_w_in",
    )(jnp.swapaxes(w_in, 1, 2))


def _prep_layer_params(l, p):
    pad = GATE_LANES - 2 * SSD_HEADS - 4 * MLSTM_HEADS
    gbias = jnp.concatenate([p['ssd_dt_bias'][l].reshape(-1), p['mlstm_gate_b'][l].reshape(-1),
                             jnp.zeros((pad,), F32)])
    alog = jnp.concatenate([p['ssd_a_log'][l].reshape(-1), jnp.zeros((GATE_LANES - 2 * SSD_HEADS,), F32)])
    feat = np.arange(GATE_LANES)[:, None]
    head = np.arange(SSD_INNER)[None, :] // SSD_HEAD_DIM
    pad_k = lambda w: jnp.concatenate([w, jnp.zeros((8 - CONV_K, w.shape[1]), F32)], axis=0)
    return {
        'gbias_row': gbias.reshape(1, -1), 'gbias_col': gbias.reshape(-1, 1),
        'alog_row': alog.reshape(1, -1), 'alog_col': alog.reshape(-1, 1),
        'expand_f': jnp.asarray(feat == head, BF16), 'expand_b': jnp.asarray(feat == head + SSD_HEADS, BF16),
        'ssd_conv_w': pad_k(p['ssd_conv_w'][l]), 'ssd_conv_b': p['ssd_conv_b'][l].reshape(1, -1),
        'ssd_d_e': jnp.repeat(p['ssd_d'][l], SSD_HEAD_DIM).reshape(1, -1),
        'ssd_norm_w': p['ssd_norm_w'][l].reshape(1, -1),
        'mlstm_conv_w': pad_k(p['mlstm_conv_w'][l]), 'mlstm_conv_b': p['mlstm_conv_b'][l].reshape(1, -1),
        'mlstm_norm_w': p['mlstm_norm_w'][l].reshape(1, -1),
        'sgu_norm_w': p['sgu_norm_w'][l].reshape(1, -1),
        'sgu_w': p['sgu_w'][l].astype(BF16),
        'sgu_b_t': jnp.concatenate([p['sgu_b'][l].T, jnp.zeros((CHUNK, GATE_LANES - SGU_GROUPS), F32)], axis=1),
    }


def kernel(x, c, ctx, c_ctx, ada_down, ada_up, ada_b, norm1_w, norm2_w, w_in, ssd_conv_w, ssd_conv_b,
           ssd_dt_bias, ssd_a_log, ssd_d, ssd_norm_w, mlstm_conv_w, mlstm_conv_b, mlstm_gate_b, mlstm_norm_w,
           sgu_norm_w, sgu_w, sgu_b, w_gate_up, w_branch, w_out, w_router, router_bias, w_exp_in, w_exp_out,
           final_norm_w):
    assert x.shape[0] == 1 and ctx.shape[0] == 1, "single-sequence kernel"
    w_main, w_gates = relayout_w_in(w_in)
    p = dict(w_main=w_main, w_gates=w_gates, ssd_conv_w=ssd_conv_w, ssd_conv_b=ssd_conv_b,
             ssd_dt_bias=ssd_dt_bias, ssd_a_log=ssd_a_log, ssd_d=ssd_d, ssd_norm_w=ssd_norm_w,
             mlstm_conv_w=mlstm_conv_w, mlstm_conv_b=mlstm_conv_b, mlstm_gate_b=mlstm_gate_b,
             mlstm_norm_w=mlstm_norm_w, sgu_norm_w=sgu_norm_w, sgu_w=sgu_w, sgu_b=sgu_b,
             w_gate_up=w_gate_up.astype(BF16), w_branch=w_branch.astype(BF16), w_out=w_out.astype(BF16))
    depth = w_in.shape[0]
    seq = x.shape[1]
    n_ctx = ctx.shape[1]
    t = n_ctx + seq
    assert n_ctx % ROW_TILE == 0 and seq % ROW_TILE == 0
    n_tiles = -(-(TOP_K * t) // EXPERT_TM) + N_EXPERTS

    xs = jnp.concatenate([ctx[0], x[0]], axis=0)
    cvec = jnp.concatenate([c_ctx[None], c, jnp.zeros((6, D_MODEL), F32)], axis=0)
    mods_all = ada_mods(cvec, ada_down, ada_up, ada_b)
    mods_all = mods_all[:, :2].reshape(depth, 2, N_MOD, D_MODEL)
    mods_all = jnp.concatenate([mods_all, jnp.zeros((depth, 2, MOD_ROWS - N_MOD, D_MODEL), F32)], axis=2)
    w_router_t = w_router.T

    h = norm_mod(xs, norm1_w[0], mods_all[0], n_ctx, 0, 1, BF16)
    for l in range(depth):
        lw = _prep_layer_params(l, p)
        mods = mods_all[l]
        last = l == depth - 1
        proj = matmul(h, p['w_main'], l, BF16, MM_TN)
        gates = matmul(h, p['w_gates'], l, F32, GATE_LANES)

        ys_b, xbc_conv, hm_b, qk_conv = scan_pair(proj, gates, lw, n_ctx)
        y_ssd, y_ml = scan_pair(proj, gates, lw, n_ctx, (xbc_conv, qk_conv), (ys_b, hm_b))
        fx = proj[:, P_FX:P_FX + BRANCH_W]
        y_four = jnp.concatenate([fourier_mix(fx[:n_ctx]), fourier_mix(fx[n_ctx:])], axis=0)
        y_sgu = sgu(proj, lw)

        merged = merge(proj, (y_ssd, y_four, y_ml, y_sgu), p['w_gate_up'], p['w_branch'], l)
        xs = outproj_residual(merged, p['w_out'], l, xs, mods, n_ctx, 2)

        h_packed, idx, wts, rank, counts = norm_router(xs, norm2_w[l], mods, w_router_t, router_bias, n_ctx)
        pos, tile_expert, n_used, pad_start, n_pad = _routing_metadata(idx, rank, counts, n_tiles)
        x_sorted, w_exp_in_b, w_exp_out_b = dispatch(h_packed, pos, pad_start, n_pad, n_used, n_tiles * EXPERT_TM,
                                                     w_exp_in, w_exp_out, l)
        y_sorted = experts(x_sorted, tile_expert, n_used, w_exp_in_b, w_exp_out_b, 0)
        if last:
            out = combine(xs, y_sorted, pos, wts.T, mods, final_norm_w, mods, n_ctx, True)
        else:
            xs, h = combine(xs, y_sorted, pos, wts.T, mods, norm1_w[l + 1], mods_all[l + 1], n_ctx, False)
    return out[None]
```

```python
import functools
import math

import numpy as np
import jax
import jax.numpy as jnp
from jax import lax
from jax.experimental import pallas as pl
from jax.experimental.pallas import tpu as pltpu

F32 = jnp.float32
BF16 = jnp.bfloat16
HIGHEST = lax.Precision.HIGHEST

D_MODEL = 4096
CHUNK = 128
EPS = 1e-6
ADA_RANK = 256
N_MOD = 6
MOD_ROWS = 8
N_BRANCH = 4
BRANCH_W = 1024
GATE_RANK = 512
SSD_HEADS = 16
SSD_HEAD_DIM = 64
SSD_INNER = SSD_HEADS * SSD_HEAD_DIM
SSD_GROUPS = 4
SSD_HPG = SSD_HEADS // SSD_GROUPS
SSD_STATE = 64
SSD_XBC = SSD_INNER + 2 * SSD_GROUPS * SSD_STATE
CONV_K = 5
CONV_PAD = CONV_K // 2
FOURIER_GROUPS = 4
FOURIER_GW = BRANCH_W // FOURIER_GROUPS
MLSTM_HEADS = 8
MLSTM_QK = 64
MLSTM_V = 128
MLSTM_QKW = 2 * MLSTM_HEADS * MLSTM_QK
SGU_GROUPS = 4
SGU_GW = BRANCH_W // SGU_GROUPS
N_EXPERTS = 16
N_EXPERT_GROUPS = 4
EXPERTS_PER_GROUP = N_EXPERTS // N_EXPERT_GROUPS
TOP_K = 2
D_EXPERT = 640

IN_SPLITS = (SSD_INNER, SSD_XBC, 2 * SSD_HEADS, BRANCH_W, MLSTM_HEADS * MLSTM_QK, MLSTM_HEADS * MLSTM_QK,
             MLSTM_HEADS * MLSTM_V, MLSTM_HEADS * MLSTM_V, 4 * MLSTM_HEADS, BRANCH_W, BRANCH_W, GATE_RANK)
_OFF = [0] + [int(v) for v in np.cumsum(IN_SPLITS)]
(O_Z, O_XBC, O_DT, O_FX, O_MQ, O_MK, O_MV, O_MO, O_MG, O_SU, O_SV, O_GD) = _OFF[:-1]

P_XBC = 0
P_GD = SSD_XBC
P_Z = P_GD + GATE_RANK
P_FX = P_Z + BRANCH_W
P_QK = P_FX + BRANCH_W
P_MV = P_QK + MLSTM_QKW
P_MO = P_MV + BRANCH_W
P_SU = P_MO + BRANCH_W
P_SV = P_SU + BRANCH_W
P_TOTAL = P_SV + BRANCH_W
GATE_LANES = 128
G_DT = 0
G_MG = 2 * SSD_HEADS

HALO = 16
VMEM_LIMIT = 56 * 1024 * 1024

ROW_TILE = 256
MM_TM = 1280
MM_TN = 512
EXPERT_TM = 512


def _cparams(sem):
    return pltpu.CompilerParams(dimension_semantics=sem, vmem_limit_bytes=VMEM_LIMIT)


def _pick_tile(n, pref):
    t = min(pref, n)
    while n % t:
        t -= ROW_TILE if t > ROW_TILE else 8
    return t


def _sigmoid(x):
    return 1.0 / (1.0 + jnp.exp(-x))


def _silu(x):
    return x * _sigmoid(x)


def _softplus(x):
    return jnp.maximum(x, 0.0) + jnp.log(1.0 + jnp.exp(-jnp.abs(x)))


def _log_sigmoid(x):
    return -_softplus(-x)


def _gelu_tanh(x):
    return 0.5 * x * (1.0 + jnp.tanh(math.sqrt(2.0 / math.pi) * (x + 0.044715 * (x * x * x))))


def _dot(a, b):
    return jnp.dot(a, b, preferred_element_type=F32)


def _dot_nt(a, b):
    return lax.dot_general(a, b, (((1,), (1,)), ((), ())), preferred_element_type=F32)


def _dot_hi(a, b):
    return jnp.dot(a, b, preferred_element_type=F32, precision=HIGHEST)


def _split_bf16(x, parts=3):
    out, rest = [], x
    for _ in range(parts):
        piece = rest.astype(BF16)
        out.append(piece)
        rest = rest - piece.astype(F32)
    return out


def _dot_f32_by_exact(a, b_exact):
    terms = [_dot(piece, b_exact) for piece in _split_bf16(a)]
    return terms[0] + terms[1] + terms[2]


def _dot_exact_by_f32(a_exact, b):
    terms = [_dot(a_exact, piece) for piece in _split_bf16(b)]
    return terms[0] + terms[1] + terms[2]


HI_MASK = 0xFFFF0000


def _pack_halves(v):
    bits = lax.bitcast_convert_type(v.astype(BF16).astype(F32), jnp.uint32)
    n = v.shape[1] // 2
    return (bits[:, :n] >> 16) | (bits[:, n:] & jnp.uint32(HI_MASK))


def _unpack_lo(p):
    return lax.bitcast_convert_type(p << 16, F32)


def _unpack_hi(p):
    return lax.bitcast_convert_type(p & jnp.uint32(HI_MASK), F32)


def _row_is_ctx(i, tm, n_ctx):
    row = i * tm + lax.broadcasted_iota(jnp.int32, (tm, 1), 0)
    return row < n_ctx


def _mod_row(mods_ref, is_ctx, idx):
    return jnp.where(is_ctx, mods_ref[0, idx:idx + 1, :], mods_ref[1, idx:idx + 1, :])


def _ada_kernel(c_ref, down_ref, up_ref, b_ref, o_ref, low_ref):
    @pl.when(pl.program_id(1) == 0)
    def _():
        low_ref[...] = _dot_hi(_silu(c_ref[...]), down_ref[...])

    o_ref[...] = _dot_hi(low_ref[...], up_ref[...]) + b_ref[...]


def ada_mods(cvec, ada_down, ada_up, ada_b):
    depth = ada_down.shape[0]
    n_out = ada_up.shape[2]
    tn = 2048
    return pl.pallas_call(
        _ada_kernel,
        out_shape=jax.ShapeDtypeStruct((depth, 8, n_out), F32),
        grid=(depth, n_out // tn),
        in_specs=[pl.BlockSpec((8, D_MODEL), lambda l, j: (0, 0)),
                  pl.BlockSpec((None, D_MODEL, ADA_RANK), lambda l, j: (l, 0, 0)),
                  pl.BlockSpec((None, ADA_RANK, tn), lambda l, j: (l, 0, j)),
                  pl.BlockSpec((None, 1, tn), lambda l, j: (l, 0, j))],
        out_specs=pl.BlockSpec((None, 8, tn), lambda l, j: (l, 0, j)),
        scratch_shapes=[pltpu.VMEM((8, ADA_RANK), F32)],
        compiler_params=_cparams(("arbitrary", "arbitrary")),
        name="ada_mods",
    )(cvec, ada_down, ada_up, ada_b.reshape(depth, 1, n_out))


def _norm_kernel(x_ref, w_ref, mods_ref, o_ref, *, tm, n_ctx, shift_idx, scale_idx, row_off):
    x = x_ref[...]
    y = x * lax.rsqrt(jnp.mean(x * x, axis=-1, keepdims=True) + EPS) * w_ref[...]
    if shift_idx is not None:
        is_ctx = _row_is_ctx(pl.program_id(0) + row_off, tm, n_ctx)
        y = y * (1.0 + _mod_row(mods_ref, is_ctx, scale_idx)) + _mod_row(mods_ref, is_ctx, shift_idx)
    o_ref[...] = y.astype(o_ref.dtype)


def norm_mod(x, w, mods, n_ctx, shift_idx, scale_idx, out_dtype, row_start=0):
    t = x.shape[0] - row_start
    tm = ROW_TILE
    row_off = row_start // tm
    return pl.pallas_call(
        functools.partial(_norm_kernel, tm=tm, n_ctx=n_ctx, shift_idx=shift_idx, scale_idx=scale_idx,
                          row_off=row_off),
        out_shape=jax.ShapeDtypeStruct((t, D_MODEL), out_dtype),
        grid=(t // tm,),
        in_specs=[pl.BlockSpec((tm, D_MODEL), lambda i: (i + row_off, 0)),
                  pl.BlockSpec((1, D_MODEL), lambda i: (0, 0)),
                  pl.BlockSpec((2, MOD_ROWS, D_MODEL), lambda i: (0, 0, 0))],
        out_specs=pl.BlockSpec((tm, D_MODEL), lambda i: (i, 0)),
        compiler_params=_cparams(("arbitrary",)),
        name="norm_mod",
    )(x, w.reshape(1, D_MODEL), mods)


def _mm_kernel(a_ref, b_ref, o_ref):
    o_ref[...] = _dot(a_ref[...], b_ref[...]).astype(o_ref.dtype)


def matmul(a, b_stack, layer, out_dtype, tn):
    m, k = a.shape
    n = b_stack.shape[2]
    tm = _pick_tile(m, MM_TM)
    return pl.pallas_call(
        _mm_kernel,
        out_shape=jax.ShapeDtypeStruct((m, n), out_dtype),
        grid=(m // tm, n // tn),
        in_specs=[pl.BlockSpec((tm, k), lambda i, j: (i, 0)),
                  pl.BlockSpec((None, k, tn), lambda i, j: (layer, 0, j))],
        out_specs=pl.BlockSpec((tm, tn), lambda i, j: (i, j)),
        compiler_params=_cparams(("arbitrary", "arbitrary")),
        name="matmul",
    )(a, b_stack)


def _conv_silu(cur_ref, prev_ref, next_ref, cw_ref, cb_ref, ext_ref, first, last):
    prev = jnp.where(first, 0.0, prev_ref[...].astype(F32))
    nxt = jnp.where(last, 0.0, next_ref[...].astype(F32))
    ext_ref[0:HALO, :] = prev
    ext_ref[HALO:HALO + CHUNK, :] = cur_ref[...].astype(F32)
    ext_ref[HALO + CHUNK:2 * HALO + CHUNK, :] = nxt
    acc = cb_ref[...] + cw_ref[0:1, :] * ext_ref[pl.ds(HALO - CONV_PAD, CHUNK), :]
    for k in range(1, CONV_K):
        acc = acc + cw_ref[k:k + 1, :] * ext_ref[pl.ds(HALO - CONV_PAD + k, CHUNK), :]
    return _silu(acc)


def _tri_masks():
    r = lax.broadcasted_iota(jnp.int32, (CHUNK, CHUNK), 0)
    c = lax.broadcasted_iota(jnp.int32, (CHUNK, CHUNK), 1)
    return c <= r, c >= r


def _cumsums(col, row, reverse):
    lower, upper = _tri_masks()
    lo = jnp.where(lower, 1.0, 0.0).astype(BF16)
    up = jnp.where(upper, 1.0, 0.0).astype(BF16)
    if reverse:
        return _dot_exact_by_f32(up, col), _dot_f32_by_exact(row, lo)
    return _dot_exact_by_f32(lo, col), _dot_f32_by_exact(row, up)


def _chunk_order(n_ctx_chunks, n_chunks, reverse):
    if not reverse:
        return jnp.arange(n_chunks, dtype=jnp.int32)
    return jnp.concatenate([jnp.arange(n_ctx_chunks - 1, -1, -1, dtype=jnp.int32),
                            jnp.arange(n_chunks - 1, n_ctx_chunks - 1, -1, dtype=jnp.int32)])


def _halo_specs(width, col_block, n_rows):
    per = CHUNK // HALO
    last_halo = n_rows // HALO - 1
    cur = pl.BlockSpec((CHUNK, width), lambda c, o: (o[c], col_block))
    prev = pl.BlockSpec((HALO, width), lambda c, o: (jnp.maximum(o[c] * per - 1, 0), col_block))
    nxt = pl.BlockSpec((HALO, width), lambda c, o: (jnp.minimum((o[c] + 1) * per, last_halo), col_block))
    return [cur, prev, nxt]


def _ssd_kernel(order_ref, *refs, reverse, final, n_ctx_chunks, n_chunks, init=True):
    if final:
        (xc_ref, gates_ref, gbias_row_ref, alog_row_ref, alog_col_ref, gbias_col_ref, expand_ref,
         yb_ref, z_ref, dskip_ref, normw_ref, o_ref, h_ref) = refs
    else:
        (xbc_ref, xprev_ref, xnext_ref, cw_ref, cb_ref, gates_ref, gbias_row_ref, alog_row_ref, alog_col_ref,
         gbias_col_ref, expand_ref, o_ref, xc_ref, ext_ref, h_ref) = refs
    c = pl.program_id(0)
    ci = order_ref[c]

    if init:
        @pl.when(c == 0)
        def _():
            h_ref[...] = jnp.zeros_like(h_ref)

    if final:
        xbc = xc_ref[...].astype(F32)
    else:
        first = jnp.logical_or(ci == 0, ci == n_ctx_chunks)
        last = jnp.logical_or(ci == n_ctx_chunks - 1, ci == n_chunks - 1)
        xbc = _conv_silu(xbc_ref, xprev_ref, xnext_ref, cw_ref, cb_ref, ext_ref, first, last)
        xc_ref[...] = xbc.astype(xc_ref.dtype)
    xs = xbc[:, :SSD_INNER]
    bm = xbc[:, SSD_INNER:SSD_INNER + SSD_GROUPS * SSD_STATE]
    cm = xbc[:, SSD_INNER + SSD_GROUPS * SSD_STATE:]

    lane0 = SSD_HEADS if reverse else 0
    g = gates_ref[...]
    dt_col = _softplus(g + gbias_row_ref[...])
    dt_row = _softplus(g.T + gbias_col_ref[...])
    acum_col, acum_row = _cumsums(dt_col * -jnp.exp(alog_row_ref[...]),
                                  dt_row * -jnp.exp(alog_col_ref[...]), reverse)
    both_e = _dot_f32_by_exact(jnp.concatenate([dt_col, acum_col], axis=0), expand_ref[...])
    dt_e = both_e[:CHUNK]
    acum_e = both_e[CHUNK:]
    alast_e = acum_e[0:1, :] if reverse else acum_e[CHUNK - 1:CHUNK, :]
    xdt = xs * dt_e
    xdt_b = xdt.astype(BF16)
    xw_b = (xdt * jnp.exp(alast_e - acum_e)).astype(BF16)
    dec_e = jnp.exp(alast_e)
    lower, upper = _tri_masks()
    mask = upper if reverse else lower
    cm_b = cm.astype(BF16)
    bm_b = bm.astype(BF16)
    bm_t = bm.T.astype(BF16)
    lane = lax.broadcasted_iota(jnp.int32, (CHUNK, 2 * SSD_HEAD_DIM), 1)
    gw = SSD_HPG * SSD_HEAD_DIM
    ys, inters = [], []
    for grp in range(SSD_GROUPS):
        c_g = cm_b[:, grp * SSD_STATE:(grp + 1) * SSD_STATE]
        b_g = bm_b[:, grp * SSD_STATE:(grp + 1) * SSD_STATE]
        cb = _dot_nt(c_g, b_g)
        h_t = h_ref[grp]
        inters.append(_dot(c_g, h_t.astype(BF16)))
        for pair in range(SSD_HPG // 2):
            ps = []
            for k in range(2):
                f = lane0 + grp * SSD_HPG + pair * 2 + k
                seg = acum_col[:, f:f + 1] - acum_row[f:f + 1, :]
                ps.append((cb * jnp.exp(jnp.where(mask, seg, -jnp.inf))).astype(BF16))
            col0 = (grp * (SSD_HPG // 2) + pair) * 2 * SSD_HEAD_DIM
            xp = xdt_b[:, col0:col0 + 2 * SSD_HEAD_DIM]
            rhs = jnp.concatenate([jnp.where(lane < SSD_HEAD_DIM, xp, jnp.zeros_like(xp)),
                                   jnp.where(lane >= SSD_HEAD_DIM, xp, jnp.zeros_like(xp))], axis=0)
            ys.append(_dot(jnp.concatenate(ps, axis=1), rhs))
        new = _dot(bm_t[grp * SSD_STATE:(grp + 1) * SSD_STATE, :], xw_b[:, grp * gw:(grp + 1) * gw])
        h_ref[grp] = dec_e[:, grp * gw:(grp + 1) * gw] * h_t + new
    y = jnp.concatenate(ys, axis=1) + jnp.exp(acum_e) * jnp.concatenate(inters, axis=1)
    if final:
        y = y + yb_ref[...].astype(F32) + dskip_ref[...] * xs
        z = z_ref[...].astype(F32)
        y = y * _silu(z)
        y = y * lax.rsqrt(jnp.mean(y * y, axis=-1, keepdims=True) + EPS) * normw_ref[...]
    o_ref[...] = y.astype(o_ref.dtype)


def ssd_pass(proj, gates, lw, n_ctx, xbc_conv=None, y_other=None):
    t = proj.shape[0]
    n_chunks = t // CHUNK
    n_ctx_chunks = n_ctx // CHUNK
    final = xbc_conv is not None
    reverse = not final
    order = _chunk_order(n_ctx_chunks, n_chunks, reverse)
    const = lambda c, o: (0, 0)
    chunk = lambda c, o: (o[c], 0)
    gate_specs = [pl.BlockSpec((CHUNK, GATE_LANES), chunk),
                  pl.BlockSpec((1, GATE_LANES), const),
                  pl.BlockSpec((1, GATE_LANES), const),
                  pl.BlockSpec((GATE_LANES, 1), const),
                  pl.BlockSpec((GATE_LANES, 1), const),
                  pl.BlockSpec((GATE_LANES, SSD_INNER), const)]
    gate_args = [gates, lw['gbias_row'], lw['alog_row'], lw['alog_col'], lw['gbias_col'],
                 lw['expand_b' if reverse else 'expand_f']]
    y_shape = jax.ShapeDtypeStruct((t, SSD_INNER), BF16)
    y_spec = pl.BlockSpec((CHUNK, SSD_INNER), chunk)
    state = pltpu.VMEM((SSD_GROUPS, SSD_STATE, SSD_HPG * SSD_HEAD_DIM), F32)
    if final:
        in_specs = [pl.BlockSpec((CHUNK, SSD_XBC), chunk)] + gate_specs + [
            pl.BlockSpec((CHUNK, SSD_INNER), chunk),
            pl.BlockSpec((CHUNK, SSD_INNER), lambda c, o: (o[c], P_Z // SSD_INNER)),
            pl.BlockSpec((1, SSD_INNER), const),
            pl.BlockSpec((1, SSD_INNER), const)]
        args = [xbc_conv] + gate_args + [y_other, proj, lw['ssd_d_e'], lw['ssd_norm_w']]
        out_shape, out_specs, scratch = [y_shape], [y_spec], [state]
    else:
        in_specs = _halo_specs(SSD_XBC, P_XBC // SSD_XBC, t) + [
            pl.BlockSpec((8, SSD_XBC), const), pl.BlockSpec((1, SSD_XBC), const)] + gate_specs
        args = [proj, proj, proj, lw['ssd_conv_w'], lw['ssd_conv_b']] + gate_args
        out_shape = [y_shape, jax.ShapeDtypeStruct((t, SSD_XBC), BF16)]
        out_specs = [y_spec, pl.BlockSpec((CHUNK, SSD_XBC), chunk)]
        scratch = [pltpu.VMEM((CHUNK + 2 * HALO, SSD_XBC), F32), state]
    return dict(in_specs=in_specs, args=args, out_shape=out_shape, out_specs=out_specs, scratch=scratch,
                n_state=1)


def _mlstm_kernel(order_ref, *refs, reverse, final, n_ctx_chunks, n_chunks, init=True):
    if final:
        (qc_ref, v_ref, gates_ref, gbias_row_ref, gbias_col_ref, hb_ref, mo_ref, normw_ref,
         o_ref, cn_ref, m_ref) = refs
    else:
        (qk_ref, qprev_ref, qnext_ref, cw_ref, cb_ref, v_ref, gates_ref, gbias_row_ref, gbias_col_ref,
         o_ref, qc_ref, ext_ref, cn_ref, m_ref) = refs
    c = pl.program_id(0)
    ci = order_ref[c]

    if init:
        @pl.when(c == 0)
        def _():
            cn_ref[...] = jnp.zeros_like(cn_ref)
            m_ref[...] = jnp.zeros_like(m_ref)

    if final:
        qk = qc_ref[...].astype(F32)
    else:
        first = jnp.logical_or(ci == 0, ci == n_ctx_chunks)
        last = jnp.logical_or(ci == n_ctx_chunks - 1, ci == n_chunks - 1)
        qk = _conv_silu(qk_ref, qprev_ref, qnext_ref, cw_ref, cb_ref, ext_ref, first, last)
        qc_ref[...] = qk.astype(qc_ref.dtype)
    hq = MLSTM_HEADS * MLSTM_QK
    q_b = qk[:, :hq].astype(BF16)
    k = qk[:, hq:] * (MLSTM_QK ** -0.5)
    k_b = k.astype(BF16)
    k_t = k.T
    v_b = v_ref[...]

    icol = G_MG + (2 * MLSTM_HEADS if reverse else 0)
    fcol = icol + MLSTM_HEADS
    g_col = gates_ref[...] + gbias_row_ref[...]
    g_row = gates_ref[...].T + gbias_col_ref[...]
    bcum_col, bcum_row = _cumsums(_log_sigmoid(g_col), _log_sigmoid(g_row), reverse)
    lower, upper = _tri_masks()
    mask = upper if reverse else lower
    ones_col = (lax.broadcasted_iota(jnp.int32, (CHUNK, MLSTM_V), 1) == 0).astype(BF16)

    for h in range(MLSTM_HEADS):
        bc = bcum_col[:, fcol + h:fcol + h + 1]
        br = bcum_row[fcol + h:fcol + h + 1, :]
        ig = g_row[icol + h:icol + h + 1, :]
        dm = jnp.where(mask, bc - br + ig, -jnp.inf)
        m_in = m_ref[h:h + 1, 0:1]
        w_inter = bc + m_in
        m_t = jnp.maximum(w_inter, jnp.max(dm, axis=1, keepdims=True))
        q_h = q_b[:, h * MLSTM_QK:(h + 1) * MLSTM_QK]
        s = _dot_nt(q_h, k_b[:, h * MLSTM_QK:(h + 1) * MLSTM_QK]) * jnp.exp(dm - m_t)
        v_ext = jnp.concatenate([v_b[:, h * MLSTM_V:(h + 1) * MLSTM_V], ones_col], axis=1)
        cn = cn_ref[h]
        tot = _dot(s.astype(BF16), v_ext) + jnp.exp(w_inter - m_t) * _dot(q_h, cn.astype(BF16))
        num = tot[:, :MLSTM_V]
        den = tot[:, MLSTM_V:MLSTM_V + 1]
        hh = num / jnp.maximum(jnp.abs(den), jnp.exp(-m_t))

        b_last = br[:, 0:1] if reverse else br[:, CHUNK - 1:CHUNK]
        w_end = b_last - br + ig
        m_loc = jnp.max(w_end, axis=1, keepdims=True)
        kw_t = (k_t[h * MLSTM_QK:(h + 1) * MLSTM_QK, :] * jnp.exp(w_end - m_loc)).astype(BF16)
        m_new = jnp.maximum(b_last + m_in, m_loc)
        cn_ref[h] = jnp.exp(b_last + m_in - m_new) * cn + jnp.exp(m_loc - m_new) * _dot(kw_t, v_ext)
        m_ref[h:h + 1, :] = jnp.broadcast_to(m_new, (1, GATE_LANES))

        sl = slice(h * MLSTM_V, (h + 1) * MLSTM_V)
        if final:
            hh = hh + hb_ref[:, sl].astype(F32)
            hh = hh * lax.rsqrt(jnp.mean(hh * hh, axis=-1, keepdims=True) + EPS) * normw_ref[:, sl]
            hh = _sigmoid(mo_ref[:, sl].astype(F32)) * hh
        o_ref[:, sl] = hh.astype(o_ref.dtype)


def mlstm_pass(proj, gates, lw, n_ctx, qk_conv=None, h_other=None):
    t = proj.shape[0]
    n_chunks = t // CHUNK
    n_ctx_chunks = n_ctx // CHUNK
    final = qk_conv is not None
    reverse = not final
    order = _chunk_order(n_ctx_chunks, n_chunks, reverse)
    const = lambda c, o: (0, 0)
    chunk = lambda c, o: (o[c], 0)
    common_specs = [pl.BlockSpec((CHUNK, BRANCH_W), lambda c, o: (o[c], P_MV // BRANCH_W)),
                    pl.BlockSpec((CHUNK, GATE_LANES), chunk),
                    pl.BlockSpec((1, GATE_LANES), const),
                    pl.BlockSpec((GATE_LANES, 1), const)]
    common_args = [proj, gates, lw['gbias_row'], lw['gbias_col']]
    h_shape = jax.ShapeDtypeStruct((t, BRANCH_W), BF16)
    h_spec = pl.BlockSpec((CHUNK, BRANCH_W), chunk)
    state = [pltpu.VMEM((MLSTM_HEADS, MLSTM_QK, 2 * MLSTM_V), F32), pltpu.VMEM((MLSTM_HEADS, GATE_LANES), F32)]
    if final:
        in_specs = [pl.BlockSpec((CHUNK, MLSTM_QKW), chunk)] + common_specs + [
            pl.BlockSpec((CHUNK, BRANCH_W), chunk),
            pl.BlockSpec((CHUNK, BRANCH_W), lambda c, o: (o[c], P_MO // BRANCH_W)),
            pl.BlockSpec((1, BRANCH_W), const)]
        args = [qk_conv] + common_args + [h_other, proj, lw['mlstm_norm_w']]
        out_shape, out_specs, scratch = [h_shape], [h_spec], state
    else:
        in_specs = _halo_specs(MLSTM_QKW, P_QK // MLSTM_QKW, t) + [
            pl.BlockSpec((8, MLSTM_QKW), const), pl.BlockSpec((1, MLSTM_QKW), const)] + common_specs
        args = [proj, proj, proj, lw['mlstm_conv_w'], lw['mlstm_conv_b']] + common_args
        out_shape = [h_shape, jax.ShapeDtypeStruct((t, MLSTM_QKW), BF16)]
        out_specs = [h_spec, pl.BlockSpec((CHUNK, MLSTM_QKW), chunk)]
        scratch = [pltpu.VMEM((CHUNK + 2 * HALO, MLSTM_QKW), F32)] + state
    return dict(in_specs=in_specs, args=args, out_shape=out_shape, out_specs=out_specs, scratch=scratch,
                n_state=2)


def _scan_pair_kernel(order_ref, *refs, n_in, n_out, n_scr, n_state, **static):
    groups, at = [], 0
    for counts in (n_in, n_out, n_scr):
        pair = []
        for cnt in counts:
            pair.append(refs[at:at + cnt])
            at += cnt
        groups.append(pair)
    (in_s, in_m), (out_s, out_m), (scr_s, scr_m) = groups

    @pl.when(pl.program_id(0) == 0)
    def _():
        for ref in scr_s[len(scr_s) - n_state[0]:] + scr_m[len(scr_m) - n_state[1]:]:
            ref[...] = jnp.zeros_like(ref)

    _ssd_kernel(order_ref, *in_s, *out_s, *scr_s, init=False, **static)
    _mlstm_kernel(order_ref, *in_m, *out_m, *scr_m, init=False, **static)


def scan_pair(proj, gates, lw, n_ctx, convs=None, others=None):
    t = proj.shape[0]
    n_chunks = t // CHUNK
    n_ctx_chunks = n_ctx // CHUNK
    final = convs is not None
    reverse = not final
    order = _chunk_order(n_ctx_chunks, n_chunks, reverse)
    ssd = ssd_pass(proj, gates, lw, n_ctx, *((convs[0], others[0]) if final else ()))
    mls = mlstm_pass(proj, gates, lw, n_ctx, *((convs[1], others[1]) if final else ()))
    count = lambda key: (len(ssd[key]), len(mls[key]))
    return pl.pallas_call(
        functools.partial(_scan_pair_kernel, n_in=count('in_specs'), n_out=count('out_shape'),
                          n_scr=count('scratch'), n_state=(ssd['n_state'], mls['n_state']),
                          reverse=reverse, final=final, n_ctx_chunks=n_ctx_chunks, n_chunks=n_chunks),
        out_shape=tuple(ssd['out_shape'] + mls['out_shape']),
        grid_spec=pltpu.PrefetchScalarGridSpec(
            num_scalar_prefetch=1, grid=(n_chunks,), in_specs=ssd['in_specs'] + mls['in_specs'],
            out_specs=tuple(ssd['out_specs'] + mls['out_specs']),
            scratch_shapes=ssd['scratch'] + mls['scratch']),
        compiler_params=_cparams(("arbitrary",)),
        name="scan_bwd" if reverse else "scan_fwd",
    )(order, *ssd['args'], *mls['args'])


def _fft_split(length):
    l1 = 1 << (int(math.log2(length)) // 2)
    return l1, length // l1


def _fft1_kernel(x_ref, cs_ref, f1_ref, twc_ref, tws_ref, ure_ref, uim_ref, *, l1):
    x = x_ref[...]
    cs = cs_ref[...]
    a, b = [], []
    for grp in range(FOURIER_GROUPS):
        ab = _dot(x[:, grp * FOURIER_GW:(grp + 1) * FOURIER_GW], cs)
        a.append(ab[:, :FOURIER_GW])
        b.append(ab[:, FOURIER_GW:])
    ab = jnp.concatenate(a + b, axis=1).astype(BF16)
    prod = _dot(f1_ref[...], ab)
    c_a = prod[:l1, :BRANCH_W]
    c_b = prod[:l1, BRANCH_W:]
    s_a = prod[l1:, :BRANCH_W]
    s_b = prod[l1:, BRANCH_W:]
    u_re = c_a - s_b
    u_im = -(c_b + s_a)
    twc = twc_ref[...]
    tws = tws_ref[...]
    ure_ref[...] = (u_re * twc + u_im * tws).astype(ure_ref.dtype)
    uim_ref[...] = (u_im * twc - u_re * tws).astype(uim_ref.dtype)


def _fft2_kernel(ure_ref, uim_ref, f2_ref, o_ref):
    u = jnp.concatenate([ure_ref[...], uim_ref[...]], axis=0)
    o_ref[...] = _dot(f2_ref[...], u).astype(o_ref.dtype)


def _fft_tables(length):
    l1, l2 = _fft_split(length)
    scale = 1.0 / math.sqrt(length * FOURIER_GW)
    kc = np.arange(FOURIER_GW)
    ang_c = 2.0 * np.pi * np.outer(kc, kc) / FOURIER_GW
    cs = np.concatenate([np.cos(ang_c), np.sin(ang_c)], axis=1) * scale
    k1 = np.arange(l1)
    ang1 = 2.0 * np.pi * np.outer(k1, k1) / l1
    f1 = np.concatenate([np.cos(ang1), np.sin(ang1)], axis=0)
    n2 = np.arange(l2)
    ang_t = 2.0 * np.pi * np.outer(n2, k1) / length
    ang2 = 2.0 * np.pi * np.outer(n2, n2) / l2
    f2 = np.concatenate([np.cos(ang2), np.sin(ang2)], axis=1)
    return (jnp.asarray(cs, BF16), jnp.asarray(f1, BF16), jnp.asarray(np.cos(ang_t)[:, :, None], F32),
            jnp.asarray(np.sin(ang_t)[:, :, None], F32), jnp.asarray(f2, BF16))


def fourier_mix(fx):
    length = fx.shape[0]
    l1, l2 = _fft_split(length)
    cs, f1, twc, tws, f2 = _fft_tables(length)
    u_shape = jax.ShapeDtypeStruct((l1, l2 * BRANCH_W), BF16)
    ure, uim = pl.pallas_call(
        functools.partial(_fft1_kernel, l1=l1),
        out_shape=(u_shape, u_shape),
        grid=(l2,),
        in_specs=[pl.BlockSpec((l1, BRANCH_W), lambda j: (0, j)),
                  pl.BlockSpec((FOURIER_GW, 2 * FOURIER_GW), lambda j: (0, 0)),
                  pl.BlockSpec((2 * l1, l1), lambda j: (0, 0)),
                  pl.BlockSpec((None, l1, 1), lambda j: (j, 0, 0)),
                  pl.BlockSpec((None, l1, 1), lambda j: (j, 0, 0))],
        out_specs=(pl.BlockSpec((l1, BRANCH_W), lambda j: (0, j)),
                   pl.BlockSpec((l1, BRANCH_W), lambda j: (0, j))),
        compiler_params=_cparams(("arbitrary",)),
        name="fft_stage1",
    )(fx.reshape(l1, l2 * BRANCH_W), cs, f1, twc, tws)
    out = pl.pallas_call(
        _fft2_kernel,
        out_shape=jax.ShapeDtypeStruct((l2, l1 * BRANCH_W), BF16),
        grid=(l1,),
        in_specs=[pl.BlockSpec((None, l2, BRANCH_W), lambda i: (i, 0, 0)),
                  pl.BlockSpec((None, l2, BRANCH_W), lambda i: (i, 0, 0)),
                  pl.BlockSpec((l2, 2 * l2), lambda i: (0, 0))],
        out_specs=pl.BlockSpec((l2, BRANCH_W), lambda i: (0, i)),
        compiler_params=_cparams(("arbitrary",)),
        name="fft_stage2",
    )(ure.reshape(l1, l2, BRANCH_W), uim.reshape(l1, l2, BRANCH_W), f2)
    return out.reshape(length, BRANCH_W)


def _sgu_kernel(su_ref, sv_ref, normw_ref, w_ref, b_ref, o_ref):
    v = _gelu_tanh(sv_ref[...].astype(F32))
    v = (v * lax.rsqrt(jnp.mean(v * v, axis=-1, keepdims=True) + EPS) * normw_ref[...]).astype(BF16)
    mix = []
    for grp in range(SGU_GROUPS):
        mix.append(_dot(w_ref[grp], v[:, grp * SGU_GW:(grp + 1) * SGU_GW]) + b_ref[:, grp:grp + 1])
    o_ref[...] = (_gelu_tanh(su_ref[...].astype(F32)) * jnp.concatenate(mix, axis=1)).astype(o_ref.dtype)


def sgu(proj, lw):
    t = proj.shape[0]
    return pl.pallas_call(
        _sgu_kernel,
        out_shape=jax.ShapeDtypeStruct((t, BRANCH_W), BF16),
        grid=(t // CHUNK,),
        in_specs=[pl.BlockSpec((CHUNK, BRANCH_W), lambda c: (c, P_SU // BRANCH_W)),
                  pl.BlockSpec((CHUNK, BRANCH_W), lambda c: (c, P_SV // BRANCH_W)),
                  pl.BlockSpec((1, BRANCH_W), lambda c: (0, 0)),
                  pl.BlockSpec((SGU_GROUPS, CHUNK, CHUNK), lambda c: (0, 0, 0)),
                  pl.BlockSpec((CHUNK, GATE_LANES), lambda c: (0, 0))],
        out_specs=pl.BlockSpec((CHUNK, BRANCH_W), lambda c: (c, 0)),
        compiler_params=_cparams(("arbitrary",)),
        name="sgu",
    )(proj, proj, lw['sgu_norm_w'], lw['sgu_w'], lw['sgu_b_t'])


def _merge_kernel(gd_ref, b0_ref, b1_ref, b2_ref, b3_ref, wg_ref, wb_ref, o_ref):
    gd = gd_ref[...]
    acc = None
    for n, b_ref in enumerate((b0_ref, b1_ref, b2_ref, b3_ref)):
        term = _sigmoid(_dot(gd, wg_ref[n])) * _dot(b_ref[...], wb_ref[n])
        acc = term if acc is None else acc + term
    o_ref[...] = acc.astype(o_ref.dtype)


def merge(proj, branches, w_gate_up, w_branch, layer):
    t = proj.shape[0]
    tm = _pick_tile(t, MM_TM)
    tn = MM_TN
    row = lambda i, j: (i, 0)
    return pl.pallas_call(
        _merge_kernel,
        out_shape=jax.ShapeDtypeStruct((t, D_MODEL), BF16),
        grid=(t // tm, D_MODEL // tn),
        in_specs=[pl.BlockSpec((tm, GATE_RANK), lambda i, j: (i, P_GD // GATE_RANK))] +
                 [pl.BlockSpec((tm, BRANCH_W), row)] * N_BRANCH +
                 [pl.BlockSpec((None, N_BRANCH, GATE_RANK, tn), lambda i, j: (layer, 0, 0, j)),
                  pl.BlockSpec((None, N_BRANCH, BRANCH_W, tn), lambda i, j: (layer, 0, 0, j))],
        out_specs=pl.BlockSpec((tm, tn), lambda i, j: (i, j)),
        compiler_params=_cparams(("arbitrary", "arbitrary")),
        name="merge",
    )(proj, *branches, w_gate_up, w_branch)


def _outproj_kernel(a_ref, w_ref, x_ref, mods_ref, o_ref, *, tm, n_ctx, gate_idx):
    is_ctx = _row_is_ctx(pl.program_id(0), tm, n_ctx)
    o_ref[...] = x_ref[...] + _mod_row(mods_ref, is_ctx, gate_idx) * _dot(a_ref[...], w_ref[...])


def outproj_residual(a, w_stack, layer, x, mods, n_ctx, gate_idx):
    t = a.shape[0]
    tm = _pick_tile(t, MM_TM)
    tn = MM_TN
    return pl.pallas_call(
        functools.partial(_outproj_kernel, tm=tm, n_ctx=n_ctx, gate_idx=gate_idx),
        out_shape=jax.ShapeDtypeStruct((t, D_MODEL), F32),
        grid=(t // tm, D_MODEL // tn),
        in_specs=[pl.BlockSpec((tm, D_MODEL), lambda i, j: (i, 0)),
                  pl.BlockSpec((None, D_MODEL, tn), lambda i, j: (layer, 0, j)),
                  pl.BlockSpec((tm, tn), lambda i, j: (i, j)),
                  pl.BlockSpec((2, MOD_ROWS, tn), lambda i, j: (0, 0, j))],
        out_specs=pl.BlockSpec((tm, tn), lambda i, j: (i, j)),
        compiler_params=_cparams(("arbitrary", "arbitrary")),
        name="outproj_residual",
    )(a, w_stack, x, mods)


def _router_kernel(x_ref, w_ref, mods_ref, wr_ref, rb_ref, h_ref, idx_ref, wts_ref, rank_ref, cnt_ref, base_ref,
                   *, tm, n_ctx):
    @pl.when(pl.program_id(0) == 0)
    def _():
        base_ref[...] = jnp.zeros_like(base_ref)

    x = x_ref[...]
    is_ctx = _row_is_ctx(pl.program_id(0), tm, n_ctx)
    y = x * lax.rsqrt(jnp.mean(x * x, axis=-1, keepdims=True) + EPS) * w_ref[...]
    h = y * (1.0 + _mod_row(mods_ref, is_ctx, 4)) + _mod_row(mods_ref, is_ctx, 3)
    h_ref[...] = _pack_halves(h)
    logits = lax.dot_general(wr_ref[...], h, (((1,), (1,)), ((), ())), preferred_element_type=F32,
                             precision=HIGHEST)
    aff = _sigmoid(logits)
    sel = aff + rb_ref[...]
    rows = [sel[e:e + 1, :] for e in range(N_EXPERTS)]
    affs = [aff[e:e + 1, :] for e in range(N_EXPERTS)]
    gscore = []
    for grp in range(N_EXPERT_GROUPS):
        r = rows[grp * EXPERTS_PER_GROUP:(grp + 1) * EXPERTS_PER_GROUP]
        best = None
        for a in range(EXPERTS_PER_GROUP):
            for b in range(a + 1, EXPERTS_PER_GROUP):
                pair = r[a] + r[b]
                best = pair if best is None else jnp.maximum(best, pair)
        gscore.append(best)
    g_best = gscore[0]
    g_idx = jnp.zeros_like(g_best, dtype=jnp.int32)
    for grp in range(1, N_EXPERT_GROUPS):
        better = gscore[grp] > g_best
        g_best = jnp.where(better, gscore[grp], g_best)
        g_idx = jnp.where(better, grp, g_idx)
    masked = [jnp.where(g_idx == e // EXPERTS_PER_GROUP, rows[e], -jnp.inf) for e in range(N_EXPERTS)]

    def argbest(exclude):
        best_v = jnp.full_like(g_best, -jnp.inf)
        best_i = jnp.full_like(g_idx, -1)
        best_a = jnp.zeros_like(g_best)
        for e in range(N_EXPERTS):
            ok = masked[e] > best_v
            if exclude is not None:
                ok = jnp.logical_and(ok, exclude != e)
            best_v = jnp.where(ok, masked[e], best_v)
            best_i = jnp.where(ok, e, best_i)
            best_a = jnp.where(ok, affs[e], best_a)
        return best_i, best_a

    i1, a1 = argbest(None)
    i2, a2 = argbest(i1)
    tot = a1 + a2
    idx_ref[...] = jnp.concatenate([i1, i2], axis=0)
    wts_ref[...] = jnp.concatenate([a1 / tot, a2 / tot], axis=0)

    e_iota = lax.broadcasted_iota(jnp.int32, (N_EXPERTS, tm), 0)
    hit1 = e_iota == i1
    hit2 = e_iota == i2
    onehot = jnp.where(jnp.logical_or(hit1, hit2), 1.0, 0.0)
    r = lax.broadcasted_iota(jnp.int32, (tm, tm), 0)
    cidx = lax.broadcasted_iota(jnp.int32, (tm, tm), 1)
    before = jnp.where(r < cidx, 1.0, 0.0).astype(BF16)
    prior = base_ref[:, 0:1] + _dot(onehot.astype(BF16), before)
    rank1 = jnp.sum(jnp.where(hit1, prior, 0.0), axis=0, keepdims=True)
    rank2 = jnp.sum(jnp.where(hit2, prior, 0.0), axis=0, keepdims=True)
    rank_ref[...] = jnp.concatenate([rank1, rank2], axis=0).astype(jnp.int32)
    total = base_ref[...] + jnp.sum(onehot, axis=1, keepdims=True)
    base_ref[...] = total
    cnt_ref[...] = total


def norm_router(x, norm_w, mods, w_router_t, router_bias, n_ctx):
    t = x.shape[0]
    tm = ROW_TILE
    sel = lambda i: (0, i)
    return pl.pallas_call(
        functools.partial(_router_kernel, tm=tm, n_ctx=n_ctx),
        out_shape=(jax.ShapeDtypeStruct((t, D_MODEL // 2), jnp.uint32),
                   jax.ShapeDtypeStruct((TOP_K, t), jnp.int32),
                   jax.ShapeDtypeStruct((TOP_K, t), F32),
                   jax.ShapeDtypeStruct((TOP_K, t), jnp.int32),
                   jax.ShapeDtypeStruct((N_EXPERTS, GATE_LANES), F32)),
        grid=(t // tm,),
        in_specs=[pl.BlockSpec((tm, D_MODEL), lambda i: (i, 0)),
                  pl.BlockSpec((1, D_MODEL), lambda i: (0, 0)),
                  pl.BlockSpec((2, MOD_ROWS, D_MODEL), lambda i: (0, 0, 0)),
                  pl.BlockSpec((N_EXPERTS, D_MODEL), lambda i: (0, 0)),
                  pl.BlockSpec((N_EXPERTS, 1), lambda i: (0, 0))],
        out_specs=(pl.BlockSpec((tm, D_MODEL // 2), lambda i: (i, 0)),
                   pl.BlockSpec((TOP_K, tm), sel), pl.BlockSpec((TOP_K, tm), sel), pl.BlockSpec((TOP_K, tm), sel),
                   pl.BlockSpec((N_EXPERTS, GATE_LANES), lambda i: (0, 0))),
        scratch_shapes=[pltpu.VMEM((N_EXPERTS, GATE_LANES), F32)],
        compiler_params=_cparams(("arbitrary",)),
        name="norm_router",
    )(x, norm_w.reshape(1, D_MODEL), mods, w_router_t, router_bias.reshape(N_EXPERTS, 1))


def _routing_metadata(idx, rank, counts, n_tiles):
    tm = EXPERT_TM
    counts = counts[:, 0].astype(jnp.int32)
    padded = ((counts + tm - 1) // tm) * tm
    ends = jnp.cumsum(padded)
    starts = ends - padded
    experts_iota = jnp.arange(N_EXPERTS, dtype=jnp.int32)
    start_of = jnp.sum(jnp.where(idx[:, :, None] == experts_iota, starts, 0), axis=-1)
    pos = (start_of + rank).reshape(-1).astype(jnp.int32)
    n_used = (ends[-1] // tm).astype(jnp.int32)
    tile_start = jnp.arange(n_tiles, dtype=jnp.int32) * tm
    tile_expert = jnp.sum((tile_start[:, None] >= ends[None, :]).astype(jnp.int32), axis=1)
    last_used = jnp.sum((jnp.maximum(n_used - 1, 0) * tm >= ends).astype(jnp.int32))
    tile_expert = jnp.minimum(jnp.where(tile_start < n_used * tm, tile_expert, last_used), N_EXPERTS - 1)
    return pos, tile_expert.astype(jnp.int32), n_used.reshape(1), (starts + counts).astype(jnp.int32), \
        (padded - counts).astype(jnp.int32)


ZERO_ROWS = 64
DMA_UNROLL = 4
COMBINE_ROWS = 32
EXPERT_SUBTILES = 2


def _dispatch_kernel(pos_ref, padstart_ref, npad_ref, nused_ref, h_ref, win_ref, wout_ref, o_hbm, win_o_ref,
                     wout_o_ref, stage_ref, zero_ref, sems, *, tm, n_tok, n_rows, w_steps):
    i = pl.program_id(0)
    n = pl.num_programs(0)
    slot = lax.rem(i, 2)
    used_rows = nused_ref[0] * EXPERT_TM
    n_trail = (n_rows - used_rows) // ZERO_ROWS

    def row_copy(s, src_row, dst_row):
        return pltpu.make_async_copy(stage_ref.at[s, pl.ds(src_row, 1), :], o_hbm.at[pl.ds(dst_row, 1), :],
                                     sems.at[s])

    def pad_copy(dst_row):
        return pltpu.make_async_copy(zero_ref.at[pl.ds(0, 1), :], o_hbm.at[pl.ds(dst_row, 1), :], sems.at[2])

    def trail_copy(j):
        row = pl.multiple_of(used_rows + j * ZERO_ROWS, ZERO_ROWS)
        return pltpu.make_async_copy(zero_ref, o_hbm.at[pl.ds(row, ZERO_ROWS), :], sems.at[2])

    def pad_loop(fn, trail_fn):
        for e in range(N_EXPERTS):
            def body(r, carry, e=e):
                fn(padstart_ref[e] + r)
                return carry
            lax.fori_loop(0, npad_ref[e], body, 0)

        def trail_body(j, carry):
            trail_fn(j)
            return carry
        lax.fori_loop(0, n_trail, trail_body, 0)

    @pl.when(i == 0)
    def _():
        zero_ref[...] = jnp.zeros_like(zero_ref)
        pad_loop(lambda row: pad_copy(row).start(), lambda j: trail_copy(j).start())

    def issue(r, carry):
        tok = i * tm + r
        for k in range(TOP_K):
            row_copy(slot, r, pos_ref[k * n_tok + tok]).start(priority=k % 2)
        return carry

    def wait_tile(s):
        def body(r, carry):
            for k in range(TOP_K):
                row_copy(s, 0, 0).wait()
            return carry
        lax.fori_loop(0, tm, body, 0, unroll=DMA_UNROLL)

    @pl.when(i < w_steps)
    def _():
        win_o_ref[...] = win_ref[...].astype(BF16)
        wout_o_ref[...] = wout_ref[...].astype(BF16)

    stage_ref[slot] = h_ref[...]
    lax.fori_loop(0, tm, issue, 0, unroll=DMA_UNROLL)

    @pl.when(i > 0)
    def _():
        wait_tile(1 - slot)

    @pl.when(i == n - 1)
    def _():
        wait_tile(slot)
        pad_loop(lambda row: pad_copy(0).wait(), lambda j: trail_copy(0).wait())


def dispatch(h_packed, pos, pad_start, n_pad, n_used, n_rows, w_exp_in, w_exp_out, layer):
    t = h_packed.shape[0]
    tm = ROW_TILE
    steps = t // tm
    w_steps = 1 << (steps.bit_length() - 1)
    depth = w_exp_in.shape[0]
    win_flat = w_exp_in.reshape(depth, N_EXPERTS * D_MODEL, 2 * D_EXPERT)
    wout_flat = w_exp_out.reshape(depth, N_EXPERTS * D_EXPERT, D_MODEL)
    rows_in = win_flat.shape[1] // w_steps
    rows_out = wout_flat.shape[1] // w_steps
    slab = lambda i, *_: (jnp.minimum(i, w_steps - 1), 0)
    slab_l = lambda i, *_: (layer, jnp.minimum(i, w_steps - 1), 0)
    x_sorted, win_b, wout_b = pl.pallas_call(
        functools.partial(_dispatch_kernel, tm=tm, n_tok=t, n_rows=n_rows, w_steps=w_steps),
        out_shape=(jax.ShapeDtypeStruct((n_rows, D_MODEL // 2), jnp.uint32),
                   jax.ShapeDtypeStruct(win_flat.shape[1:], BF16),
                   jax.ShapeDtypeStruct(wout_flat.shape[1:], BF16)),
        grid_spec=pltpu.PrefetchScalarGridSpec(
            num_scalar_prefetch=4, grid=(steps,),
            in_specs=[pl.BlockSpec((tm, D_MODEL // 2), lambda i, *_: (i, 0)),
                      pl.BlockSpec((None, rows_in, 2 * D_EXPERT), slab_l),
                      pl.BlockSpec((None, rows_out, D_MODEL), slab_l)],
            out_specs=(pl.BlockSpec(memory_space=pl.ANY),
                       pl.BlockSpec((rows_in, 2 * D_EXPERT), slab),
                       pl.BlockSpec((rows_out, D_MODEL), slab)),
            scratch_shapes=[pltpu.VMEM((2, tm, D_MODEL // 2), jnp.uint32),
                            pltpu.VMEM((ZERO_ROWS, D_MODEL // 2), jnp.uint32), pltpu.SemaphoreType.DMA((3,))]),
        compiler_params=_cparams(("arbitrary",)),
        name="dispatch",
    )(pos, pad_start, n_pad, n_used, h_packed, win_flat, wout_flat)
    return (x_sorted, win_b.reshape(1, N_EXPERTS, D_MODEL, 2 * D_EXPERT),
            wout_b.reshape(1, N_EXPERTS, D_EXPERT, D_MODEL))


def _experts_kernel(texp_ref, nused_ref, x_ref, win_ref, wout_ref, o_ref):
    i = pl.program_id(0)
    half = D_MODEL // 2

    @pl.when(i < nused_ref[0])
    def _():
        sub = x_ref.shape[0] // EXPERT_SUBTILES
        for s in range(EXPERT_SUBTILES):
            rs = slice(s * sub, (s + 1) * sub)
            xp = x_ref[rs, :]
            gu = (_dot(_unpack_lo(xp).astype(BF16), win_ref[:half, :]) +
                  _dot(_unpack_hi(xp).astype(BF16), win_ref[half:, :]))
            act = (_silu(gu[:, :D_EXPERT]) * gu[:, D_EXPERT:]).astype(BF16)
            o_ref[rs, :] = _pack_halves(_dot(act, wout_ref[...]))

    @pl.when(i >= nused_ref[0])
    def _():
        o_ref[...] = jnp.zeros_like(o_ref)


def experts(x_sorted, tile_expert, n_used, w_exp_in, w_exp_out, layer):
    tm = EXPERT_TM
    n_tiles = tile_expert.shape[0]
    half = D_MODEL // 2
    return pl.pallas_call(
        _experts_kernel,
        out_shape=jax.ShapeDtypeStruct((n_tiles * tm, half), jnp.uint32),
        grid_spec=pltpu.PrefetchScalarGridSpec(
            num_scalar_prefetch=2, grid=(n_tiles,),
            in_specs=[pl.BlockSpec((tm, half), lambda i, te, nu: (jnp.minimum(i, nu[0] - 1), 0)),
                      pl.BlockSpec((None, None, D_MODEL, 2 * D_EXPERT), lambda i, te, nu: (layer, te[i], 0, 0)),
                      pl.BlockSpec((None, None, D_EXPERT, D_MODEL), lambda i, te, nu: (layer, te[i], 0, 0))],
            out_specs=pl.BlockSpec((tm, half), lambda i, te, nu: (i, 0))),
        compiler_params=_cparams(("arbitrary",)),
        name="experts",
    )(tile_expert, n_used, x_sorted, w_exp_in, w_exp_out)


def _combine_kernel(pos_ref, x_ref, wts_ref, mods_ref, nw_ref, nmods_ref, y_hbm, *rest, tm, n_ctx, n_tok, last):
    if last:
        f_ref, ybuf, sems = rest
    else:
        xo_ref, h_ref, ybuf, sems = rest
    i = pl.program_id(0)
    n = pl.num_programs(0)
    slot = lax.rem(i, 2)
    half = D_MODEL // 2

    def row_copy(src_row, s, k, r):
        return pltpu.make_async_copy(y_hbm.at[pl.ds(src_row, 1), :], ybuf.at[s, k, pl.ds(r, 1), :], sems.at[s])

    def issue(tile, s):
        def body(r, carry):
            for k in range(TOP_K):
                row_copy(pos_ref[k * n_tok + tile * tm + r], s, k, r).start(priority=k % 2)
            return carry
        lax.fori_loop(0, tm, body, 0, unroll=DMA_UNROLL)

    @pl.when(i == 0)
    def _():
        issue(0, 0)

    @pl.when(i + 1 < n)
    def _():
        issue(i + 1, 1 - slot)

    def wait_body(r, carry):
        for k in range(TOP_K):
            row_copy(0, slot, k, r).wait()
        return carry
    lax.fori_loop(0, tm, wait_body, 0, unroll=DMA_UNROLL)

    sel = jnp.where(i * tm < n_ctx, 0, 1)
    lo, hi = slice(0, half), slice(half, D_MODEL)

    def rows(b, carry):
        rs = pl.ds(pl.multiple_of(b * COMBINE_ROWS, COMBINE_ROWS), COMBINE_ROWS)
        w = wts_ref[rs, :]
        y0 = ybuf[slot, 0, rs, :]
        y1 = ybuf[slot, 1, rs, :]
        gate = mods_ref[sel, 5:6, :]
        halves = []
        for sl, unpack in ((lo, _unpack_lo), (hi, _unpack_hi)):
            moe = w[:, 0:1] * unpack(y0) + w[:, 1:2] * unpack(y1)
            halves.append(x_ref[rs, sl] + gate[:, sl] * moe)
        ssq = sum(jnp.sum(v * v, axis=-1, keepdims=True) for v in halves)
        inv = lax.rsqrt(ssq / D_MODEL + EPS)
        for v, sl in zip(halves, (lo, hi)):
            normed = v * inv * nw_ref[:, sl]
            if last:
                f_ref[rs, sl] = normed
            else:
                xo_ref[rs, sl] = v
                scale = nmods_ref[sel, 1:2, :][:, sl]
                shift = nmods_ref[sel, 0:1, :][:, sl]
                h_ref[rs, sl] = (normed * (1.0 + scale) + shift).astype(h_ref.dtype)
        return carry
    lax.fori_loop(0, tm // COMBINE_ROWS, rows, 0)


def combine(x, y_sorted, pos, wts_t, mods, next_norm_w, next_mods, n_ctx, last):
    t = x.shape[0]
    tm = CHUNK
    full = lambda i, p: (i, 0)
    if last:
        ctx_tiles = n_ctx // tm
        out_shape = jax.ShapeDtypeStruct((t - n_ctx, D_MODEL), F32)
        out_specs = pl.BlockSpec((tm, D_MODEL), lambda i, p: (jnp.maximum(i - ctx_tiles, 0), 0))
    else:
        out_shape = (jax.ShapeDtypeStruct((t, D_MODEL), F32), jax.ShapeDtypeStruct((t, D_MODEL), BF16))
        out_specs = (pl.BlockSpec((tm, D_MODEL), full), pl.BlockSpec((tm, D_MODEL), full))
    return pl.pallas_call(
        functools.partial(_combine_kernel, tm=tm, n_ctx=n_ctx, n_tok=t, last=last),
        out_shape=out_shape,
        grid_spec=pltpu.PrefetchScalarGridSpec(
            num_scalar_prefetch=1, grid=(t // tm,),
            in_specs=[pl.BlockSpec((tm, D_MODEL), full),
                      pl.BlockSpec((tm, TOP_K), full),
                      pl.BlockSpec((2, MOD_ROWS, D_MODEL), lambda i, p: (0, 0, 0)),
                      pl.BlockSpec((1, D_MODEL), lambda i, p: (0, 0)),
                      pl.BlockSpec((2, MOD_ROWS, D_MODEL), lambda i, p: (0, 0, 0)),
                      pl.BlockSpec(memory_space=pl.ANY)],
            out_specs=out_specs,
            scratch_shapes=[pltpu.VMEM((2, TOP_K, tm, D_MODEL // 2), jnp.uint32),
                            pltpu.SemaphoreType.DMA((2,))]),
        compiler_params=_cparams(("arbitrary",)),
        name="combine_last" if last else "combine",
    )(pos, x, wts_t, mods, next_norm_w.reshape(1, D_MODEL), next_mods, y_sorted)


_MAIN_PIECES = ((O_XBC, SSD_XBC), (O_GD, GATE_RANK), (O_Z, SSD_INNER), (O_FX, BRANCH_W), (O_MQ, MLSTM_QKW),
                (O_MV, BRANCH_W), (O_MO, BRANCH_W), (O_SU, BRANCH_W), (O_SV, BRANCH_W))


def _relayout_w_in_kernel(wt_ref, main_ref, gates_ref):
    col = 0
    for off, width in _MAIN_PIECES:
        main_ref[:, col:col + width] = wt_ref[off:off + width, :].T.astype(BF16)
        col += width
    tk = wt_ref.shape[1]
    pad = GATE_LANES - 2 * SSD_HEADS - 4 * MLSTM_HEADS
    narrow = jnp.concatenate([wt_ref[O_DT:O_DT + 2 * SSD_HEADS, :], wt_ref[O_MG:O_MG + 4 * MLSTM_HEADS, :],
                              jnp.zeros((pad, tk), F32)], axis=0)
    gates_ref[...] = narrow.T.astype(BF16)


def relayout_w_in(w_in):
    depth, d, d_in = w_in.shape
    tk = ROW_TILE
    return pl.pallas_call(
        _relayout_w_in_kernel,
        out_shape=(jax.ShapeDtypeStruct((depth, d, P_TOTAL), BF16),
                   jax.ShapeDtypeStruct((depth, d, GATE_LANES), BF16)),
        grid=(depth, d // tk),
        in_specs=[pl.BlockSpec((None, d_in, tk), lambda l, i: (l, 0, i))],
        out_specs=(pl.BlockSpec((None, tk, P_TOTAL), lambda l, i: (l, i, 0)),
                   pl.BlockSpec((None, tk, GATE_LANES), lambda l, i: (l, i, 0))),
        compiler_params=_cparams(("arbitrary", "arbitrary")),
        name="relayout_w_in",
    )(jnp.swapaxes(w_in, 1, 2))


def _prep_layer_params(l, p):
    pad = GATE_LANES - 2 * SSD_HEADS - 4 * MLSTM_HEADS
    gbias = jnp.concatenate([p['ssd_dt_bias'][l].reshape(-1), p['mlstm_gate_b'][l].reshape(-1),
                             jnp.zeros((pad,), F32)])
    alog = jnp.concatenate([p['ssd_a_log'][l].reshape(-1), jnp.zeros((GATE_LANES - 2 * SSD_HEADS,), F32)])
    feat = np.arange(GATE_LANES)[:, None]
    head = np.arange(SSD_INNER)[None, :] // SSD_HEAD_DIM
    pad_k = lambda w: jnp.concatenate([w, jnp.zeros((8 - CONV_K, w.shape[1]), F32)], axis=0)
    return {
        'gbias_row': gbias.reshape(1, -1), 'gbias_col': gbias.reshape(-1, 1),
        'alog_row': alog.reshape(1, -1), 'alog_col': alog.reshape(-1, 1),
        'expand_f': jnp.asarray(feat == head, BF16), 'expand_b': jnp.asarray(feat == head + SSD_HEADS, BF16),
        'ssd_conv_w': pad_k(p['ssd_conv_w'][l]), 'ssd_conv_b': p['ssd_conv_b'][l].reshape(1, -1),
        'ssd_d_e': jnp.repeat(p['ssd_d'][l], SSD_HEAD_DIM).reshape(1, -1),
        'ssd_norm_w': p['ssd_norm_w'][l].reshape(1, -1),
        'mlstm_conv_w': pad_k(p['mlstm_conv_w'][l]), 'mlstm_conv_b': p['mlstm_conv_b'][l].reshape(1, -1),
        'mlstm_norm_w': p['mlstm_norm_w'][l].reshape(1, -1),
        'sgu_norm_w': p['sgu_norm_w'][l].reshape(1, -1),
        'sgu_w': p['sgu_w'][l].astype(BF16),
        'sgu_b_t': jnp.concatenate([p['sgu_b'][l].T, jnp.zeros((CHUNK, GATE_LANES - SGU_GROUPS), F32)], axis=1),
    }


def kernel(x, c, ctx, c_ctx, ada_down, ada_up, ada_b, norm1_w, norm2_w, w_in, ssd_conv_w, ssd_conv_b,
           ssd_dt_bias, ssd_a_log, ssd_d, ssd_norm_w, mlstm_conv_w, mlstm_conv_b, mlstm_gate_b, mlstm_norm_w,
           sgu_norm_w, sgu_w, sgu_b, w_gate_up, w_branch, w_out, w_router, router_bias, w_exp_in, w_exp_out,
           final_norm_w):
    assert x.shape[0] == 1 and ctx.shape[0] == 1, "single-sequence kernel"
    w_main, w_gates = relayout_w_in(w_in)
    p = dict(w_main=w_main, w_gates=w_gates, ssd_conv_w=ssd_conv_w, ssd_conv_b=ssd_conv_b,
             ssd_dt_bias=ssd_dt_bias, ssd_a_log=ssd_a_log, ssd_d=ssd_d, ssd_norm_w=ssd_norm_w,
             mlstm_conv_w=mlstm_conv_w, mlstm_conv_b=mlstm_conv_b, mlstm_gate_b=mlstm_gate_b,
             mlstm_norm_w=mlstm_norm_w, sgu_norm_w=sgu_norm_w, sgu_w=sgu_w, sgu_b=sgu_b,
             w_gate_up=w_gate_up.astype(BF16), w_branch=w_branch.astype(BF16), w_out=w_out.astype(BF16))
    depth = w_in.shape[0]
    seq = x.shape[1]
    n_ctx = ctx.shape[1]
    t = n_ctx + seq
    assert n_ctx % ROW_TILE == 0 and seq % ROW_TILE == 0
    n_tiles = -(-(TOP_K * t) // EXPERT_TM) + N_EXPERTS

    xs = jnp.concatenate([ctx[0], x[0]], axis=0)
    cvec = jnp.concatenate([c_ctx[None], c, jnp.zeros((6, D_MODEL), F32)], axis=0)
    mods_all = ada_mods(cvec, ada_down, ada_up, ada_b)
    mods_all = mods_all[:, :2].reshape(depth, 2, N_MOD, D_MODEL)
    mods_all = jnp.concatenate([mods_all, jnp.zeros((depth, 2, MOD_ROWS - N_MOD, D_MODEL), F32)], axis=2)
    w_router_t = w_router.T

    h = norm_mod(xs, norm1_w[0], mods_all[0], n_ctx, 0, 1, BF16)
    for l in range(depth):
        lw = _prep_layer_params(l, p)
        mods = mods_all[l]
        last = l == depth - 1
        proj = matmul(h, p['w_main'], l, BF16, MM_TN)
        gates = matmul(h, p['w_gates'], l, F32, GATE_LANES)

        ys_b, xbc_conv, hm_b, qk_conv = scan_pair(proj, gates, lw, n_ctx)
        y_ssd, y_ml = scan_pair(proj, gates, lw, n_ctx, (xbc_conv, qk_conv), (ys_b, hm_b))
        fx = proj[:, P_FX:P_FX + BRANCH_W]
        y_four = jnp.concatenate([fourier_mix(fx[:n_ctx]), fourier_mix(fx[n_ctx:])], axis=0)
        y_sgu = sgu(proj, lw)

        merged = merge(proj, (y_ssd, y_four, y_ml, y_sgu), p['w_gate_up'], p['w_branch'], l)
        xs = outproj_residual(merged, p['w_out'], l, xs, mods, n_ctx, 2)

        h_packed, idx, wts, rank, counts = norm_router(xs, norm2_w[l], mods, w_router_t, router_bias, n_ctx)
        pos, tile_expert, n_used, pad_start, n_pad = _routing_metadata(idx, rank, counts, n_tiles)
        x_sorted, w_exp_in_b, w_exp_out_b = dispatch(h_packed, pos, pad_start, n_pad, n_used, n_tiles * EXPERT_TM,
                                                     w_exp_in, w_exp_out, l)
        y_sorted = experts(x_sorted, tile_expert, n_used, w_exp_in_b, w_exp_out_b, 0)
        if last:
            out = combine(xs, y_sorted, pos, wts.T, mods, final_norm_w, mods, n_ctx, True)
        else:
            xs, h = combine(xs, y_sorted, pos, wts.T, mods, norm1_w[l + 1], mods_all[l + 1], n_ctx, False)
    return out[None]
```
